```python
import math
import jax, jax.numpy as jnp
from jax import lax
import numpy as np

D_MODEL = 2048
BATCH = 4
SEQ = 2048
DEPTH = 1
DEC_BATCH = 128
DEC_SEQ = 8
PAST_LEN = 16384
PAGE_SIZE = 128

D_MIX = D_MODEL
SSD_WIDTH = D_MIX // 2
SSD_HEADDIM = 64
SSD_HEADS = SSD_WIDTH // SSD_HEADDIM
SSD_GROUPS = 2
SSD_STATE = 128
SSD_CHUNK = 128
CONV_K = 4
CONV_DIM = SSD_WIDTH + 2 * SSD_GROUPS * SSD_STATE
GM_WIDTH = D_MIX - SSD_WIDTH
GM_HEAD = 128
GM_HEADS = GM_WIDTH // GM_HEAD
GM_CHUNK = 128
D_FF = 4 * D_MODEL
D_IN_PROJ = SSD_WIDTH + CONV_DIM + SSD_HEADS + 2 * GM_WIDTH
ALPHA = (2.0 * DEPTH) ** 0.25
BETA = (8.0 * DEPTH) ** -0.25
LN_EPS = 1e-5

kernel_name = "hymba_ssd_gmlp_deepnorm_adaln_step"


def layer_norm(x, g, b):
    xf = x.astype(jnp.float32)
    mu = jnp.mean(xf, -1, keepdims=True)
    var = jnp.mean(jnp.square(xf - mu), -1, keepdims=True)
    return ((xf - mu) * lax.rsqrt(var + LN_EPS) * g + b).astype(x.dtype)


def gated_group_rmsnorm(y, z, g):
    h = (y * jax.nn.silu(z)).astype(jnp.float32)
    shp = h.shape
    h = h.reshape(shp[:-1] + (SSD_GROUPS, shp[-1] // SSD_GROUPS))
    h = h * lax.rsqrt(jnp.mean(h * h, -1, keepdims=True) + LN_EPS)
    return (h.reshape(shp) * g).astype(y.dtype)


def causal_dwconv(xbc, buf, w, b):
    xp = jnp.concatenate([buf.astype(xbc.dtype), xbc], axis=1)
    y = lax.conv_general_dilated(xp, w[:, None, :].astype(xbc.dtype), window_strides=(1,), padding='VALID',
                                 dimension_numbers=('NWC', 'WIO', 'NWC'), feature_group_count=xbc.shape[-1])
    return jax.nn.silu(y + b), xp[:, -(CONV_K - 1):]


def ssd_scan(x, dt, A, Bm, Cm, D, s0):
    b, L, H, P = x.shape
    q = math.gcd(L, SSD_CHUNK)
    nc = L // q
    rep = H // SSD_GROUPS
    f32 = jnp.float32
    xc = x.astype(f32).reshape(b, nc, q, H, P)
    dtc = dt.astype(f32).reshape(b, nc, q, H)
    Bh = jnp.repeat(Bm.astype(f32), rep, axis=2).reshape(b, nc, q, H, -1)
    Ch = jnp.repeat(Cm.astype(f32), rep, axis=2).reshape(b, nc, q, H, -1)
    acum = jnp.cumsum(dtc * A.astype(f32), axis=2)
    seg = acum[:, :, :, None, :] - acum[:, :, None, :, :]
    causal = jnp.tril(jnp.ones((q, q), bool))[:, :, None]
    decay = jnp.exp(jnp.where(causal, seg, -jnp.inf))
    xdt = xc * dtc[..., None]
    scores = jnp.einsum('bcihn,bcjhn->bcijh', Ch, Bh) * decay
    y_diag = jnp.einsum('bcijh,bcjhp->bcihp', scores, xdt)
    decay_end = jnp.exp(acum[:, :, -1:, :] - acum)
    chunk_states = jnp.einsum('bcjhn,bcjhp->bchpn', Bh * decay_end[..., None], xdt)
    chunk_decay = jnp.exp(acum[:, :, -1, :])

    def step(s, inp):
        st, dc = inp
        return dc[:, :, None, None] * s + st, s

    s_final, s_in = lax.scan(step, s0.astype(f32),
                             (jnp.moveaxis(chunk_states, 1, 0), jnp.moveaxis(chunk_decay, 1, 0)))
    s_in = jnp.moveaxis(s_in, 0, 1)
    y_off = jnp.einsum('bcihn,bchpn->bcihp', Ch, s_in) * jnp.exp(acum)[..., None]
    y = y_diag + y_off + D.astype(f32)[:, None] * xc
    return y.reshape(b, L, H, P).astype(x.dtype), s_final.astype(s0.dtype)


def chunk_spatial_gate(u, v, ln_g, ln_b, w_s, b_s):
    b, L, _ = u.shape
    q = min(GM_CHUNK, L)
    nc = L // q
    u = jax.nn.gelu(u)
    v = layer_norm(jax.nn.gelu(v), ln_g, ln_b)
    w = jnp.where(jnp.tril(jnp.ones((q, q), bool)), w_s[:, :q, :q], 0.0)
    vc = v.reshape(b, nc, q, GM_HEADS, GM_HEAD)
    mixed = jnp.einsum('hij,bcjhd->bcihd', w, vc) + jnp.transpose(b_s[:, :q])[None, None, :, :, None]
    return u * mixed.reshape(b, L, GM_WIDTH).astype(u.dtype), v


def hybrid_mixer(h, conv_buf, ssm_state, w_in, conv_w, conv_b, dt_bias, a_log, d_skip, ssd_norm_g,
                 gm_ln_g, gm_ln_b, gm_w_s, gm_b_s, w_out):
    b, L, _ = h.shape
    proj = h @ w_in
    i1 = SSD_WIDTH
    i2 = i1 + CONV_DIM
    i3 = i2 + SSD_HEADS
    i4 = i3 + GM_WIDTH
    z, xbc, dt_raw, u, v = jnp.split(proj, [i1, i2, i3, i4], axis=-1)
    xbc, new_buf = causal_dwconv(xbc, conv_buf, conv_w, conv_b)
    xs, Bm, Cm = jnp.split(xbc, [SSD_WIDTH, SSD_WIDTH + SSD_GROUPS * SSD_STATE], axis=-1)
    dt = jax.nn.softplus((dt_raw + dt_bias).astype(jnp.float32))
    A = -jnp.exp(a_log.astype(jnp.float32))
    y, s_new = ssd_scan(xs.reshape(b, L, SSD_HEADS, SSD_HEADDIM), dt, A,
                        Bm.reshape(b, L, SSD_GROUPS, SSD_STATE), Cm.reshape(b, L, SSD_GROUPS, SSD_STATE),
                        d_skip, ssm_state)
    y_ssd = gated_group_rmsnorm(y.reshape(b, L, SSD_WIDTH), z, ssd_norm_g)
    y_gm, v_rows = chunk_spatial_gate(u, v, gm_ln_g, gm_ln_b, gm_w_s, gm_b_s)
    out = jnp.concatenate([y_ssd, y_gm], axis=-1) @ w_out
    return out, new_buf, s_new, v_rows


def decoder_layer(x, c, conv_buf, ssm_state, w_mod, b_mod, w_in, conv_w, conv_b, dt_bias, a_log, d_skip,
                  ssd_norm_g, gm_ln_g, gm_ln_b, gm_w_s, gm_b_s, w_out, ln_mix_g, ln_mix_b,
                  w_ff1, w_ff2, ln_ffn_g, ln_ffn_b):
    mod = (jax.nn.silu(c) @ w_mod + b_mod)[:, None, :]
    sh_m, sc_m, g_m, sh_f, sc_f, g_f = jnp.split(mod, 6, axis=-1)
    h = x * (1 + sc_m) + sh_m
    mix, new_buf, s_new, v_rows = hybrid_mixer(h, conv_buf, ssm_state, w_in, conv_w, conv_b, dt_bias, a_log,
                                               d_skip, ssd_norm_g, gm_ln_g, gm_ln_b, gm_w_s, gm_b_s, w_out)
    x = layer_norm(ALPHA * x + (1 + g_m) * mix, ln_mix_g, ln_mix_b)
    h = x * (1 + sc_f) + sh_f
    f = jnp.square(jax.nn.relu(h @ w_ff1)) @ w_ff2
    x = layer_norm(ALPHA * x + (1 + g_f) * f, ln_ffn_g, ln_ffn_b)
    return x, new_buf, s_new, v_rows


def setup_inputs(seed: int = 0) -> dict:
    key = jax.random.key(seed)
    ks = iter(jax.random.split(key, 40))
    f32 = jnp.float32

    def nrm(shape, s):
        return jax.random.normal(next(ks), shape, f32) * s

    dt0 = jnp.exp(jax.random.uniform(next(ks), (DEPTH, SSD_HEADS), f32, math.log(1e-3), math.log(1e-1)))
    dt_bias = dt0 + jnp.log(-jnp.expm1(-dt0))
    a_log = jnp.log(jax.random.uniform(next(ks), (DEPTH, SSD_HEADS), f32, 1.0, 16.0))
    return {
        "x_prompt": nrm((BATCH, SEQ, D_MODEL), 1.0),
        "x_sample": nrm((DEC_BATCH, DEC_SEQ, D_MODEL), 1.0),
        "state_ssm": nrm((DEPTH, DEC_BATCH, SSD_HEADS, SSD_HEADDIM, SSD_STATE), 0.5),
        "state_conv": nrm((DEPTH, DEC_BATCH, CONV_K - 1, CONV_DIM), 1.0),
        "c_prompt": nrm((BATCH, D_MODEL), 1.0),
        "c_sample": nrm((DEC_BATCH, D_MODEL), 1.0),
        "ln_in_g": 1.0 + nrm((D_MODEL,), 0.02),
        "ln_in_b": nrm((D_MODEL,), 0.02),
        "w_mod": nrm((DEPTH, D_MODEL, 6 * D_MODEL), 0.5 * D_MODEL ** -0.5),
        "b_mod": nrm((DEPTH, 6 * D_MODEL), 0.02),
        "w_in": nrm((DEPTH, D_MODEL, D_IN_PROJ), D_MODEL ** -0.5),
        "conv_w": nrm((DEPTH, CONV_K, CONV_DIM), CONV_K ** -0.5),
        "conv_b": nrm((DEPTH, CONV_DIM), 0.02),
        "dt_bias": dt_bias,
        "a_log": a_log,
        "d_skip": 1.0 + nrm((DEPTH, SSD_HEADS), 0.02),
        "ssd_norm_g": 1.0 + nrm((DEPTH, SSD_WIDTH), 0.02),
        "gm_ln_g": 1.0 + nrm((DEPTH, GM_WIDTH), 0.02),
        "gm_ln_b": nrm((DEPTH, GM_WIDTH), 0.02),
        "gm_w_s": nrm((DEPTH, GM_HEADS, GM_CHUNK, GM_CHUNK), GM_CHUNK ** -0.5),
        "gm_b_s": 1.0 + nrm((DEPTH, GM_HEADS, GM_CHUNK), 0.02),
        "w_out": nrm((DEPTH, D_MIX, D_MODEL), BETA * D_MIX ** -0.5),
        "ln_mix_g": 1.0 + nrm((DEPTH, D_MODEL), 0.02),
        "ln_mix_b": nrm((DEPTH, D_MODEL), 0.02),
        "w_ff1": nrm((DEPTH, D_MODEL, D_FF), D_MODEL ** -0.5),
        "w_ff2": nrm((DEPTH, D_FF, D_MODEL), BETA * D_FF ** -0.5),
        "ln_ffn_g": 1.0 + nrm((DEPTH, D_MODEL), 0.02),
        "ln_ffn_b": nrm((DEPTH, D_MODEL), 0.02),
    }


def reference(x_prompt, x_sample, state_ssm, state_conv, c_prompt, c_sample, ln_in_g, ln_in_b,
              w_mod, b_mod, w_in, conv_w, conv_b, dt_bias, a_log, d_skip, ssd_norm_g, gm_ln_g, gm_ln_b,
              gm_w_s, gm_b_s, w_out, ln_mix_g, ln_mix_b, w_ff1, w_ff2, ln_ffn_g, ln_ffn_b):
    bp = x_prompt.shape[0]
    xp = layer_norm(x_prompt, ln_in_g, ln_in_b)
    xs = layer_norm(x_sample, ln_in_g, ln_in_b)
    ssm_p, conv_p, ssm_s, conv_s, v_s = [], [], [], [], []
    for l in range(DEPTH):
        prm = (w_mod[l], b_mod[l], w_in[l], conv_w[l], conv_b[l], dt_bias[l], a_log[l], d_skip[l],
               ssd_norm_g[l], gm_ln_g[l], gm_ln_b[l], gm_w_s[l], gm_b_s[l], w_out[l], ln_mix_g[l], ln_mix_b[l],
               w_ff1[l], w_ff2[l], ln_ffn_g[l], ln_ffn_b[l])
        zero_buf = jnp.zeros((bp, CONV_K - 1, CONV_DIM), xp.dtype)
        zero_ssm = jnp.zeros((bp, SSD_HEADS, SSD_HEADDIM, SSD_STATE), state_ssm.dtype)
        xp, buf_p, s_p, _ = decoder_layer(xp, c_prompt, zero_buf, zero_ssm, *prm)
        xs, buf_s, s_s, v_rows = decoder_layer(xs, c_sample, state_conv[l], state_ssm[l], *prm)
        ssm_p.append(s_p)
        conv_p.append(buf_p)
        ssm_s.append(s_s)
        conv_s.append(buf_s)
        v_s.append(v_rows)
    return (xp, xs, jnp.stack(ssm_p), jnp.stack(conv_p), jnp.stack(ssm_s), jnp.stack(conv_s), jnp.stack(v_s))
```

```python
import functools
import math

import jax
import jax.numpy as jnp
from jax import lax
from jax.experimental import pallas as pl
from jax.experimental.pallas import tpu as pltpu

D_MODEL = 2048
SSD_WIDTH = 1024
SSD_HEADDIM = 64
SSD_HEADS = 16
SSD_GROUPS = 2
SSD_STATE = 128
GROUP_WIDTH = SSD_WIDTH // SSD_GROUPS
CONV_K = 4
CONV_DIM = SSD_WIDTH + 2 * SSD_GROUPS * SSD_STATE
GM_WIDTH = 1024
GM_HEAD = 128
GM_HEADS = 8
D_FF = 4 * D_MODEL
CHUNK = 128
DEC_SEQ = 8
DT_PAD = 128
ALPHA = 2.0 ** 0.25
LN_EPS = 1e-5

V7X_VMEM_BYTES = 64 * 1024 * 1024
VMEM_LIMIT = 56 * 1024 * 1024

F32 = jnp.float32
BF16 = jnp.bfloat16
HIGHEST = lax.Precision.HIGHEST


def _layer_norm(x, g, b):
    mu = jnp.mean(x, axis=-1, keepdims=True)
    xc = x - mu
    var = jnp.mean(xc * xc, axis=-1, keepdims=True)
    return xc * lax.rsqrt(var + LN_EPS) * g + b


def _silu(x):
    return x / (1.0 + jnp.exp(-x))


def _gelu_tanh(x):
    c = math.sqrt(2.0 / math.pi)
    return 0.5 * x * (1.0 + jnp.tanh(c * (x + 0.044715 * (x * x * x))))


def _softplus(x):
    return jnp.maximum(x, 0.0) + jnp.log1p(jnp.exp(-jnp.abs(x)))


def _dot(a, b):
    return jnp.dot(a, b, preferred_element_type=F32)


def _dot_nt(a, b):
    return lax.dot_general(a, b, (((1,), (1,)), ((), ())), preferred_element_type=F32)


def _dot_tn(a, b):
    return lax.dot_general(a, b, (((0,), (0,)), ((), ())), preferred_element_type=F32)


def _dot_f32(a, b):
    return jnp.dot(a, b, preferred_element_type=F32, precision=HIGHEST)


def _mod_kernel(c_ref, w_ref, b_ref, o_ref):
    a = _silu(c_ref[...]).astype(BF16)
    o_ref[...] = _dot(a, w_ref[...].astype(BF16)) + b_ref[...]


def _mod_call(c_all, w_mod, b_mod):
    m = c_all.shape[0]
    n = w_mod.shape[1]
    tn = 1024
    return pl.pallas_call(
        _mod_kernel,
        grid=(n // tn,),
        in_specs=[
            pl.BlockSpec((m, D_MODEL), lambda j: (0, 0)),
            pl.BlockSpec((D_MODEL, tn), lambda j: (0, j)),
            pl.BlockSpec((1, tn), lambda j: (0, j)),
        ],
        out_specs=pl.BlockSpec((m, tn), lambda j: (0, j)),
        out_shape=jax.ShapeDtypeStruct((m, n), F32),
        compiler_params=pltpu.CompilerParams(
            dimension_semantics=("arbitrary",), vmem_limit_bytes=VMEM_LIMIT),
        name="mod",
    )(c_all, w_mod, b_mod)


def _inproj_kernel(x_ref, sh_ref, sc_ref, g_ref, b_ref, wz_ref, wx_ref, wd_ref, wu_ref, wv_ref,
                   z_ref, xbc_ref, dt_ref, u_ref, v_ref):
    xn = _layer_norm(x_ref[...], g_ref[...], b_ref[...])
    h = (xn * (1.0 + sc_ref[...]) + sh_ref[...]).astype(BF16)
    z_ref[...] = _dot(h, wz_ref[...])
    xbc_ref[...] = _dot(h, wx_ref[...])
    dt_ref[...] = _dot(h, wd_ref[...])
    u_ref[...] = _dot(h, wu_ref[...])
    v_ref[...] = _dot(h, wv_ref[...])


def _mod_specs(tm, rows_per_mod, per_token, pieces):
    if per_token:
        return [pl.BlockSpec((tm, D_MODEL), lambda i, *_, p=p: (i, p)) for p in pieces]
    tiles_per_mod = rows_per_mod // tm
    return [pl.BlockSpec((None, 1, D_MODEL), lambda i, *_, p=p: (i // tiles_per_mod, 0, p))
            for p in pieces]


def _resident(shape):
    nd = len(shape)
    return pl.BlockSpec(shape, lambda *_: (0,) * nd, pipeline_mode=pl.Buffered(1))


def _inproj_call(x2d, mod, rows_per_mod, per_token, ln_g, ln_b, w_pieces, tm):
    t = x2d.shape[0]
    widths = [w.shape[1] for w in w_pieces]
    row = lambda i: (i, 0)
    return pl.pallas_call(
        _inproj_kernel,
        grid=(t // tm,),
        in_specs=[pl.BlockSpec((tm, D_MODEL), row)]
        + _mod_specs(tm, rows_per_mod, per_token, (0, 1))
        + [_resident((1, D_MODEL)), _resident((1, D_MODEL))]
        + [_resident(w.shape) for w in w_pieces],
        out_specs=[pl.BlockSpec((tm, n), row) for n in widths],
        out_shape=[jax.ShapeDtypeStruct((t, n), F32) for n in widths],
        compiler_params=pltpu.CompilerParams(
            dimension_semantics=("arbitrary",), vmem_limit_bytes=VMEM_LIMIT),
        name="in_proj",
    )(x2d, mod, mod, ln_g, ln_b, *w_pieces)


def _head_cols(a, h):
    return a[:, h:h + 1]


def _ssd_gmlp_block(*, xc, acum, tot, dt, d_row, e_exp, mask, state_t, w_s_ref, bsb, u, v,
                    gm_g, gm_b):
    rows = xc.shape[0]
    xs = xc[:, :SSD_WIDTH]
    bm = xc[:, SSD_WIDTH:SSD_WIDTH + SSD_GROUPS * SSD_STATE]
    cm = xc[:, SSD_WIDTH + SSD_GROUPS * SSD_STATE:]

    acum_t = acum.T
    dt_t = dt.T
    w_end = jnp.exp(tot - acum) * dt
    xw = xs * _dot_f32(w_end, e_exp)
    eacum = jnp.exp(acum)

    xs_b = xs.astype(BF16)
    lane = lax.broadcasted_iota(jnp.int32, (rows, 2 * SSD_HEADDIM), 1)
    low_half = lane < SSD_HEADDIM

    scores = []
    for g in range(SSD_GROUPS):
        cg = cm[:, g * SSD_STATE:(g + 1) * SSD_STATE].astype(BF16)
        bg = bm[:, g * SSD_STATE:(g + 1) * SSD_STATE].astype(BF16)
        scores.append(_dot_nt(cg, bg))

    heads_per_group = SSD_HEADS // SSD_GROUPS
    y_pairs = []
    for pair in range(SSD_HEADS // 2):
        sl = slice(pair * 2 * SSD_HEADDIM, (pair + 1) * 2 * SSD_HEADDIM)
        x_pair = xs_b[:, sl]
        zero = jnp.zeros_like(x_pair)
        if state_t is not None:
            s_pair = state_t[:, sl].astype(BF16)
        acc = None
        for k in range(2):
            h = 2 * pair + k
            g = h // heads_per_group
            seg = _head_cols(acum, h) - acum_t[h:h + 1, :]
            m = jnp.where(mask, scores[g] * jnp.exp(seg) * dt_t[h:h + 1, :], 0.0)
            keep = low_half if k == 0 else jnp.logical_not(low_half)
            rhs = jnp.where(keep, x_pair, zero)
            lhs = m.astype(BF16)
            if state_t is not None:
                c_sc = cm[:, g * SSD_STATE:(g + 1) * SSD_STATE] * _head_cols(eacum, h)
                lhs = jnp.concatenate([lhs, c_sc.astype(BF16)], axis=1)
                rhs = jnp.concatenate([rhs, jnp.where(keep, s_pair, zero)], axis=0)
            part = _dot(lhs, rhs)
            acc = part if acc is None else acc + part
        y_pairs.append(acc)
    y = jnp.concatenate(y_pairs, axis=1) + d_row * xs

    ug = _gelu_tanh(u)
    vn = _layer_norm(_gelu_tanh(v), gm_g, gm_b)
    vn_b = vn.astype(BF16)
    mixed = []
    for h in range(GM_HEADS):
        w = jnp.where(mask, w_s_ref[h], 0.0).astype(BF16)
        mixed.append(_dot(w, vn_b[:, h * GM_HEAD:(h + 1) * GM_HEAD]))
    y_gm = ug * (jnp.concatenate(mixed, axis=1) + bsb)
    return y, xw, eacum, y_gm, vn


def _gated_rmsnorm(y, z, norm_g):
    hg = y * _silu(z)
    parts = []
    for g in range(SSD_GROUPS):
        hh = hg[:, g * GROUP_WIDTH:(g + 1) * GROUP_WIDTH]
        ms = jnp.mean(hh * hh, axis=-1, keepdims=True)
        parts.append(hh * lax.rsqrt(ms + LN_EPS))
    return jnp.concatenate(parts, axis=1) * norm_g


def _conv_silu(taps, conv_w_ref, conv_b):
    acc = conv_b + conv_w_ref[0:1, :] * taps[0]
    for k in range(1, CONV_K):
        acc = acc + conv_w_ref[k:k + 1, :] * taps[k]
    return _silu(acc)


def _prompt_mixer_kernel(z_ref, xbc_ref, dt_ref, u_ref, v_ref, conv_w_ref, conv_b_ref, dtb_ref,
                         alog_ref, d_ref, e_ref, ng_ref, gg_ref, gb_ref, ws_ref, bsb_ref,
                         y_ref, ssm_ref, xpad_ref, st_ref):
    c = pl.program_id(1)
    nc = pl.num_programs(1)

    @pl.when(c == 0)
    def _():
        xpad_ref[0:8, :] = jnp.zeros((8, CONV_DIM), F32)
        st_ref[...] = jnp.zeros_like(st_ref)

    xbc = xbc_ref[...]
    xpad_ref[8:8 + CHUNK, :] = xbc
    taps = [xpad_ref[8 - (CONV_K - 1 - k):8 - (CONV_K - 1 - k) + CHUNK, :] for k in range(CONV_K - 1)]
    taps.append(xbc)
    xc = _conv_silu(taps, conv_w_ref, conv_b_ref[...])
    xpad_ref[0:8, :] = xbc[CHUNK - 8:, :]

    dt = _softplus(dt_ref[...] + dtb_ref[...])
    a = dt * (-jnp.exp(alog_ref[...]))
    ri = lax.broadcasted_iota(jnp.int32, (CHUNK, CHUNK), 0)
    ci = lax.broadcasted_iota(jnp.int32, (CHUNK, CHUNK), 1)
    causal = ci <= ri
    acum = _dot_f32(causal.astype(F32), a)
    tot = jnp.broadcast_to(acum[CHUNK - 1:CHUNK, :], (CHUNK, DT_PAD))

    state_t = st_ref[...]
    y, xw, _, y_gm, _ = _ssd_gmlp_block(
        xc=xc, acum=acum, tot=tot, dt=dt, d_row=d_ref[...], e_exp=e_ref[...], mask=causal,
        state_t=state_t, w_s_ref=ws_ref, bsb=bsb_ref[...], u=u_ref[...], v=v_ref[...],
        gm_g=gg_ref[...], gm_b=gb_ref[...])

    cd = jnp.exp(_dot_f32(tot[0:8, :], e_ref[...]))[0:1, :]
    bm = xc[:, SSD_WIDTH:SSD_WIDTH + SSD_GROUPS * SSD_STATE].astype(BF16)
    xw_b = xw.astype(BF16)
    upd = [_dot_tn(bm[:, g * SSD_STATE:(g + 1) * SSD_STATE],
                   xw_b[:, g * GROUP_WIDTH:(g + 1) * GROUP_WIDTH]) for g in range(SSD_GROUPS)]
    new_state = state_t * cd + jnp.concatenate(upd, axis=1)
    st_ref[...] = new_state

    y_ssd = _gated_rmsnorm(y, z_ref[...], ng_ref[...])
    y_ref[...] = jnp.concatenate([y_ssd, y_gm], axis=1).astype(BF16)

    @pl.when(c == nc - 1)
    def _():
        ssm_ref[...] = new_state.T


def _prompt_mixer_call(z, xbc, dt, u, v, consts, batch, seq):
    nc = seq // CHUNK
    row = lambda b, c: (b * nc + c, 0)
    t = z.shape[0]
    const_specs = [_resident(a.shape) for a in consts]
    return pl.pallas_call(
        _prompt_mixer_kernel,
        grid=(batch, nc),
        in_specs=[pl.BlockSpec((CHUNK, SSD_WIDTH), row),
                  pl.BlockSpec((CHUNK, CONV_DIM), row),
                  pl.BlockSpec((CHUNK, DT_PAD), row),
                  pl.BlockSpec((CHUNK, GM_WIDTH), row),
                  pl.BlockSpec((CHUNK, GM_WIDTH), row)] + const_specs,
        out_specs=[pl.BlockSpec((CHUNK, 2 * SSD_WIDTH), row),
                   pl.BlockSpec((None, SSD_WIDTH, SSD_STATE), lambda b, c: (b, 0, 0))],
        out_shape=[jax.ShapeDtypeStruct((t, 2 * SSD_WIDTH), BF16),
                   jax.ShapeDtypeStruct((batch, SSD_WIDTH, SSD_STATE), F32)],
        scratch_shapes=[pltpu.VMEM((8 + CHUNK, CONV_DIM), F32),
                        pltpu.VMEM((SSD_STATE, SSD_WIDTH), F32)],
        compiler_params=pltpu.CompilerParams(
            dimension_semantics=("arbitrary", "arbitrary"), vmem_limit_bytes=VMEM_LIMIT),
        name="mixer_prompt",
    )(z, xbc, dt, u, v, *consts)


SAMPLE_BB = CHUNK // DEC_SEQ


def _seg_cumsum(a, t):
    k = 1
    while k < DEC_SEQ:
        a = a + jnp.where(t >= k, pltpu.roll(a, k, 0), 0.0)
        k *= 2
    return a


def _seg_last(a, t):
    rows = a.shape[0]
    x = jnp.where(t == DEC_SEQ - 1, a, 0.0)
    k = 1
    while k < DEC_SEQ:
        x = x + pltpu.roll(x, rows - k, 0)
        k *= 2
    return x


def _split3(x):
    hi = x.astype(BF16)
    r1 = x - hi.astype(F32)
    mid = r1.astype(BF16)
    lo = (r1 - mid.astype(F32)).astype(BF16)
    return hi, mid, lo


def _sample_mixer_kernel(z_ref, xbc_ref, buf_ref, dt_ref, u_ref, v_ref, s_ref, conv_w_ref,
                         conv_b_ref, dtb_ref, alog_ref, d_ref, e_ref, ng_ref, gg_ref, gb_ref,
                         ws_ref, bsb_ref,
                         y_ref, snew_ref, vn_ref,
                         c_scr, b_scr, xw_scr, aux_scr, yoff_scr):
    rows = CHUNK
    tcol = lax.broadcasted_iota(jnp.int32, (rows, 1), 0) % DEC_SEQ

    xbc = xbc_ref[...]
    buf = buf_ref[...]
    taps = []
    for k in range(CONV_K - 1):
        back = CONV_K - 1 - k
        taps.append(jnp.where(tcol >= back, pltpu.roll(xbc, back, 0),
                              pltpu.roll(buf, rows - DEC_SEQ + back, 0)))
    taps.append(xbc)
    xc = _conv_silu(taps, conv_w_ref, conv_b_ref[...])

    dt = _softplus(dt_ref[...] + dtb_ref[...])
    a = dt * (-jnp.exp(alog_ref[...]))
    acum = _seg_cumsum(a, tcol)
    tot = _seg_last(acum, tcol)

    ri = lax.broadcasted_iota(jnp.int32, (rows, rows), 0)
    ci = lax.broadcasted_iota(jnp.int32, (rows, rows), 1)
    mask = jnp.logical_and(ci <= ri, (ci // DEC_SEQ) == (ri // DEC_SEQ))

    y, xw, eacum, y_gm, vn = _ssd_gmlp_block(
        xc=xc, acum=acum, tot=tot, dt=dt, d_row=d_ref[...], e_exp=e_ref[...], mask=mask,
        state_t=None, w_s_ref=ws_ref, bsb=bsb_ref[...], u=u_ref[...], v=v_ref[...],
        gm_g=gg_ref[...], gm_b=gb_ref[...])
    vn_ref[...] = vn

    tfull = lax.broadcasted_iota(jnp.int32, (rows, SSD_WIDTH), 0) % DEC_SEQ
    dcx = jnp.exp(_dot_f32(tot, e_ref[...]))
    hi, mid, lo = _split3(dcx)
    aux_scr[...] = jnp.where(tfull == 0, hi.astype(F32),
                             jnp.where(tfull == 1, mid.astype(F32),
                                       jnp.where(tfull == 2, lo.astype(F32), 0.0)))
    xw_scr[...] = xw
    b_scr[...] = xc[:, SSD_WIDTH:SSD_WIDTH + SSD_GROUPS * SSD_STATE]
    c_scr[...] = xc[:, SSD_WIDTH + SSD_GROUPS * SSD_STATE:]

    r8 = lax.broadcasted_iota(jnp.int32, (DEC_SEQ, 2 * SSD_STATE), 0)
    l8 = lax.broadcasted_iota(jnp.int32, (DEC_SEQ, 2 * SSD_STATE), 1)
    ones_part = jnp.where(jnp.logical_and(r8 < 3, l8 >= SSD_STATE), 1.0, 0.0)
    zeros_b = jnp.zeros((DEC_SEQ, SSD_STATE), F32)

    def per_seq(b, carry):
        r0 = pl.multiple_of(b * DEC_SEQ, DEC_SEQ)
        rsl = pl.ds(r0, DEC_SEQ)
        for g in range(SSD_GROUPS):
            gs = slice(g * GROUP_WIDTH, (g + 1) * GROUP_WIDTH)
            ns = slice(g * SSD_STATE, (g + 1) * SSD_STATE)
            s_bg = s_ref[b, gs, :]
            yoff_scr[rsl, gs] = _dot_nt(c_scr[rsl, ns].astype(BF16), s_bg.astype(BF16))
            lhs = jnp.concatenate([xw_scr[rsl, gs], aux_scr[rsl, gs]], axis=0)
            rhs = jnp.concatenate(
                [jnp.concatenate([b_scr[rsl, ns], zeros_b], axis=1), ones_part], axis=0)
            res = _dot_tn(lhs.astype(BF16), rhs.astype(BF16))
            snew_ref[b, gs, :] = res[:, SSD_STATE:] * s_bg + res[:, :SSD_STATE]
        return carry

    lax.fori_loop(0, SAMPLE_BB, per_seq, 0)

    y = y + yoff_scr[...] * jnp.exp(_dot_f32(acum, e_ref[...]))
    y_ssd = _gated_rmsnorm(y, z_ref[...], ng_ref[...])
    y_ref[...] = jnp.concatenate([y_ssd, y_gm], axis=1).astype(BF16)


def _sample_mixer_call(z, xbc, buf8, dt, u, v, state, consts):
    t = z.shape[0]
    nb = t // CHUNK
    row = lambda i: (i, 0)
    st_spec = pl.BlockSpec((SAMPLE_BB, SSD_WIDTH, SSD_STATE), lambda i: (i, 0, 0))
    const_specs = [_resident(a.shape) for a in consts]
    return pl.pallas_call(
        _sample_mixer_kernel,
        grid=(nb,),
        in_specs=[pl.BlockSpec((CHUNK, SSD_WIDTH), row),
                  pl.BlockSpec((CHUNK, CONV_DIM), row),
                  pl.BlockSpec((CHUNK, CONV_DIM), row),
                  pl.BlockSpec((CHUNK, DT_PAD), row),
                  pl.BlockSpec((CHUNK, GM_WIDTH), row),
                  pl.BlockSpec((CHUNK, GM_WIDTH), row),
                  st_spec] + const_specs,
        out_specs=[pl.BlockSpec((CHUNK, 2 * SSD_WIDTH), row), st_spec,
                   pl.BlockSpec((CHUNK, GM_WIDTH), row)],
        out_shape=[jax.ShapeDtypeStruct((t, 2 * SSD_WIDTH), BF16),
                   jax.ShapeDtypeStruct(state.shape, F32),
                   jax.ShapeDtypeStruct((t, GM_WIDTH), F32)],
        scratch_shapes=[pltpu.VMEM((CHUNK, SSD_GROUPS * SSD_STATE), F32),
                        pltpu.VMEM((CHUNK, SSD_GROUPS * SSD_STATE), F32),
                        pltpu.VMEM((CHUNK, SSD_WIDTH), F32),
                        pltpu.VMEM((CHUNK, SSD_WIDTH), F32),
                        pltpu.VMEM((CHUNK, SSD_WIDTH), F32)],
        compiler_params=pltpu.CompilerParams(
            dimension_semantics=("arbitrary",), vmem_limit_bytes=VMEM_LIMIT),
        name="mixer_sample",
    )(z, xbc, buf8, dt, u, v, state, *consts)


def _outln_kernel(x_ref, y_ref, g_ref, ing_ref, inb_ref, w_ref, lg_ref, lb_ref, o_ref):
    xn = _layer_norm(x_ref[...], ing_ref[...], inb_ref[...])
    mix = _dot(y_ref[...], w_ref[...])
    o_ref[...] = _layer_norm(ALPHA * xn + (1.0 + g_ref[...]) * mix, lg_ref[...], lb_ref[...])


def _outln_call(x2d, ymix, mod, rows_per_mod, per_token, ln_in_g, ln_in_b, w_out, ln_g, ln_b, tm):
    t = x2d.shape[0]
    row = lambda i: (i, 0)
    return pl.pallas_call(
        _outln_kernel,
        grid=(t // tm,),
        in_specs=[pl.BlockSpec((tm, D_MODEL), row), pl.BlockSpec((tm, D_MODEL), row)]
        + _mod_specs(tm, rows_per_mod, per_token, (2,))
        + [_resident((1, D_MODEL)), _resident((1, D_MODEL)), _resident(w_out.shape),
           _resident((1, D_MODEL)), _resident((1, D_MODEL))],
        out_specs=pl.BlockSpec((tm, D_MODEL), row),
        out_shape=jax.ShapeDtypeStruct((t, D_MODEL), F32),
        compiler_params=pltpu.CompilerParams(
            dimension_semantics=("arbitrary",), vmem_limit_bytes=VMEM_LIMIT),
        name="out_ln",
    )(x2d, ymix, mod, ln_in_g, ln_in_b, w_out, ln_g, ln_b)


def _ffn_kernel(x_ref, sh_ref, sc_ref, g_ref, w1_ref, w2_ref, lg_ref, lb_ref, o_ref, h_scr):
    j = pl.program_id(1)
    nj = pl.num_programs(1)

    @pl.when(j == 0)
    def _():
        h_scr[...] = (x_ref[...] * (1.0 + sc_ref[...]) + sh_ref[...]).astype(BF16)

    a = _dot(h_scr[...], w1_ref[...])
    a = jnp.maximum(a, 0.0)
    part = _dot((a * a).astype(BF16), w2_ref[...])

    @pl.when(j == 0)
    def _():
        o_ref[...] = part

    @pl.when(j > 0)
    def _():
        o_ref[...] += part

    @pl.when(j == nj - 1)
    def _():
        o_ref[...] = _layer_norm(ALPHA * x_ref[...] + (1.0 + g_ref[...]) * o_ref[...],
                                 lg_ref[...], lb_ref[...])


def _ffn_call(x1, mod, rows_per_mod, per_token, w1, w2, ln_g, ln_b, tm, tf):
    t = x1.shape[0]
    row = lambda i, j: (i, 0)
    return pl.pallas_call(
        _ffn_kernel,
        grid=(t // tm, D_FF // tf),
        in_specs=[pl.BlockSpec((tm, D_MODEL), row)]
        + _mod_specs(tm, rows_per_mod, per_token, (3, 4, 5))
        + [pl.BlockSpec((D_MODEL, tf), lambda i, j: (0, j)),
           pl.BlockSpec((tf, D_MODEL), lambda i, j: (j, 0)),
           _resident((1, D_MODEL)), _resident((1, D_MODEL))],
        out_specs=pl.BlockSpec((tm, D_MODEL), row),
        out_shape=jax.ShapeDtypeStruct((t, D_MODEL), F32),
        scratch_shapes=[pltpu.VMEM((tm, D_MODEL), BF16)],
        compiler_params=pltpu.CompilerParams(
            dimension_semantics=("arbitrary", "arbitrary"), vmem_limit_bytes=VMEM_LIMIT),
        name="ffn",
    )(x1, mod, mod, mod, w1, w2, ln_g, ln_b)


def _pad_cols(a, n):
    return jnp.pad(a, ((0, 0), (0, n - a.shape[1])))


def kernel(x_prompt, x_sample, state_ssm, state_conv, c_prompt, c_sample, ln_in_g, ln_in_b, w_mod, b_mod, w_in, conv_w, conv_b, dt_bias, a_log, d_skip, ssd_norm_g, gm_ln_g, gm_ln_b, gm_w_s, gm_b_s, w_out, ln_mix_g, ln_mix_b, w_ff1, w_ff2, ln_ffn_g, ln_ffn_b):
    depth = w_mod.shape[0]
    assert depth == 1
    bp, seq, _ = x_prompt.shape
    bs, dec, _ = x_sample.shape
    assert dec == DEC_SEQ and seq % CHUNK == 0 and (bs * dec) % CHUNK == 0

    r1 = lambda a: a.reshape(1, -1)
    ln_in_g2, ln_in_b2 = r1(ln_in_g), r1(ln_in_b)
    l = 0

    i1 = SSD_WIDTH
    i2 = i1 + CONV_DIM
    i3 = i2 + SSD_HEADS
    i4 = i3 + GM_WIDTH
    w_in_b = w_in[l].astype(BF16)
    w_pieces = [w_in_b[:, :i1], w_in_b[:, i1:i2], _pad_cols(w_in_b[:, i2:i3], DT_PAD),
                w_in_b[:, i3:i4], w_in_b[:, i4:]]
    w_out_b = w_out[l].astype(BF16)
    w1_b = w_ff1[l].astype(BF16)
    w2_b = w_ff2[l].astype(BF16)

    head_of_chan = jnp.arange(SSD_WIDTH, dtype=jnp.int32) // SSD_HEADDIM
    e_exp = (jnp.arange(DT_PAD, dtype=jnp.int32)[:, None] == head_of_chan[None, :]).astype(F32)
    d_row = r1(jnp.repeat(d_skip[l], SSD_HEADDIM))
    dtb = _pad_cols(r1(dt_bias[l]), DT_PAD)
    alog = _pad_cols(r1(a_log[l]), DT_PAD)
    mixer_consts = [conv_w[l], r1(conv_b[l]), dtb, alog, d_row, e_exp, r1(ssd_norm_g[l]),
                    r1(gm_ln_g[l]), r1(gm_ln_b[l])]
    bsb_p = jnp.repeat(gm_b_s[l].T, GM_HEAD, axis=1)
    reps = CHUNK // DEC_SEQ
    ws_s = jnp.tile(gm_w_s[l][:, :DEC_SEQ, :DEC_SEQ], (1, reps, reps))
    bsb_s = jnp.tile(jnp.repeat(gm_b_s[l][:, :DEC_SEQ].T, GM_HEAD, axis=1), (reps, 1))

    n_c = bp + bs
    c_all = jnp.concatenate([c_prompt, c_sample], axis=0)
    c_all = jnp.pad(c_all, ((0, (-n_c) % 8), (0, 0)))
    mod = _mod_call(c_all, w_mod[l], r1(b_mod[l]))
    mod_p = mod[:bp].reshape(bp, 1, 6 * D_MODEL)
    mod_s = jnp.repeat(mod[bp:n_c], DEC_SEQ, axis=0)

    xp2 = x_prompt.reshape(bp * seq, D_MODEL)
    z, xbc, dtr, u, v = _inproj_call(xp2, mod_p, seq, False, ln_in_g2, ln_in_b2, w_pieces, tm=256)
    ymix, ssm_p = _prompt_mixer_call(z, xbc, dtr, u, v, mixer_consts + [gm_w_s[l], bsb_p], bp, seq)
    x1 = _outln_call(xp2, ymix, mod_p, seq, False, ln_in_g2, ln_in_b2, w_out_b,
                     r1(ln_mix_g[l]), r1(ln_mix_b[l]), tm=512)
    yp = _ffn_call(x1, mod_p, seq, False, w1_b, w2_b, r1(ln_ffn_g[l]), r1(ln_ffn_b[l]),
                   tm=512, tf=1024)
    conv_p = xbc.reshape(bp, seq, CONV_DIM)[:, seq - (CONV_K - 1):, :]

    xs2 = x_sample.reshape(bs * dec, D_MODEL)
    zs, xbcs, dtrs, us, vs = _inproj_call(xs2, mod_s, None, True, ln_in_g2, ln_in_b2, w_pieces, tm=256)
    buf8 = jnp.pad(state_conv[l], ((0, 0), (DEC_SEQ - (CONV_K - 1), 0), (0, 0)))
    buf8 = buf8.reshape(bs * dec, CONV_DIM)
    st_in = state_ssm[l].reshape(bs, SSD_WIDTH, SSD_STATE)
    ymix_s, ssm_s, vn_s = _sample_mixer_call(zs, xbcs, buf8, dtrs, us, vs, st_in,
                                             mixer_consts + [ws_s, bsb_s])
    x1s = _outln_call(xs2, ymix_s, mod_s, None, True, ln_in_g2, ln_in_b2, w_out_b,
                      r1(ln_mix_g[l]), r1(ln_mix_b[l]), tm=512)
    ys = _ffn_call(x1s, mod_s, None, True, w1_b, w2_b, r1(ln_ffn_g[l]), r1(ln_ffn_b[l]),
                   tm=256, tf=1024)
    conv_s = xbcs.reshape(bs, dec, CONV_DIM)[:, dec - (CONV_K - 1):, :]

    return (yp.reshape(bp, seq, D_MODEL),
            ys.reshape(bs, dec, D_MODEL),
            ssm_p.reshape(1, bp, SSD_HEADS, SSD_HEADDIM, SSD_STATE),
            conv_p[None],
            ssm_s.reshape(1, bs, SSD_HEADS, SSD_HEADDIM, SSD_STATE),
            conv_s[None],
            vn_s.reshape(1, bs, dec, GM_WIDTH))
```

```python
import functools
import math

import jax
import jax.numpy as jnp
from jax import lax
from jax.experimental import pallas as pl
from jax.experimental.pallas import tpu as pltpu

D_MODEL = 2048
SSD_WIDTH = 1024
SSD_HEADDIM = 64
SSD_HEADS = 16
SSD_GROUPS = 2
SSD_STATE = 128
GROUP_WIDTH = SSD_WIDTH // SSD_GROUPS
CONV_K = 4
CONV_DIM = SSD_WIDTH + 2 * SSD_GROUPS * SSD_STATE
GM_WIDTH = 1024
GM_HEAD = 128
GM_HEADS = 8
D_FF = 4 * D_MODEL
FF_TILE = 1024
N_FF_TILES = D_FF // FF_TILE
CHUNK = 128
DEC_SEQ = 8
DT_PAD = 128
SUBLANES = 8
LANES = 128
ALPHA = 2.0 ** 0.25
LN_EPS = 1e-5

V7X_VMEM_BYTES = 64 * 1024 * 1024
VMEM_LIMIT = V7X_VMEM_BYTES - 8 * 1024 * 1024

F32 = jnp.float32
BF16 = jnp.bfloat16


def _layer_norm(x, g, b):
    mu = jnp.mean(x, axis=-1, keepdims=True)
    xc = x - mu
    var = jnp.mean(xc * xc, axis=-1, keepdims=True)
    return xc * lax.rsqrt(var + LN_EPS) * g + b


def _silu(x):
    return x / (1.0 + jnp.exp(-x))


def _gelu_tanh(x):
    c = math.sqrt(2.0 / math.pi)
    return 0.5 * x * (1.0 + jnp.tanh(c * (x + 0.044715 * (x * x * x))))


def _softplus(x):
    return jnp.maximum(x, 0.0) + jnp.log1p(jnp.exp(-jnp.abs(x)))


def _dot(a, b):
    return jnp.dot(a, b, preferred_element_type=F32)


def _dot_nt(a, b):
    return lax.dot_general(a, b, (((1,), (1,)), ((), ())), preferred_element_type=F32)


def _dot_tn(a, b):
    return lax.dot_general(a, b, (((0,), (0,)), ((), ())), preferred_element_type=F32)


def _split3(x):
    hi = x.astype(BF16)
    r1 = x - hi.astype(F32)
    mid = r1.astype(BF16)
    lo = (r1 - mid.astype(F32)).astype(BF16)
    return hi, mid, lo


def _expand(x, sel3):
    return _dot(jnp.concatenate(_split3(x), axis=1), sel3)


def _mod_rows(ref, p3_ref):
    if p3_ref is None:
        return ref[...]
    parts = list(_split3(ref[...]))
    pad = p3_ref.shape[1] - 3 * ref.shape[0]
    if pad:
        parts.append(jnp.zeros((pad, ref.shape[1]), BF16))
    return _dot(p3_ref[...], jnp.concatenate(parts, axis=0))


def _resident(shape):
    nd = len(shape)
    return pl.BlockSpec(shape, lambda *_: (0,) * nd, pipeline_mode=pl.Buffered(1))


def _mod_specs(tm, rows_per_mod, per_seq, pieces):
    if per_seq:
        return [pl.BlockSpec((tm // DEC_SEQ, D_MODEL), lambda i, *_, p=p: (i, p)) for p in pieces]
    tiles_per_mod = rows_per_mod // tm
    return [pl.BlockSpec((None, 1, D_MODEL), lambda i, *_, p=p: (i // tiles_per_mod, 0, p))
            for p in pieces]


def _repeat_matrix3(tm):
    nb = tm // DEC_SEQ
    sel = (jnp.arange(tm, dtype=jnp.int32)[:, None] // DEC_SEQ
           == jnp.arange(nb, dtype=jnp.int32)[None, :]).astype(BF16)
    return _pad_cols(jnp.concatenate([sel, sel, sel], axis=1), _repeat_k(tm))


def _repeat_k(tm):
    return -(-(3 * tm // DEC_SEQ) // LANES) * LANES


def _pad_cols(a, n):
    return jnp.pad(a, ((0, 0), (0, n - a.shape[1])))


def _params(sem):
    return pltpu.CompilerParams(dimension_semantics=sem, vmem_limit_bytes=VMEM_LIMIT)


def _mod_kernel(c_ref, w_ref, b_ref, o_ref):
    a = _silu(c_ref[...]).astype(BF16)
    o_ref[...] = _dot(a, w_ref[...].astype(BF16)) + b_ref[...]


def _mod_call(c_all, w_mod, b_mod):
    m = c_all.shape[0]
    n = w_mod.shape[1]
    tn = 1024
    return pl.pallas_call(
        _mod_kernel,
        grid=(n // tn,),
        in_specs=[
            pl.BlockSpec((m, D_MODEL), lambda j: (0, 0)),
            pl.BlockSpec((D_MODEL, tn), lambda j: (0, j)),
            pl.BlockSpec((1, tn), lambda j: (0, j)),
        ],
        out_specs=pl.BlockSpec((m, tn), lambda j: (0, j)),
        out_shape=jax.ShapeDtypeStruct((m, n), F32),
        compiler_params=_params(("arbitrary",)),
        name="mod",
    )(c_all, w_mod, b_mod)


def _inproj_kernel(*refs, per_seq, cast_ffn):
    it = iter(refs)
    x_ref, sh_ref, sc_ref = next(it), next(it), next(it)
    p3_ref = next(it) if per_seq else None
    g_ref, b_ref = next(it), next(it)
    w_refs = [next(it) for _ in range(5)]
    if cast_ffn:
        w1f_ref, w2f_ref = next(it), next(it)
    o_refs = [next(it) for _ in range(5)]

    xn = _layer_norm(x_ref[...], g_ref[...], b_ref[...])
    h = (xn * (1.0 + _mod_rows(sc_ref, p3_ref)) + _mod_rows(sh_ref, p3_ref)).astype(BF16)
    for w_ref, o_ref in zip(w_refs, o_refs):
        o_ref[...] = _dot(h, w_ref[...])

    if cast_ffn:
        w1b_ref, w2b_ref = next(it), next(it)
        for c in range(N_FF_TILES):
            w1b_ref[c] = w1f_ref[:, c * FF_TILE:(c + 1) * FF_TILE].astype(BF16)
        w2b_ref[...] = w2f_ref[...].astype(BF16)


def _inproj_call(x2d, mod, rows_per_mod, per_seq, ln_g, ln_b, w_pieces, tm, ffn_w=None):
    t = x2d.shape[0]
    steps = t // tm
    widths = [w.shape[1] for w in w_pieces]
    row = lambda i: (i, 0)
    cast_ffn = ffn_w is not None
    operands = [x2d, mod, mod]
    in_specs = [pl.BlockSpec((tm, D_MODEL), row)] + _mod_specs(tm, rows_per_mod, per_seq, (0, 1))
    if per_seq:
        operands.append(_repeat_matrix3(tm))
        in_specs.append(_resident((tm, _repeat_k(tm))))
    operands += [ln_g, ln_b, *w_pieces]
    in_specs += [_resident((1, D_MODEL)), _resident((1, D_MODEL))] + [_resident(w.shape) for w in w_pieces]
    out_specs = [pl.BlockSpec((tm, n), row) for n in widths]
    out_shape = [jax.ShapeDtypeStruct((t, n), F32) for n in widths]
    if cast_ffn:
        r1, r2 = D_MODEL // steps, D_FF // steps
        operands += list(ffn_w)
        in_specs += [pl.BlockSpec((r1, D_FF), row), pl.BlockSpec((r2, D_MODEL), row)]
        out_specs += [pl.BlockSpec((N_FF_TILES, r1, FF_TILE), lambda i: (0, i, 0)),
                      pl.BlockSpec((r2, D_MODEL), row)]
        out_shape += [jax.ShapeDtypeStruct((N_FF_TILES, D_MODEL, FF_TILE), BF16),
                      jax.ShapeDtypeStruct((D_FF, D_MODEL), BF16)]
    return pl.pallas_call(
        functools.partial(_inproj_kernel, per_seq=per_seq, cast_ffn=cast_ffn),
        grid=(steps,),
        in_specs=in_specs,
        out_specs=out_specs,
        out_shape=out_shape,
        compiler_params=_params(("arbitrary",)),
        name="in_proj",
    )(*operands)


def _ssd_gmlp_block(*, xc, acum, tot, dt, d_row, e3, mask, state_t, w_s_ref, bsb, u, v,
                    gm_g, gm_b):
    rows = xc.shape[0]
    xs = xc[:, :SSD_WIDTH]
    bm = xc[:, SSD_WIDTH:SSD_WIDTH + SSD_GROUPS * SSD_STATE]
    cm = xc[:, SSD_WIDTH + SSD_GROUPS * SSD_STATE:]

    acum_t = acum.T
    dt_t = dt.T
    w_end = jnp.exp(tot - acum) * dt
    xw = xs * _expand(w_end, e3)
    eacum = jnp.exp(acum)

    xs_b = xs.astype(BF16)
    lane = lax.broadcasted_iota(jnp.int32, (rows, 2 * SSD_HEADDIM), 1)
    low_half = lane < SSD_HEADDIM

    scores = []
    for g in range(SSD_GROUPS):
        cg = cm[:, g * SSD_STATE:(g + 1) * SSD_STATE].astype(BF16)
        bg = bm[:, g * SSD_STATE:(g + 1) * SSD_STATE].astype(BF16)
        scores.append(_dot_nt(cg, bg))

    heads_per_group = SSD_HEADS // SSD_GROUPS
    y_pairs = []
    for pair in range(SSD_HEADS // 2):
        sl = slice(pair * 2 * SSD_HEADDIM, (pair + 1) * 2 * SSD_HEADDIM)
        x_pair = xs_b[:, sl]
        zero = jnp.zeros_like(x_pair)
        if state_t is not None:
            s_pair = state_t[:, sl].astype(BF16)
        acc = None
        for k in range(2):
            h = 2 * pair + k
            g = h // heads_per_group
            seg = acum[:, h:h + 1] - acum_t[h:h + 1, :]
            m = jnp.where(mask, scores[g] * jnp.exp(seg) * dt_t[h:h + 1, :], 0.0)
            keep = low_half if k == 0 else jnp.logical_not(low_half)
            rhs = jnp.where(keep, x_pair, zero)
            lhs = m.astype(BF16)
            if state_t is not None:
                c_sc = cm[:, g * SSD_STATE:(g + 1) * SSD_STATE] * eacum[:, h:h + 1]
                lhs = jnp.concatenate([lhs, c_sc.astype(BF16)], axis=1)
                rhs = jnp.concatenate([rhs, jnp.where(keep, s_pair, zero)], axis=0)
            part = _dot(lhs, rhs)
            acc = part if acc is None else acc + part
        y_pairs.append(acc)
    y = jnp.concatenate(y_pairs, axis=1) + d_row * xs

    ug = _gelu_tanh(u)
    vn = _layer_norm(_gelu_tanh(v), gm_g, gm_b)
    vn_b = vn.astype(BF16)
    mixed = []
    for h in range(GM_HEADS):
        w = jnp.where(mask, w_s_ref[h], 0.0).astype(BF16)
        mixed.append(_dot(w, vn_b[:, h * GM_HEAD:(h + 1) * GM_HEAD]))
    y_gm = ug * (jnp.concatenate(mixed, axis=1) + bsb)
    return y, xw, y_gm, vn


def _gated_rmsnorm(y, z, norm_g):
    hg = y * _silu(z)
    parts = []
    for g in range(SSD_GROUPS):
        hh = hg[:, g * GROUP_WIDTH:(g + 1) * GROUP_WIDTH]
        ms = jnp.mean(hh * hh, axis=-1, keepdims=True)
        parts.append(hh * lax.rsqrt(ms + LN_EPS))
    return jnp.concatenate(parts, axis=1) * norm_g


def _conv_silu(taps, conv_w_ref, conv_b):
    acc = conv_b + conv_w_ref[0:1, :] * taps[0]
    for k in range(1, CONV_K):
        acc = acc + conv_w_ref[k:k + 1, :] * taps[k]
    return _silu(acc)


def _prompt_mixer_kernel(z_ref, xbc_ref, dt_ref, u_ref, v_ref, conv_w_ref, conv_b_ref, dtb_ref,
                         alog_ref, d_ref, e_ref, tril3_ref, ng_ref, gg_ref, gb_ref, ws_ref, bsb_ref,
                         y_ref, ssm_ref, halo_ref, st_ref):
    c = pl.program_id(1)
    nc = pl.num_programs(1)

    @pl.when(c == 0)
    def _():
        halo_ref[...] = jnp.zeros_like(halo_ref)
        st_ref[...] = jnp.zeros_like(st_ref)

    xbc = xbc_ref[...]
    xp = jnp.concatenate([halo_ref[...], xbc], axis=0)
    taps = [pltpu.roll(xp, CONV_K - 1 - k, 0)[SUBLANES:, :] for k in range(CONV_K - 1)]
    taps.append(xbc)
    xc = _conv_silu(taps, conv_w_ref, conv_b_ref[...])
    halo_ref[...] = xbc[CHUNK - SUBLANES:, :]

    dt = _softplus(dt_ref[...] + dtb_ref[...])
    a = dt * (-jnp.exp(alog_ref[...]))
    ri = lax.broadcasted_iota(jnp.int32, (CHUNK, CHUNK), 0)
    ci = lax.broadcasted_iota(jnp.int32, (CHUNK, CHUNK), 1)
    causal = ci <= ri
    acum = _dot(tril3_ref[...], jnp.concatenate(_split3(a), axis=0))
    tot = jnp.broadcast_to(acum[CHUNK - 1:CHUNK, :], (CHUNK, DT_PAD))

    state_t = st_ref[...]
    y, xw, y_gm, _ = _ssd_gmlp_block(
        xc=xc, acum=acum, tot=tot, dt=dt, d_row=d_ref[...], e3=e_ref[...], mask=causal,
        state_t=state_t, w_s_ref=ws_ref, bsb=bsb_ref[...], u=u_ref[...], v=v_ref[...],
        gm_g=gg_ref[...], gm_b=gb_ref[...])

    cd = jnp.exp(_expand(tot[0:SUBLANES, :], e_ref[...]))[0:1, :]
    bm = xc[:, SSD_WIDTH:SSD_WIDTH + SSD_GROUPS * SSD_STATE].astype(BF16)
    xw_b = xw.astype(BF16)
    upd = [_dot_tn(bm[:, g * SSD_STATE:(g + 1) * SSD_STATE],
                   xw_b[:, g * GROUP_WIDTH:(g + 1) * GROUP_WIDTH]) for g in range(SSD_GROUPS)]
    new_state = state_t * cd + jnp.concatenate(upd, axis=1)
    st_ref[...] = new_state

    y_ssd = _gated_rmsnorm(y, z_ref[...], ng_ref[...])
    y_ref[...] = jnp.concatenate([y_ssd, y_gm], axis=1).astype(BF16)

    @pl.when(c == nc - 1)
    def _():
        ssm_ref[...] = new_state.T


def _prompt_mixer_call(z, xbc, dt, u, v, consts, batch, seq):
    nc = seq // CHUNK
    row = lambda b, c: (b * nc + c, 0)
    t = z.shape[0]
    const_specs = [_resident(a.shape) for a in consts]
    return pl.pallas_call(
        _prompt_mixer_kernel,
        grid=(batch, nc),
        in_specs=[pl.BlockSpec((CHUNK, SSD_WIDTH), row),
                  pl.BlockSpec((CHUNK, CONV_DIM), row),
                  pl.BlockSpec((CHUNK, DT_PAD), row),
                  pl.BlockSpec((CHUNK, GM_WIDTH), row),
                  pl.BlockSpec((CHUNK, GM_WIDTH), row)] + const_specs,
        out_specs=[pl.BlockSpec((CHUNK, 2 * SSD_WIDTH), row),
                   pl.BlockSpec((None, SSD_WIDTH, SSD_STATE), lambda b, c: (b, 0, 0))],
        out_shape=[jax.ShapeDtypeStruct((t, 2 * SSD_WIDTH), BF16),
                   jax.ShapeDtypeStruct((batch, SSD_WIDTH, SSD_STATE), F32)],
        scratch_shapes=[pltpu.VMEM((SUBLANES, CONV_DIM), F32),
                        pltpu.VMEM((SSD_STATE, SSD_WIDTH), F32)],
        compiler_params=_params(("arbitrary", "arbitrary")),
        name="mixer_prompt",
    )(z, xbc, dt, u, v, *consts)


SAMPLE_BB = CHUNK // DEC_SEQ


def _seg_cumsum(a, t):
    k = 1
    while k < DEC_SEQ:
        a = a + jnp.where(t >= k, pltpu.roll(a, k, 0), 0.0)
        k *= 2
    return a


def _seg_last(a, t):
    rows = a.shape[0]
    x = jnp.where(t == DEC_SEQ - 1, a, 0.0)
    k = 1
    while k < DEC_SEQ:
        x = x + pltpu.roll(x, rows - k, 0)
        k *= 2
    return x


def _sample_mixer_kernel(z_ref, xbc_ref, buf_ref, dt_ref, u_ref, v_ref, s_ref, conv_w_ref,
                         conv_b_ref, dtb_ref, alog_ref, d_ref, e_ref, tril3_ref, ng_ref, gg_ref,
                         gb_ref, ws_ref, bsb_ref,
                         y_ref, snew_ref, vn_ref,
                         c_scr, b_scr, xw_scr, aux_scr, yoff_scr):
    del tril3_ref
    rows = CHUNK
    tcol = lax.broadcasted_iota(jnp.int32, (rows, 1), 0) % DEC_SEQ

    xbc = xbc_ref[...]
    buf = buf_ref[...]
    taps = []
    for k in range(CONV_K - 1):
        back = CONV_K - 1 - k
        taps.append(jnp.where(tcol >= back, pltpu.roll(xbc, back, 0),
                              pltpu.roll(buf, rows - DEC_SEQ + back, 0)))
    taps.append(xbc)
    xc = _conv_silu(taps, conv_w_ref, conv_b_ref[...])

    dt = _softplus(dt_ref[...] + dtb_ref[...])
    a = dt * (-jnp.exp(alog_ref[...]))
    acum = _seg_cumsum(a, tcol)
    tot = _seg_last(acum, tcol)

    ri = lax.broadcasted_iota(jnp.int32, (rows, rows), 0)
    ci = lax.broadcasted_iota(jnp.int32, (rows, rows), 1)
    mask = jnp.logical_and(ci <= ri, (ci // DEC_SEQ) == (ri // DEC_SEQ))

    y, xw, y_gm, vn = _ssd_gmlp_block(
        xc=xc, acum=acum, tot=tot, dt=dt, d_row=d_ref[...], e3=e_ref[...], mask=mask,
        state_t=None, w_s_ref=ws_ref, bsb=bsb_ref[...], u=u_ref[...], v=v_ref[...],
        gm_g=gg_ref[...], gm_b=gb_ref[...])
    vn_ref[...] = vn

    tfull = lax.broadcasted_iota(jnp.int32, (rows, SSD_WIDTH), 0) % DEC_SEQ
    dcx = jnp.exp(_expand(tot, e_ref[...]))
    hi, mid, lo = _split3(dcx)
    aux_scr[...] = jnp.where(tfull == 0, hi.astype(F32),
                             jnp.where(tfull == 1, mid.astype(F32),
                                       jnp.where(tfull == 2, lo.astype(F32), 0.0)))
    xw_scr[...] = xw
    b_scr[...] = xc[:, SSD_WIDTH:SSD_WIDTH + SSD_GROUPS * SSD_STATE]
    c_scr[...] = xc[:, SSD_WIDTH + SSD_GROUPS * SSD_STATE:]

    r8 = lax.broadcasted_iota(jnp.int32, (DEC_SEQ, 2 * SSD_STATE), 0)
    l8 = lax.broadcasted_iota(jnp.int32, (DEC_SEQ, 2 * SSD_STATE), 1)
    ones_part = jnp.where(jnp.logical_and(r8 < 3, l8 >= SSD_STATE), 1.0, 0.0)
    zeros_b = jnp.zeros((DEC_SEQ, SSD_STATE), F32)

    def per_seq(b, carry):
        r0 = pl.multiple_of(b * DEC_SEQ, DEC_SEQ)
        rsl = pl.ds(r0, DEC_SEQ)
        for g in range(SSD_GROUPS):
            gs = slice(g * GROUP_WIDTH, (g + 1) * GROUP_WIDTH)
            ns = slice(g * SSD_STATE, (g + 1) * SSD_STATE)
            s_bg = s_ref[b, gs, :]
            yoff_scr[rsl, gs] = _dot_nt(c_scr[rsl, ns].astype(BF16), s_bg.astype(BF16))
            lhs = jnp.concatenate([xw_scr[rsl, gs], aux_scr[rsl, gs]], axis=0)
            rhs = jnp.concatenate(
                [jnp.concatenate([b_scr[rsl, ns], zeros_b], axis=1), ones_part], axis=0)
            res = _dot_tn(lhs.astype(BF16), rhs.astype(BF16))
            snew_ref[b, gs, :] = res[:, SSD_STATE:] * s_bg + res[:, :SSD_STATE]
        return carry

    lax.fori_loop(0, SAMPLE_BB, per_seq, 0)

    y = y + yoff_scr[...] * jnp.exp(_expand(acum, e_ref[...]))
    y_ssd = _gated_rmsnorm(y, z_ref[...], ng_ref[...])
    y_ref[...] = jnp.concatenate([y_ssd, y_gm], axis=1).astype(BF16)


def _sample_mixer_call(z, xbc, buf8, dt, u, v, state, consts):
    t = z.shape[0]
    nb = t // CHUNK
    row = lambda i: (i, 0)
    st_spec = pl.BlockSpec((SAMPLE_BB, SSD_WIDTH, SSD_STATE), lambda i: (i, 0, 0))
    const_specs = [_resident(a.shape) for a in consts]
    return pl.pallas_call(
        _sample_mixer_kernel,
        grid=(nb,),
        in_specs=[pl.BlockSpec((CHUNK, SSD_WIDTH), row),
                  pl.BlockSpec((CHUNK, CONV_DIM), row),
                  pl.BlockSpec((CHUNK, CONV_DIM), row),
                  pl.BlockSpec((CHUNK, DT_PAD), row),
                  pl.BlockSpec((CHUNK, GM_WIDTH), row),
                  pl.BlockSpec((CHUNK, GM_WIDTH), row),
                  st_spec] + const_specs,
        out_specs=[pl.BlockSpec((CHUNK, 2 * SSD_WIDTH), row), st_spec,
                   pl.BlockSpec((CHUNK, GM_WIDTH), row)],
        out_shape=[jax.ShapeDtypeStruct((t, 2 * SSD_WIDTH), BF16),
                   jax.ShapeDtypeStruct(state.shape, F32),
                   jax.ShapeDtypeStruct((t, GM_WIDTH), F32)],
        scratch_shapes=[pltpu.VMEM((CHUNK, SSD_GROUPS * SSD_STATE), F32),
                        pltpu.VMEM((CHUNK, SSD_GROUPS * SSD_STATE), F32),
                        pltpu.VMEM((CHUNK, SSD_WIDTH), F32),
                        pltpu.VMEM((CHUNK, SSD_WIDTH), F32),
                        pltpu.VMEM((CHUNK, SSD_WIDTH), F32)],
        compiler_params=_params(("arbitrary",)),
        name="mixer_sample",
    )(z, xbc, buf8, dt, u, v, state, *consts)


def _outln_kernel(*refs, per_seq):
    it = iter(refs)
    x_ref, y_ref, g_ref = next(it), next(it), next(it)
    p3_ref = next(it) if per_seq else None
    ing_ref, inb_ref, w_ref, lg_ref, lb_ref, o_ref = (next(it) for _ in range(6))
    xn = _layer_norm(x_ref[...], ing_ref[...], inb_ref[...])
    mix = _dot(y_ref[...], w_ref[...])
    o_ref[...] = _layer_norm(ALPHA * xn + (1.0 + _mod_rows(g_ref, p3_ref)) * mix,
                             lg_ref[...], lb_ref[...])


def _outln_call(x2d, ymix, mod, rows_per_mod, per_seq, ln_in_g, ln_in_b, w_out, ln_g, ln_b, tm):
    t = x2d.shape[0]
    row = lambda i: (i, 0)
    operands = [x2d, ymix, mod]
    in_specs = ([pl.BlockSpec((tm, D_MODEL), row), pl.BlockSpec((tm, D_MODEL), row)]
                + _mod_specs(tm, rows_per_mod, per_seq, (2,)))
    if per_seq:
        operands.append(_repeat_matrix3(tm))
        in_specs.append(_resident((tm, _repeat_k(tm))))
    operands += [ln_in_g, ln_in_b, w_out, ln_g, ln_b]
    in_specs += [_resident((1, D_MODEL)), _resident((1, D_MODEL)), _resident(w_out.shape),
                 _resident((1, D_MODEL)), _resident((1, D_MODEL))]
    return pl.pallas_call(
        functools.partial(_outln_kernel, per_seq=per_seq),
        grid=(t // tm,),
        in_specs=in_specs,
        out_specs=pl.BlockSpec((tm, D_MODEL), row),
        out_shape=jax.ShapeDtypeStruct((t, D_MODEL), F32),
        compiler_params=_params(("arbitrary",)),
        name="out_ln",
    )(*operands)


def _ffn_kernel(*refs, per_seq):
    it = iter(refs)
    x_ref, sh_ref, sc_ref, g_ref = (next(it) for _ in range(4))
    p3_ref = next(it) if per_seq else None
    w1_ref, w2_ref, lg_ref, lb_ref, o_ref, h_scr = (next(it) for _ in range(6))
    j = pl.program_id(1)
    nj = pl.num_programs(1)

    @pl.when(j == 0)
    def _():
        h_scr[...] = (x_ref[...] * (1.0 + _mod_rows(sc_ref, p3_ref))
                      + _mod_rows(sh_ref, p3_ref)).astype(BF16)
        o_ref[...] = jnp.zeros_like(o_ref)

    a = jnp.maximum(_dot(h_scr[...], w1_ref[...]), 0.0)
    o_ref[...] += _dot((a * a).astype(BF16), w2_ref[...])

    @pl.when(j == nj - 1)
    def _():
        o_ref[...] = _layer_norm(ALPHA * x_ref[...] + (1.0 + _mod_rows(g_ref, p3_ref)) * o_ref[...],
                                 lg_ref[...], lb_ref[...])


def _ffn_call(x1, mod, rows_per_mod, per_seq, w1_tiles, w2, ln_g, ln_b, tm):
    t = x1.shape[0]
    row = lambda i, j: (i, 0)
    operands = [x1, mod, mod, mod]
    in_specs = [pl.BlockSpec((tm, D_MODEL), row)] + _mod_specs(tm, rows_per_mod, per_seq, (3, 4, 5))
    if per_seq:
        operands.append(_repeat_matrix3(tm))
        in_specs.append(_resident((tm, _repeat_k(tm))))
    operands += [w1_tiles, w2, ln_g, ln_b]
    in_specs += [pl.BlockSpec((None, D_MODEL, FF_TILE), lambda i, j: (j, 0, 0)),
                 pl.BlockSpec((FF_TILE, D_MODEL), lambda i, j: (j, 0)),
                 _resident((1, D_MODEL)), _resident((1, D_MODEL))]
    return pl.pallas_call(
        functools.partial(_ffn_kernel, per_seq=per_seq),
        grid=(t // tm, N_FF_TILES),
        in_specs=in_specs,
        out_specs=pl.BlockSpec((tm, D_MODEL), row),
        out_shape=jax.ShapeDtypeStruct((t, D_MODEL), F32),
        scratch_shapes=[pltpu.VMEM((tm, D_MODEL), BF16)],
        compiler_params=_params(("arbitrary", "arbitrary")),
        name="ffn",
    )(*operands)


def kernel(x_prompt, x_sample, state_ssm, state_conv, c_prompt, c_sample, ln_in_g, ln_in_b, w_mod, b_mod, w_in, conv_w, conv_b, dt_bias, a_log, d_skip, ssd_norm_g, gm_ln_g, gm_ln_b, gm_w_s, gm_b_s, w_out, ln_mix_g, ln_mix_b, w_ff1, w_ff2, ln_ffn_g, ln_ffn_b):
    depth = w_mod.shape[0]
    assert depth == 1
    bp, seq, _ = x_prompt.shape
    bs, dec, _ = x_sample.shape
    assert dec == DEC_SEQ and seq % CHUNK == 0 and (bs * dec) % CHUNK == 0

    r1 = lambda a: a.reshape(1, -1)
    ln_in_g2, ln_in_b2 = r1(ln_in_g), r1(ln_in_b)
    l = 0

    i1 = SSD_WIDTH
    i2 = i1 + CONV_DIM
    i3 = i2 + SSD_HEADS
    i4 = i3 + GM_WIDTH
    w_in_l = w_in[l]
    w_pieces = [w_in_l[:, :i1].astype(BF16), w_in_l[:, i1:i2].astype(BF16),
                _pad_cols(w_in_l[:, i2:i3], DT_PAD).astype(BF16),
                w_in_l[:, i3:i4].astype(BF16), w_in_l[:, i4:].astype(BF16)]
    w_out_b = w_out[l].astype(BF16)

    head_of_chan = jnp.arange(SSD_WIDTH, dtype=jnp.int32) // SSD_HEADDIM
    e_sel = (jnp.arange(DT_PAD, dtype=jnp.int32)[:, None] == head_of_chan[None, :]).astype(BF16)
    e3 = jnp.concatenate([e_sel, e_sel, e_sel], axis=0)
    tril = jnp.tril(jnp.ones((CHUNK, CHUNK), BF16))
    tril3 = jnp.concatenate([tril, tril, tril], axis=1)
    d_row = r1(jnp.repeat(d_skip[l], SSD_HEADDIM))
    dtb = _pad_cols(r1(dt_bias[l]), DT_PAD)
    alog = _pad_cols(r1(a_log[l]), DT_PAD)
    mixer_consts = [conv_w[l], r1(conv_b[l]), dtb, alog, d_row, e3, tril3, r1(ssd_norm_g[l]),
                    r1(gm_ln_g[l]), r1(gm_ln_b[l])]
    bsb_p = jnp.repeat(gm_b_s[l].T, GM_HEAD, axis=1)
    reps = CHUNK // DEC_SEQ
    ws_s = jnp.tile(gm_w_s[l][:, :DEC_SEQ, :DEC_SEQ], (1, reps, reps))
    bsb_s = jnp.tile(jnp.repeat(gm_b_s[l][:, :DEC_SEQ].T, GM_HEAD, axis=1), (reps, 1))

    n_c = bp + bs
    c_all = jnp.concatenate([c_prompt, c_sample], axis=0)
    c_all = jnp.pad(c_all, ((0, (-n_c) % SUBLANES), (0, 0)))
    mod = _mod_call(c_all, w_mod[l], r1(b_mod[l]))
    mod_p = mod[:bp].reshape(bp, 1, 6 * D_MODEL)
    mod_s = mod[bp:n_c]

    xp2 = x_prompt.reshape(bp * seq, D_MODEL)
    z, xbc, dtr, u, v, w1_t, w2_b = _inproj_call(
        xp2, mod_p, seq, False, ln_in_g2, ln_in_b2, w_pieces, tm=256, ffn_w=(w_ff1[l], w_ff2[l]))
    ymix, ssm_p = _prompt_mixer_call(z, xbc, dtr, u, v, mixer_consts + [gm_w_s[l], bsb_p], bp, seq)
    x1 = _outln_call(xp2, ymix, mod_p, seq, False, ln_in_g2, ln_in_b2, w_out_b,
                     r1(ln_mix_g[l]), r1(ln_mix_b[l]), tm=512)
    yp = _ffn_call(x1, mod_p, seq, False, w1_t, w2_b, r1(ln_ffn_g[l]), r1(ln_ffn_b[l]), tm=512)
    conv_p = xbc.reshape(bp, seq, CONV_DIM)[:, seq - (CONV_K - 1):, :]

    xs2 = x_sample.reshape(bs * dec, D_MODEL)
    zs, xbcs, dtrs, us, vs = _inproj_call(xs2, mod_s, None, True, ln_in_g2, ln_in_b2, w_pieces, tm=256)
    buf8 = jnp.pad(state_conv[l], ((0, 0), (DEC_SEQ - (CONV_K - 1), 0), (0, 0)))
    buf8 = buf8.reshape(bs * dec, CONV_DIM)
    st_in = state_ssm[l].reshape(bs, SSD_WIDTH, SSD_STATE)
    ymix_s, ssm_s, vn_s = _sample_mixer_call(zs, xbcs, buf8, dtrs, us, vs, st_in,
                                             mixer_consts + [ws_s, bsb_s])
    x1s = _outln_call(xs2, ymix_s, mod_s, None, True, ln_in_g2, ln_in_b2, w_out_b,
                      r1(ln_mix_g[l]), r1(ln_mix_b[l]), tm=512)
    ys = _ffn_call(x1s, mod_s, None, True, w1_t, w2_b, r1(ln_ffn_g[l]), r1(ln_ffn_b[l]), tm=512)
    conv_s = xbcs.reshape(bs, dec, CONV_DIM)[:, dec - (CONV_K - 1):, :]

    return (yp.reshape(bp, seq, D_MODEL),
            ys.reshape(bs, dec, D_MODEL),
            ssm_p.reshape(1, bp, SSD_HEADS, SSD_HEADDIM, SSD_STATE),
            conv_p[None],
            ssm_s.reshape(1, bs, SSD_HEADS, SSD_HEADDIM, SSD_STATE),
            conv_s[None],
            vn_s.reshape(1, bs, dec, GM_WIDTH))
```

```python
import functools
import math

import jax
import jax.numpy as jnp
from jax import lax
from jax.experimental import pallas as pl
from jax.experimental.pallas import tpu as pltpu

D_MODEL = 2048
SSD_WIDTH = 1024
SSD_HEADDIM = 64
SSD_HEADS = 16
SSD_GROUPS = 2
SSD_STATE = 128
GROUP_WIDTH = SSD_WIDTH // SSD_GROUPS
CONV_K = 4
CONV_DIM = SSD_WIDTH + 2 * SSD_GROUPS * SSD_STATE
GM_WIDTH = 1024
GM_HEAD = 128
GM_HEADS = 8
D_FF = 4 * D_MODEL
FF_TILE = 1024
N_FF_TILES = D_FF // FF_TILE
CHUNK = 128
DEC_SEQ = 8
DT_PAD = 128
SUBLANES = 8
LANES = 128
ALPHA = 2.0 ** 0.25
LN_EPS = 1e-5

V7X_VMEM_BYTES = 64 * 1024 * 1024
VMEM_LIMIT = V7X_VMEM_BYTES - 8 * 1024 * 1024

F32 = jnp.float32
BF16 = jnp.bfloat16


def _layer_norm(x, g, b):
    mu = jnp.mean(x, axis=-1, keepdims=True)
    xc = x - mu
    var = jnp.mean(xc * xc, axis=-1, keepdims=True)
    return xc * lax.rsqrt(var + LN_EPS) * g + b


def _silu(x):
    return x / (1.0 + jnp.exp(-x))


def _gelu_tanh(x):
    c = math.sqrt(2.0 / math.pi)
    return 0.5 * x * (1.0 + jnp.tanh(c * (x + 0.044715 * (x * x * x))))


def _softplus(x):
    return jnp.maximum(x, 0.0) + jnp.log1p(jnp.exp(-jnp.abs(x)))


def _dot(a, b):
    return jnp.dot(a, b, preferred_element_type=F32)


def _dot_nt(a, b):
    return lax.dot_general(a, b, (((1,), (1,)), ((), ())), preferred_element_type=F32)


def _dot_tn(a, b):
    return lax.dot_general(a, b, (((0,), (0,)), ((), ())), preferred_element_type=F32)


def _split3(x):
    hi = x.astype(BF16)
    r1 = x - hi.astype(F32)
    mid = r1.astype(BF16)
    lo = (r1 - mid.astype(F32)).astype(BF16)
    return hi, mid, lo


def _expand(x, sel3):
    return _dot(jnp.concatenate(_split3(x), axis=1), sel3)


def _mod_rows(ref, p3_ref):
    if p3_ref is None:
        return ref[...]
    parts = list(_split3(ref[...]))
    pad = p3_ref.shape[1] - 3 * ref.shape[0]
    if pad:
        parts.append(jnp.zeros((pad, ref.shape[1]), BF16))
    return _dot(p3_ref[...], jnp.concatenate(parts, axis=0))


def _resident(shape):
    nd = len(shape)
    return pl.BlockSpec(shape, lambda *_: (0,) * nd, pipeline_mode=pl.Buffered(1))


def _mod_specs(tm, rows_per_mod, per_seq, pieces):
    if per_seq:
        return [pl.BlockSpec((tm // DEC_SEQ, D_MODEL), lambda i, *_, p=p: (i, p)) for p in pieces]
    tiles_per_mod = rows_per_mod // tm
    return [pl.BlockSpec((None, 1, D_MODEL), lambda i, *_, p=p: (i // tiles_per_mod, 0, p))
            for p in pieces]


def _repeat_matrix3(tm):
    nb = tm // DEC_SEQ
    sel = (jnp.arange(tm, dtype=jnp.int32)[:, None] // DEC_SEQ
           == jnp.arange(nb, dtype=jnp.int32)[None, :]).astype(BF16)
    return _pad_cols(jnp.concatenate([sel, sel, sel], axis=1), _repeat_k(tm))


def _repeat_k(tm):
    return -(-(3 * tm // DEC_SEQ) // LANES) * LANES


def _pad_cols(a, n):
    return jnp.pad(a, ((0, 0), (0, n - a.shape[1])))


def _params(sem):
    return pltpu.CompilerParams(dimension_semantics=sem, vmem_limit_bytes=VMEM_LIMIT)


IN_SPLITS = (0, SSD_WIDTH, SSD_WIDTH + CONV_DIM, SSD_WIDTH + CONV_DIM + SSD_HEADS,
             SSD_WIDTH + CONV_DIM + SSD_HEADS + GM_WIDTH)
IN_WIDTHS = (SSD_WIDTH, CONV_DIM, DT_PAD, GM_WIDTH, GM_WIDTH)
MOD_STEPS = 16
N_PROJ = len(IN_WIDTHS)


def _mod_kernel(c_ref, w_ref, b_ref, win_ref, o_ref, wz_ref, wx_ref, wd_ref, wu_ref, wv_ref):
    a = _silu(c_ref[...]).astype(BF16)
    o_ref[...] = _dot(a, w_ref[...].astype(BF16)) + b_ref[...]
    blk = win_ref[...]
    z0, x0, d0, u0, v0 = IN_SPLITS
    wz_ref[...] = blk[:, z0:x0].astype(BF16)
    wx_ref[...] = blk[:, x0:d0].astype(BF16)
    lane = lax.broadcasted_iota(jnp.int32, (blk.shape[0], DT_PAD), 1)
    wd_ref[...] = jnp.where(lane < SSD_HEADS, blk[:, d0:d0 + DT_PAD], 0.0).astype(BF16)
    wu_ref[...] = blk[:, u0:v0].astype(BF16)
    wv_ref[...] = blk[:, v0:v0 + GM_WIDTH].astype(BF16)


def _mod_call(c_all, w_mod, b_mod, w_in):
    m = c_all.shape[0]
    n = w_mod.shape[1]
    tn = n // MOD_STEPS
    band = D_MODEL // MOD_STEPS
    row = lambda j: (j, 0)
    return pl.pallas_call(
        _mod_kernel,
        grid=(MOD_STEPS,),
        in_specs=[
            pl.BlockSpec((m, D_MODEL), lambda j: (0, 0)),
            pl.BlockSpec((D_MODEL, tn), lambda j: (0, j)),
            pl.BlockSpec((1, tn), lambda j: (0, j)),
            pl.BlockSpec((band, w_in.shape[1]), row),
        ],
        out_specs=[pl.BlockSpec((m, tn), lambda j: (0, j))]
        + [pl.BlockSpec((band, w), row) for w in IN_WIDTHS],
        out_shape=[jax.ShapeDtypeStruct((m, n), F32)]
        + [jax.ShapeDtypeStruct((D_MODEL, w), BF16) for w in IN_WIDTHS],
        compiler_params=_params(("arbitrary",)),
        name="mod",
    )(c_all, w_mod, b_mod, w_in)


def _inproj_kernel(*refs, per_seq):
    it = iter(refs)
    x_ref, sh_ref, sc_ref = next(it), next(it), next(it)
    p3_ref = next(it) if per_seq else None
    g_ref, b_ref = next(it), next(it)
    w_refs = [next(it) for _ in range(N_PROJ)]
    o_refs = [next(it) for _ in range(N_PROJ)]
    xn = _layer_norm(x_ref[...], g_ref[...], b_ref[...])
    h = (xn * (1.0 + _mod_rows(sc_ref, p3_ref)) + _mod_rows(sh_ref, p3_ref)).astype(BF16)
    for w_ref, o_ref in zip(w_refs, o_refs):
        o_ref[...] = _dot(h, w_ref[...])


def _inproj_call(x2d, mod, rows_per_mod, per_seq, ln_g, ln_b, w_pieces, tm):
    t = x2d.shape[0]
    row = lambda i: (i, 0)
    operands = [x2d, mod, mod]
    in_specs = [pl.BlockSpec((tm, D_MODEL), row)] + _mod_specs(tm, rows_per_mod, per_seq, (0, 1))
    if per_seq:
        operands.append(_repeat_matrix3(tm))
        in_specs.append(_resident((tm, _repeat_k(tm))))
    operands += [ln_g, ln_b, *w_pieces]
    in_specs += [_resident((1, D_MODEL)), _resident((1, D_MODEL))] + [_resident(w.shape) for w in w_pieces]
    return pl.pallas_call(
        functools.partial(_inproj_kernel, per_seq=per_seq),
        grid=(t // tm,),
        in_specs=in_specs,
        out_specs=[pl.BlockSpec((tm, n), row) for n in IN_WIDTHS],
        out_shape=[jax.ShapeDtypeStruct((t, n), F32) for n in IN_WIDTHS],
        compiler_params=_params(("arbitrary",)),
        name="in_proj",
    )(*operands)


def _ssd_gmlp_block(*, xc, acum, tot, dt, d_row, e3, mask, state_t, w_s_ref, bsb, u, v,
                    gm_g, gm_b):
    rows = xc.shape[0]
    xs = xc[:, :SSD_WIDTH]
    bm = xc[:, SSD_WIDTH:SSD_WIDTH + SSD_GROUPS * SSD_STATE]
    cm = xc[:, SSD_WIDTH + SSD_GROUPS * SSD_STATE:]

    acum_t = acum.T
    dt_t = dt.T
    w_end = jnp.exp(tot - acum) * dt
    xw = xs * _expand(w_end, e3)
    eacum = jnp.exp(acum)

    xs_b = xs.astype(BF16)
    lane = lax.broadcasted_iota(jnp.int32, (rows, 2 * SSD_HEADDIM), 1)
    low_half = lane < SSD_HEADDIM

    scores = []
    for g in range(SSD_GROUPS):
        cg = cm[:, g * SSD_STATE:(g + 1) * SSD_STATE].astype(BF16)
        bg = bm[:, g * SSD_STATE:(g + 1) * SSD_STATE].astype(BF16)
        scores.append(_dot_nt(cg, bg))

    heads_per_group = SSD_HEADS // SSD_GROUPS
    y_pairs = []
    for pair in range(SSD_HEADS // 2):
        sl = slice(pair * 2 * SSD_HEADDIM, (pair + 1) * 2 * SSD_HEADDIM)
        x_pair = xs_b[:, sl]
        zero = jnp.zeros_like(x_pair)
        if state_t is not None:
            s_pair = state_t[:, sl].astype(BF16)
        acc = None
        for k in range(2):
            h = 2 * pair + k
            g = h // heads_per_group
            seg = acum[:, h:h + 1] - acum_t[h:h + 1, :]
            m = jnp.where(mask, scores[g] * jnp.exp(seg) * dt_t[h:h + 1, :], 0.0)
            keep = low_half if k == 0 else jnp.logical_not(low_half)
            rhs = jnp.where(keep, x_pair, zero)
            lhs = m.astype(BF16)
            if state_t is not None:
                c_sc = cm[:, g * SSD_STATE:(g + 1) * SSD_STATE] * eacum[:, h:h + 1]
                lhs = jnp.concatenate([lhs, c_sc.astype(BF16)], axis=1)
                rhs = jnp.concatenate([rhs, jnp.where(keep, s_pair, zero)], axis=0)
            part = _dot(lhs, rhs)
            acc = part if acc is None else acc + part
        y_pairs.append(acc)
    y = jnp.concatenate(y_pairs, axis=1) + d_row * xs

    ug = _gelu_tanh(u)
    vn = _layer_norm(_gelu_tanh(v), gm_g, gm_b)
    vn_b = vn.astype(BF16)
    mixed = []
    for h in range(GM_HEADS):
        w = jnp.where(mask, w_s_ref[h], 0.0).astype(BF16)
        mixed.append(_dot(w, vn_b[:, h * GM_HEAD:(h + 1) * GM_HEAD]))
    y_gm = ug * (jnp.concatenate(mixed, axis=1) + bsb)
    return y, xw, y_gm, vn


def _gated_rmsnorm(y, z, norm_g):
    hg = y * _silu(z)
    parts = []
    for g in range(SSD_GROUPS):
        hh = hg[:, g * GROUP_WIDTH:(g + 1) * GROUP_WIDTH]
        ms = jnp.mean(hh * hh, axis=-1, keepdims=True)
        parts.append(hh * lax.rsqrt(ms + LN_EPS))
    return jnp.concatenate(parts, axis=1) * norm_g


def _conv_silu(taps, conv_w_ref, conv_b):
    acc = conv_b + conv_w_ref[0:1, :] * taps[0]
    for k in range(1, CONV_K):
        acc = acc + conv_w_ref[k:k + 1, :] * taps[k]
    return _silu(acc)


def _prompt_chunk(proj, rows, reset, consts, halo_ref, st_ref, causal):
    z_s, xbc_s, dt_s, u_s, v_s = proj
    (conv_w_ref, conv_b_ref, dtb_ref, alog_ref, d_ref, e_ref, tril3_ref, ng_ref, gg_ref, gb_ref,
     ws_ref, bsb_ref) = consts
    xbc = xbc_s[rows, :]
    halo = halo_ref[...] if reset is None else jnp.where(reset, 0.0, halo_ref[...])
    xp = jnp.concatenate([halo, xbc], axis=0)
    taps = [pltpu.roll(xp, CONV_K - 1 - k, 0)[SUBLANES:, :] for k in range(CONV_K - 1)]
    taps.append(xbc)
    xc = _conv_silu(taps, conv_w_ref, conv_b_ref[...])
    halo_ref[...] = xbc[CHUNK - SUBLANES:, :]

    dt = _softplus(dt_s[rows, :] + dtb_ref[...])
    a = dt * (-jnp.exp(alog_ref[...]))
    acum = _dot(tril3_ref[...], jnp.concatenate(_split3(a), axis=0))
    tot = jnp.broadcast_to(acum[CHUNK - 1:CHUNK, :], (CHUNK, DT_PAD))

    state_t = st_ref[...] if reset is None else jnp.where(reset, 0.0, st_ref[...])
    y, xw, y_gm, _ = _ssd_gmlp_block(
        xc=xc, acum=acum, tot=tot, dt=dt, d_row=d_ref[...], e3=e_ref[...], mask=causal,
        state_t=state_t, w_s_ref=ws_ref, bsb=bsb_ref[...], u=u_s[rows, :], v=v_s[rows, :],
        gm_g=gg_ref[...], gm_b=gb_ref[...])

    cd = jnp.exp(_expand(tot[0:SUBLANES, :], e_ref[...]))[0:1, :]
    bm = xc[:, SSD_WIDTH:SSD_WIDTH + SSD_GROUPS * SSD_STATE].astype(BF16)
    xw_b = xw.astype(BF16)
    upd = [_dot_tn(bm[:, g * SSD_STATE:(g + 1) * SSD_STATE],
                   xw_b[:, g * GROUP_WIDTH:(g + 1) * GROUP_WIDTH]) for g in range(SSD_GROUPS)]
    st_ref[...] = state_t * cd + jnp.concatenate(upd, axis=1)

    y_ssd = _gated_rmsnorm(y, z_s[rows, :], ng_ref[...])
    return jnp.concatenate([y_ssd, y_gm], axis=1).astype(BF16)


FRONT_TM = 256


def _front_kernel(*refs, tiles_per_seq):
    it = iter(refs)
    x_ref, sh_ref, sc_ref, g_ref, b_ref = (next(it) for _ in range(5))
    w_refs = [next(it) for _ in range(N_PROJ)]
    consts = [next(it) for _ in range(12)]
    woutf_ref = next(it)
    y_even_ref, y_odd_ref, ssm_ref, conv_ref, woutb_ref = (next(it) for _ in range(5))
    proj_a = [next(it) for _ in range(N_PROJ)]
    proj_b = [next(it) for _ in range(N_PROJ)]
    halo_ref, st_ref = next(it), next(it)

    k = pl.program_id(0)
    n_pairs = pl.num_programs(0) - 1
    ri = lax.broadcasted_iota(jnp.int32, (CHUNK, CHUNK), 0)
    ci = lax.broadcasted_iota(jnp.int32, (CHUNK, CHUNK), 1)
    causal = ci <= ri

    def project(j, proj):
        xn = _layer_norm(x_ref[j * FRONT_TM:(j + 1) * FRONT_TM, :], g_ref[...], b_ref[...])
        h = (xn * (1.0 + sc_ref[...]) + sh_ref[...]).astype(BF16)
        for w_ref, p in zip(w_refs, proj):
            p[...] = _dot(h, w_ref[...])

    def mix(tile, proj, y_ref):
        opens_seq = None if tile is None else lax.rem(tile, tiles_per_seq) == 0
        for c in range(FRONT_TM // CHUNK):
            rows = slice(c * CHUNK, (c + 1) * CHUNK)
            reset = opens_seq if c == 0 else None
            y_ref[rows, :] = _prompt_chunk(proj, rows, reset, consts, halo_ref, st_ref, causal)

    def finish_odd():
        ssm_ref[...] = st_ref[...].T
        conv_ref[...] = halo_ref[...]

    @pl.when(k == 0)
    def _():
        for p in proj_b:
            p[...] = jnp.zeros_like(p)
        halo_ref[...] = jnp.zeros_like(halo_ref)
        st_ref[...] = jnp.zeros_like(st_ref)

    @pl.when(k < n_pairs)
    def _():
        project(0, proj_a)
        mix(None, proj_b, y_odd_ref)
        finish_odd()
        project(1, proj_b)
        mix(2 * k, proj_a, y_even_ref)
        woutb_ref[...] = woutf_ref[...].astype(BF16)

    @pl.when(k == n_pairs)
    def _():
        mix(None, proj_b, y_odd_ref)
        finish_odd()


def _front_call(x2d, mod, seq, ln_g, ln_b, w_pieces, consts, w_out):
    t = x2d.shape[0]
    n_pairs = t // (2 * FRONT_TM)
    tiles_per_seq = seq // FRONT_TM
    pairs_per_seq = tiles_per_seq // 2
    batch = t // seq
    pair = lambda k: jnp.minimum(k, n_pairs - 1)
    prev = lambda k: jnp.maximum(k - 1, 0)
    band = D_MODEL // n_pairs
    mod_spec = lambda p: pl.BlockSpec((None, 1, D_MODEL), lambda k: (pair(k) // pairs_per_seq, 0, p))
    return pl.pallas_call(
        functools.partial(_front_kernel, tiles_per_seq=tiles_per_seq),
        grid=(n_pairs + 1,),
        in_specs=[pl.BlockSpec((2 * FRONT_TM, D_MODEL), lambda k: (pair(k), 0)),
                  mod_spec(0), mod_spec(1), _resident((1, D_MODEL)), _resident((1, D_MODEL))]
        + [_resident(w.shape) for w in w_pieces]
        + [_resident(a.shape) for a in consts]
        + [pl.BlockSpec((band, D_MODEL), lambda k: (pair(k), 0))],
        out_specs=[pl.BlockSpec((FRONT_TM, D_MODEL), lambda k: (pair(k), 0)),
                   pl.BlockSpec((FRONT_TM, D_MODEL), lambda k: (prev(k), 0)),
                   pl.BlockSpec((None, SSD_WIDTH, SSD_STATE), lambda k: (prev(k) // pairs_per_seq, 0, 0)),
                   pl.BlockSpec((None, SUBLANES, CONV_DIM), lambda k: (prev(k) // pairs_per_seq, 0, 0)),
                   pl.BlockSpec((band, D_MODEL), lambda k: (pair(k), 0))],
        out_shape=[jax.ShapeDtypeStruct((t // 2, D_MODEL), BF16),
                   jax.ShapeDtypeStruct((t // 2, D_MODEL), BF16),
                   jax.ShapeDtypeStruct((batch, SSD_WIDTH, SSD_STATE), F32),
                   jax.ShapeDtypeStruct((batch, SUBLANES, CONV_DIM), F32),
                   jax.ShapeDtypeStruct((D_MODEL, D_MODEL), BF16)],
        scratch_shapes=[pltpu.VMEM((FRONT_TM, w), F32) for w in IN_WIDTHS] * 2
        + [pltpu.VMEM((SUBLANES, CONV_DIM), F32), pltpu.VMEM((SSD_STATE, SSD_WIDTH), F32)],
        compiler_params=_params(("arbitrary",)),
        name="front",
    )(x2d, mod, mod, ln_g, ln_b, *w_pieces, *consts, w_out)


SAMPLE_BB = CHUNK // DEC_SEQ


def _seg_cumsum(a, t):
    k = 1
    while k < DEC_SEQ:
        a = a + jnp.where(t >= k, pltpu.roll(a, k, 0), 0.0)
        k *= 2
    return a


def _seg_last(a, t):
    rows = a.shape[0]
    x = jnp.where(t == DEC_SEQ - 1, a, 0.0)
    k = 1
    while k < DEC_SEQ:
        x = x + pltpu.roll(x, rows - k, 0)
        k *= 2
    return x


def _sample_mixer_kernel(z_ref, xbc_ref, buf_ref, dt_ref, u_ref, v_ref, s_ref, conv_w_ref,
                         conv_b_ref, dtb_ref, alog_ref, d_ref, e_ref, tril3_ref, ng_ref, gg_ref,
                         gb_ref, ws_ref, bsb_ref,
                         y_ref, snew_ref, vn_ref,
                         c_scr, b_scr, xw_scr, aux_scr, yoff_scr):
    del tril3_ref
    rows = CHUNK
    tcol = lax.broadcasted_iota(jnp.int32, (rows, 1), 0) % DEC_SEQ

    xbc = xbc_ref[...]
    buf = buf_ref[...]
    taps = []
    for k in range(CONV_K - 1):
        back = CONV_K - 1 - k
        taps.append(jnp.where(tcol >= back, pltpu.roll(xbc, back, 0),
                              pltpu.roll(buf, rows - DEC_SEQ + back, 0)))
    taps.append(xbc)
    xc = _conv_silu(taps, conv_w_ref, conv_b_ref[...])

    dt = _softplus(dt_ref[...] + dtb_ref[...])
    a = dt * (-jnp.exp(alog_ref[...]))
    acum = _seg_cumsum(a, tcol)
    tot = _seg_last(acum, tcol)

    ri = lax.broadcasted_iota(jnp.int32, (rows, rows), 0)
    ci = lax.broadcasted_iota(jnp.int32, (rows, rows), 1)
    mask = jnp.logical_and(ci <= ri, (ci // DEC_SEQ) == (ri // DEC_SEQ))

    y, xw, y_gm, vn = _ssd_gmlp_block(
        xc=xc, acum=acum, tot=tot, dt=dt, d_row=d_ref[...], e3=e_ref[...], mask=mask,
        state_t=None, w_s_ref=ws_ref, bsb=bsb_ref[...], u=u_ref[...], v=v_ref[...],
        gm_g=gg_ref[...], gm_b=gb_ref[...])
    vn_ref[...] = vn

    tfull = lax.broadcasted_iota(jnp.int32, (rows, SSD_WIDTH), 0) % DEC_SEQ
    dcx = jnp.exp(_expand(tot, e_ref[...]))
    hi, mid, lo = _split3(dcx)
    aux_scr[...] = jnp.where(tfull == 0, hi.astype(F32),
                             jnp.where(tfull == 1, mid.astype(F32),
                                       jnp.where(tfull == 2, lo.astype(F32), 0.0)))
    xw_scr[...] = xw
    b_scr[...] = xc[:, SSD_WIDTH:SSD_WIDTH + SSD_GROUPS * SSD_STATE]
    c_scr[...] = xc[:, SSD_WIDTH + SSD_GROUPS * SSD_STATE:]

    r8 = lax.broadcasted_iota(jnp.int32, (DEC_SEQ, 2 * SSD_STATE), 0)
    l8 = lax.broadcasted_iota(jnp.int32, (DEC_SEQ, 2 * SSD_STATE), 1)
    ones_part = jnp.where(jnp.logical_and(r8 < 3, l8 >= SSD_STATE), 1.0, 0.0)
    zeros_b = jnp.zeros((DEC_SEQ, SSD_STATE), F32)

    def per_seq(b, carry):
        r0 = pl.multiple_of(b * DEC_SEQ, DEC_SEQ)
        rsl = pl.ds(r0, DEC_SEQ)
        for g in range(SSD_GROUPS):
            gs = slice(g * GROUP_WIDTH, (g + 1) * GROUP_WIDTH)
            ns = slice(g * SSD_STATE, (g + 1) * SSD_STATE)
            s_bg = s_ref[b, gs, :]
            yoff_scr[rsl, gs] = _dot_nt(c_scr[rsl, ns].astype(BF16), s_bg.astype(BF16))
            lhs = jnp.concatenate([xw_scr[rsl, gs], aux_scr[rsl, gs]], axis=0)
            rhs = jnp.concatenate(
                [jnp.concatenate([b_scr[rsl, ns], zeros_b], axis=1), ones_part], axis=0)
            res = _dot_tn(lhs.astype(BF16), rhs.astype(BF16))
            snew_ref[b, gs, :] = res[:, SSD_STATE:] * s_bg + res[:, :SSD_STATE]
        return carry

    lax.fori_loop(0, SAMPLE_BB, per_seq, 0)

    y = y + yoff_scr[...] * jnp.exp(_expand(acum, e_ref[...]))
    y_ssd = _gated_rmsnorm(y, z_ref[...], ng_ref[...])
    y_ref[...] = jnp.concatenate([y_ssd, y_gm], axis=1).astype(BF16)


def _sample_mixer_call(z, xbc, buf8, dt, u, v, state, consts):
    t = z.shape[0]
    nb = t // CHUNK
    row = lambda i: (i, 0)
    st_spec = pl.BlockSpec((SAMPLE_BB, SSD_WIDTH, SSD_STATE), lambda i: (i, 0, 0))
    const_specs = [_resident(a.shape) for a in consts]
    return pl.pallas_call(
        _sample_mixer_kernel,
        grid=(nb,),
        in_specs=[pl.BlockSpec((CHUNK, SSD_WIDTH), row),
                  pl.BlockSpec((CHUNK, CONV_DIM), row),
                  pl.BlockSpec((CHUNK, CONV_DIM), row),
                  pl.BlockSpec((CHUNK, DT_PAD), row),
                  pl.BlockSpec((CHUNK, GM_WIDTH), row),
                  pl.BlockSpec((CHUNK, GM_WIDTH), row),
                  st_spec] + const_specs,
        out_specs=[pl.BlockSpec((CHUNK, 2 * SSD_WIDTH), row), st_spec,
                   pl.BlockSpec((CHUNK, GM_WIDTH), row)],
        out_shape=[jax.ShapeDtypeStruct((t, 2 * SSD_WIDTH), BF16),
                   jax.ShapeDtypeStruct(state.shape, F32),
                   jax.ShapeDtypeStruct((t, GM_WIDTH), F32)],
        scratch_shapes=[pltpu.VMEM((CHUNK, SSD_GROUPS * SSD_STATE), F32),
                        pltpu.VMEM((CHUNK, SSD_GROUPS * SSD_STATE), F32),
                        pltpu.VMEM((CHUNK, SSD_WIDTH), F32),
                        pltpu.VMEM((CHUNK, SSD_WIDTH), F32),
                        pltpu.VMEM((CHUNK, SSD_WIDTH), F32)],
        compiler_params=_params(("arbitrary",)),
        name="mixer_sample",
    )(z, xbc, buf8, dt, u, v, state, *consts)


def _outln_kernel(*refs, per_seq, cast_ffn):
    it = iter(refs)
    x_ref = next(it)
    y_refs = [next(it)] if per_seq else [next(it), next(it)]
    g_ref = next(it)
    p3_ref = next(it) if per_seq else None
    ing_ref, inb_ref, w_ref, lg_ref, lb_ref = (next(it) for _ in range(5))
    w1f_ref = next(it) if cast_ffn else None
    o_ref = next(it)
    xn = _layer_norm(x_ref[...], ing_ref[...], inb_ref[...])
    y = y_refs[0][...] if per_seq else jnp.concatenate([r[...] for r in y_refs], axis=0)
    mix = _dot(y, w_ref[...])
    o_ref[...] = _layer_norm(ALPHA * xn + (1.0 + _mod_rows(g_ref, p3_ref)) * mix,
                             lg_ref[...], lb_ref[...])
    if cast_ffn:
        w1b_ref = next(it)
        for c in range(N_FF_TILES):
            w1b_ref[c] = w1f_ref[:, c * FF_TILE:(c + 1) * FF_TILE].astype(BF16)


def _outln_call(x2d, ymix, mod, rows_per_mod, per_seq, ln_in_g, ln_in_b, w_out, ln_g, ln_b, tm,
                w_ff1=None):
    t = x2d.shape[0]
    steps = t // tm
    row = lambda i: (i, 0)
    cast_ffn = w_ff1 is not None
    if per_seq:
        operands = [x2d, ymix, mod]
        y_specs = [pl.BlockSpec((tm, D_MODEL), row)]
    else:
        assert tm == 2 * FRONT_TM
        operands = [x2d, *ymix, mod]
        y_specs = [pl.BlockSpec((FRONT_TM, D_MODEL), row)] * 2
    in_specs = ([pl.BlockSpec((tm, D_MODEL), row)] + y_specs
                + _mod_specs(tm, rows_per_mod, per_seq, (2,)))
    if per_seq:
        operands.append(_repeat_matrix3(tm))
        in_specs.append(_resident((tm, _repeat_k(tm))))
    operands += [ln_in_g, ln_in_b, w_out, ln_g, ln_b]
    in_specs += [_resident((1, D_MODEL)), _resident((1, D_MODEL)), _resident(w_out.shape),
                 _resident((1, D_MODEL)), _resident((1, D_MODEL))]
    out_specs = [pl.BlockSpec((tm, D_MODEL), row)]
    out_shape = [jax.ShapeDtypeStruct((t, D_MODEL), F32)]
    if cast_ffn:
        band = D_MODEL // steps
        operands.append(w_ff1)
        in_specs.append(pl.BlockSpec((band, D_FF), row))
        out_specs.append(pl.BlockSpec((N_FF_TILES, band, FF_TILE), lambda i: (0, i, 0)))
        out_shape.append(jax.ShapeDtypeStruct((N_FF_TILES, D_MODEL, FF_TILE), BF16))
    return pl.pallas_call(
        functools.partial(_outln_kernel, per_seq=per_seq, cast_ffn=cast_ffn),
        grid=(steps,),
        in_specs=in_specs,
        out_specs=out_specs,
        out_shape=out_shape,
        compiler_params=_params(("arbitrary",)),
        name="out_ln",
    )(*operands)


def _ffn_kernel(*refs, per_seq):
    it = iter(refs)
    x_ref, sh_ref, sc_ref, g_ref = (next(it) for _ in range(4))
    p3_ref = next(it) if per_seq else None
    w1_ref, w2_ref, lg_ref, lb_ref, o_ref, h_scr = (next(it) for _ in range(6))
    j = pl.program_id(1)
    nj = pl.num_programs(1)

    @pl.when(j == 0)
    def _():
        h_scr[...] = (x_ref[...] * (1.0 + _mod_rows(sc_ref, p3_ref))
                      + _mod_rows(sh_ref, p3_ref)).astype(BF16)
        o_ref[...] = jnp.zeros_like(o_ref)

    a = jnp.maximum(_dot(h_scr[...], w1_ref[...]), 0.0)
    o_ref[...] += _dot((a * a).astype(BF16), w2_ref[...])

    @pl.when(j == nj - 1)
    def _():
        o_ref[...] = _layer_norm(ALPHA * x_ref[...] + (1.0 + _mod_rows(g_ref, p3_ref)) * o_ref[...],
                                 lg_ref[...], lb_ref[...])


def _ffn_call(x1, mod, rows_per_mod, per_seq, w1_tiles, w2, ln_g, ln_b, tm):
    t = x1.shape[0]
    row = lambda i, j: (i, 0)
    operands = [x1, mod, mod, mod]
    in_specs = [pl.BlockSpec((tm, D_MODEL), row)] + _mod_specs(tm, rows_per_mod, per_seq, (3, 4, 5))
    if per_seq:
        operands.append(_repeat_matrix3(tm))
        in_specs.append(_resident((tm, _repeat_k(tm))))
    operands += [w1_tiles, w2, ln_g, ln_b]
    in_specs += [pl.BlockSpec((None, D_MODEL, FF_TILE), lambda i, j: (j, 0, 0)),
                 pl.BlockSpec((FF_TILE, D_MODEL), lambda i, j: (j, 0)),
                 _resident((1, D_MODEL)), _resident((1, D_MODEL))]
    return pl.pallas_call(
        functools.partial(_ffn_kernel, per_seq=per_seq),
        grid=(t // tm, N_FF_TILES),
        in_specs=in_specs,
        out_specs=pl.BlockSpec((tm, D_MODEL), row),
        out_shape=jax.ShapeDtypeStruct((t, D_MODEL), F32),
        scratch_shapes=[pltpu.VMEM((tm, D_MODEL), BF16)],
        compiler_params=_params(("arbitrary", "arbitrary")),
        name="ffn",
    )(*operands)


def kernel(x_prompt, x_sample, state_ssm, state_conv, c_prompt, c_sample, ln_in_g, ln_in_b, w_mod, b_mod, w_in, conv_w, conv_b, dt_bias, a_log, d_skip, ssd_norm_g, gm_ln_g, gm_ln_b, gm_w_s, gm_b_s, w_out, ln_mix_g, ln_mix_b, w_ff1, w_ff2, ln_ffn_g, ln_ffn_b):
    depth = w_mod.shape[0]
    assert depth == 1
    bp, seq, _ = x_prompt.shape
    bs, dec, _ = x_sample.shape
    assert dec == DEC_SEQ and seq % CHUNK == 0 and (bs * dec) % CHUNK == 0

    r1 = lambda a: a.reshape(1, -1)
    ln_in_g2, ln_in_b2 = r1(ln_in_g), r1(ln_in_b)
    l = 0

    head_of_chan = jnp.arange(SSD_WIDTH, dtype=jnp.int32) // SSD_HEADDIM
    e_sel = (jnp.arange(DT_PAD, dtype=jnp.int32)[:, None] == head_of_chan[None, :]).astype(BF16)
    e3 = jnp.concatenate([e_sel, e_sel, e_sel], axis=0)
    tril = jnp.tril(jnp.ones((CHUNK, CHUNK), BF16))
    tril3 = jnp.concatenate([tril, tril, tril], axis=1)
    d_row = r1(jnp.repeat(d_skip[l], SSD_HEADDIM))
    dtb = _pad_cols(r1(dt_bias[l]), DT_PAD)
    alog = _pad_cols(r1(a_log[l]), DT_PAD)
    mixer_consts = [conv_w[l], r1(conv_b[l]), dtb, alog, d_row, e3, tril3, r1(ssd_norm_g[l]),
                    r1(gm_ln_g[l]), r1(gm_ln_b[l])]
    bsb_p = jnp.repeat(gm_b_s[l].T, GM_HEAD, axis=1)
    reps = CHUNK // DEC_SEQ
    ws_s = jnp.tile(gm_w_s[l][:, :DEC_SEQ, :DEC_SEQ], (1, reps, reps))
    bsb_s = jnp.tile(jnp.repeat(gm_b_s[l][:, :DEC_SEQ].T, GM_HEAD, axis=1), (reps, 1))

    n_c = bp + bs
    c_all = jnp.concatenate([c_prompt, c_sample], axis=0)
    c_all = jnp.pad(c_all, ((0, (-n_c) % SUBLANES), (0, 0)))
    mod, *w_pieces = _mod_call(c_all, w_mod[l], r1(b_mod[l]), w_in[l])
    mod_p = mod[:bp].reshape(bp, 1, 6 * D_MODEL)
    mod_s = mod[bp:n_c]

    xp2 = x_prompt.reshape(bp * seq, D_MODEL)
    y_even, y_odd, ssm_p, conv_p8, w_out_b = _front_call(
        xp2, mod_p, seq, ln_in_g2, ln_in_b2, w_pieces, mixer_consts + [gm_w_s[l], bsb_p], w_out[l])
    w2_b = w_ff2[l].astype(BF16)
    x1, w1_t = _outln_call(xp2, (y_even, y_odd), mod_p, seq, False, ln_in_g2, ln_in_b2,
                           w_out_b, r1(ln_mix_g[l]), r1(ln_mix_b[l]), tm=2 * FRONT_TM,
                           w_ff1=w_ff1[l])
    yp = _ffn_call(x1, mod_p, seq, False, w1_t, w2_b, r1(ln_ffn_g[l]), r1(ln_ffn_b[l]), tm=512)
    conv_p = conv_p8[:, SUBLANES - (CONV_K - 1):, :]

    xs2 = x_sample.reshape(bs * dec, D_MODEL)
    zs, xbcs, dtrs, us, vs = _inproj_call(xs2, mod_s, None, True, ln_in_g2, ln_in_b2, w_pieces, tm=256)
    buf8 = jnp.pad(state_conv[l], ((0, 0), (DEC_SEQ - (CONV_K - 1), 0), (0, 0)))
    buf8 = buf8.reshape(bs * dec, CONV_DIM)
    st_in = state_ssm[l].reshape(bs, SSD_WIDTH, SSD_STATE)
    ymix_s, ssm_s, vn_s = _sample_mixer_call(zs, xbcs, buf8, dtrs, us, vs, st_in,
                                             mixer_consts + [ws_s, bsb_s])
    (x1s,) = _outln_call(xs2, ymix_s, mod_s, None, True, ln_in_g2, ln_in_b2, w_out_b,
                         r1(ln_mix_g[l]), r1(ln_mix_b[l]), tm=512)
    ys = _ffn_call(x1s, mod_s, None, True, w1_t, w2_b, r1(ln_ffn_g[l]), r1(ln_ffn_b[l]), tm=512)
    conv_s = xbcs.reshape(bs, dec, CONV_DIM)[:, dec - (CONV_K - 1):, :]

    return (yp.reshape(bp, seq, D_MODEL),
            ys.reshape(bs, dec, D_MODEL),
            ssm_p.reshape(1, bp, SSD_HEADS, SSD_HEADDIM, SSD_STATE),
            conv_p[None],
            ssm_s.reshape(1, bs, SSD_HEADS, SSD_HEADDIM, SSD_STATE),
            conv_s[None],
            vn_s.reshape(1, bs, dec, GM_WIDTH))
```

```python
import functools
import math

import jax
import jax.numpy as jnp
from jax import lax
from jax.experimental import pallas as pl
from jax.experimental.pallas import tpu as pltpu

D_MODEL = 2048
SSD_WIDTH = 1024
SSD_HEADDIM = 64
SSD_HEADS = 16
SSD_GROUPS = 2
SSD_STATE = 128
GROUP_WIDTH = SSD_WIDTH // SSD_GROUPS
CONV_K = 4
CONV_DIM = SSD_WIDTH + 2 * SSD_GROUPS * SSD_STATE
GM_WIDTH = 1024
GM_HEAD = 128
GM_HEADS = 8
D_FF = 4 * D_MODEL
FF_TILE = 1024
N_FF_TILES = D_FF // FF_TILE
CHUNK = 128
DEC_SEQ = 8
DT_PAD = 128
SUBLANES = 8
LANES = 128
ALPHA = 2.0 ** 0.25
LN_EPS = 1e-5
LOG2E = math.log2(math.e)

V7X_VMEM_BYTES = 64 * 1024 * 1024
VMEM_LIMIT = V7X_VMEM_BYTES - 8 * 1024 * 1024

F32 = jnp.float32
BF16 = jnp.bfloat16


def _layer_norm(x, g, b):
    mu = jnp.mean(x, axis=-1, keepdims=True)
    xc = x - mu
    var = jnp.mean(xc * xc, axis=-1, keepdims=True)
    return xc * lax.rsqrt(var + LN_EPS) * g + b


def _silu(x):
    h = 0.5 * x
    return h + h * jnp.tanh(h)


def _gelu_tanh(x):
    c = math.sqrt(2.0 / math.pi)
    h = 0.5 * x
    return h + h * jnp.tanh(x * (c + (c * 0.044715) * (x * x)))


def _softplus(x):
    return jnp.maximum(x, 0.0) + jnp.log1p(jnp.exp(-jnp.abs(x)))


def _dot(a, b):
    return jnp.dot(a, b, preferred_element_type=F32)


def _dot_nt(a, b):
    return lax.dot_general(a, b, (((1,), (1,)), ((), ())), preferred_element_type=F32)


def _dot_tn(a, b):
    return lax.dot_general(a, b, (((0,), (0,)), ((), ())), preferred_element_type=F32)


def _split3(x):
    hi = x.astype(BF16)
    r1 = x - hi.astype(F32)
    mid = r1.astype(BF16)
    lo = (r1 - mid.astype(F32)).astype(BF16)
    return hi, mid, lo


def _expand(x, sel3):
    return _dot(jnp.concatenate(_split3(x), axis=1), sel3)


def _mod_rows(ref, p3_ref):
    if p3_ref is None:
        return ref[...]
    parts = list(_split3(ref[...]))
    pad = p3_ref.shape[1] - 3 * ref.shape[0]
    if pad:
        parts.append(jnp.zeros((pad, ref.shape[1]), BF16))
    return _dot(p3_ref[...], jnp.concatenate(parts, axis=0))


def _resident(shape):
    nd = len(shape)
    return pl.BlockSpec(shape, lambda *_: (0,) * nd, pipeline_mode=pl.Buffered(1))


def _mod_specs(tm, rows_per_mod, per_seq, pieces):
    if per_seq:
        return [pl.BlockSpec((tm // DEC_SEQ, D_MODEL), lambda i, *_, p=p: (i, p)) for p in pieces]
    tiles_per_mod = rows_per_mod // tm
    return [pl.BlockSpec((None, 1, D_MODEL), lambda i, *_, p=p: (i // tiles_per_mod, 0, p))
            for p in pieces]


def _repeat_matrix3(tm):
    nb = tm // DEC_SEQ
    sel = (jnp.arange(tm, dtype=jnp.int32)[:, None] // DEC_SEQ
           == jnp.arange(nb, dtype=jnp.int32)[None, :]).astype(BF16)
    return _pad_cols(jnp.concatenate([sel, sel, sel], axis=1), _repeat_k(tm))


def _repeat_k(tm):
    return -(-(3 * tm // DEC_SEQ) // LANES) * LANES


def _pad_cols(a, n):
    return jnp.pad(a, ((0, 0), (0, n - a.shape[1])))


def _params(sem):
    return pltpu.CompilerParams(dimension_semantics=sem, vmem_limit_bytes=VMEM_LIMIT)


def _mod_kernel(c_ref, w_ref, b_ref, o_ref):
    a = _silu(c_ref[...]).astype(BF16)
    o_ref[...] = _dot(a, w_ref[...].astype(BF16)) + b_ref[...]


def _mod_call(c_all, w_mod, b_mod):
    m = c_all.shape[0]
    n = w_mod.shape[1]
    tn = 1024
    return pl.pallas_call(
        _mod_kernel,
        grid=(n // tn,),
        in_specs=[
            pl.BlockSpec((m, D_MODEL), lambda j: (0, 0)),
            pl.BlockSpec((D_MODEL, tn), lambda j: (0, j)),
            pl.BlockSpec((1, tn), lambda j: (0, j)),
        ],
        out_specs=pl.BlockSpec((m, tn), lambda j: (0, j)),
        out_shape=jax.ShapeDtypeStruct((m, n), F32),
        compiler_params=_params(("arbitrary",)),
        name="mod",
    )(c_all, w_mod, b_mod)


IN_WIDTHS = (SSD_WIDTH, CONV_DIM, DT_PAD, GM_WIDTH, GM_WIDTH)
W_ALL = sum(IN_WIDTHS)
DT_BLOCK = (SSD_WIDTH + CONV_DIM) // LANES


def _win_kernel(a_ref, b_ref, o_ref):
    s = pl.program_id(0)
    a = a_ref[...]
    shifted = jnp.concatenate([a[SSD_HEADS:, :], b_ref[0:SSD_HEADS, :]], axis=0)
    blk = jnp.where(s <= DT_BLOCK, a, shifted)
    row = lax.broadcasted_iota(jnp.int32, blk.shape, 0)
    blk = jnp.where(jnp.logical_and(s == DT_BLOCK, row >= SSD_HEADS), 0.0, blk)
    o_ref[...] = blk.T.astype(BF16)


def _win_call(w_in_t):
    return pl.pallas_call(
        _win_kernel,
        grid=(W_ALL // LANES,),
        in_specs=[pl.BlockSpec((LANES, D_MODEL), lambda s: (jnp.where(s <= DT_BLOCK, s, s - 1), 0)),
                  pl.BlockSpec((LANES, D_MODEL), lambda s: (jnp.where(s <= DT_BLOCK, DT_BLOCK + 1, s), 0))],
        out_specs=pl.BlockSpec((D_MODEL, LANES), lambda s: (0, s)),
        out_shape=jax.ShapeDtypeStruct((D_MODEL, W_ALL), BF16),
        compiler_params=_params(("arbitrary",)),
        name="w_in_cast",
    )(w_in_t, w_in_t)


def _inproj_kernel(*refs, per_seq, cast_ffn):
    it = iter(refs)
    x_ref, sh_ref, sc_ref = next(it), next(it), next(it)
    p3_ref = next(it) if per_seq else None
    g_ref, b_ref = next(it), next(it)
    w_ref = next(it)
    if cast_ffn:
        w1f_ref, w2f_ref, wof_ref = next(it), next(it), next(it)
    o_refs = [next(it) for _ in range(len(IN_WIDTHS))]

    xn = _layer_norm(x_ref[...], g_ref[...], b_ref[...])
    h = (xn * (1.0 + _mod_rows(sc_ref, p3_ref)) + _mod_rows(sh_ref, p3_ref)).astype(BF16)
    off = 0
    for width, o_ref in zip(IN_WIDTHS, o_refs):
        o_ref[...] = _dot(h, w_ref[:, off:off + width])
        off += width

    if cast_ffn:
        w1b_ref, w2b_ref, wob_ref = next(it), next(it), next(it)
        for c in range(N_FF_TILES):
            w1b_ref[c] = w1f_ref[:, c * FF_TILE:(c + 1) * FF_TILE].astype(BF16)
        w2b_ref[...] = w2f_ref[...].astype(BF16)
        wob_ref[...] = wof_ref[...].astype(BF16)


def _inproj_call(x2d, mod, rows_per_mod, per_seq, ln_g, ln_b, w_all, tm, ffn_w=None):
    t = x2d.shape[0]
    steps = t // tm
    widths = IN_WIDTHS
    row = lambda i: (i, 0)
    cast_ffn = ffn_w is not None
    operands = [x2d, mod, mod]
    in_specs = [pl.BlockSpec((tm, D_MODEL), row)] + _mod_specs(tm, rows_per_mod, per_seq, (0, 1))
    if per_seq:
        operands.append(_repeat_matrix3(tm))
        in_specs.append(_resident((tm, _repeat_k(tm))))
    operands += [ln_g, ln_b, w_all]
    in_specs += [_resident((1, D_MODEL)), _resident((1, D_MODEL)), _resident(w_all.shape)]
    out_specs = [pl.BlockSpec((tm, n), row) for n in widths]
    out_shape = [jax.ShapeDtypeStruct((t, n), F32) for n in widths]
    if cast_ffn:
        r1, r2 = D_MODEL // steps, D_FF // steps
        operands += list(ffn_w)
        in_specs += [pl.BlockSpec((r1, D_FF), row), pl.BlockSpec((r2, D_MODEL), row),
                     pl.BlockSpec((r1, D_MODEL), row)]
        out_specs += [pl.BlockSpec((N_FF_TILES, r1, FF_TILE), lambda i: (0, i, 0)),
                      pl.BlockSpec((r2, D_MODEL), row), pl.BlockSpec((r1, D_MODEL), row)]
        out_shape += [jax.ShapeDtypeStruct((N_FF_TILES, D_MODEL, FF_TILE), BF16),
                      jax.ShapeDtypeStruct((D_FF, D_MODEL), BF16),
                      jax.ShapeDtypeStruct((D_MODEL, D_MODEL), BF16)]
    return pl.pallas_call(
        functools.partial(_inproj_kernel, per_seq=per_seq, cast_ffn=cast_ffn),
        grid=(steps,),
        in_specs=in_specs,
        out_specs=out_specs,
        out_shape=out_shape,
        compiler_params=_params(("arbitrary",)),
        name="in_proj",
    )(*operands)


def _ssd_gmlp_block(*, xc, acum, tot, dt, d_row, e3, mask, state_t, w_s_ref, bsb, u, v,
                    gm_g, gm_b):
    rows = xc.shape[0]
    xs = xc[:, :SSD_WIDTH]
    bm = xc[:, SSD_WIDTH:SSD_WIDTH + SSD_GROUPS * SSD_STATE]
    cm = xc[:, SSD_WIDTH + SSD_GROUPS * SSD_STATE:]

    acum2 = acum * LOG2E
    acum2_t = acum2.T
    dt_t = dt.T
    w_end = jnp.exp(tot - acum) * dt
    xw = xs * _expand(w_end, e3)
    eacum = jnp.exp(acum)

    xs_b = xs.astype(BF16)
    lane = lax.broadcasted_iota(jnp.int32, (rows, 2 * SSD_HEADDIM), 1)
    low_half = lane < SSD_HEADDIM

    scores = []
    for g in range(SSD_GROUPS):
        cg = cm[:, g * SSD_STATE:(g + 1) * SSD_STATE].astype(BF16)
        bg = bm[:, g * SSD_STATE:(g + 1) * SSD_STATE].astype(BF16)
        scores.append(_dot_nt(cg, bg))

    heads_per_group = SSD_HEADS // SSD_GROUPS
    y_pairs = []
    for pair in range(SSD_HEADS // 2):
        sl = slice(pair * 2 * SSD_HEADDIM, (pair + 1) * 2 * SSD_HEADDIM)
        x_pair = xs_b[:, sl]
        zero = jnp.zeros_like(x_pair)
        if state_t is not None:
            s_pair = state_t[:, sl].astype(BF16)
        acc = None
        for k in range(2):
            h = 2 * pair + k
            g = h // heads_per_group
            seg2 = acum2[:, h:h + 1] - acum2_t[h:h + 1, :]
            m = jnp.where(mask, scores[g] * jnp.exp2(seg2) * dt_t[h:h + 1, :], 0.0)
            keep = low_half if k == 0 else jnp.logical_not(low_half)
            rhs = jnp.where(keep, x_pair, zero)
            lhs = m.astype(BF16)
            if state_t is not None:
                c_sc = cm[:, g * SSD_STATE:(g + 1) * SSD_STATE] * eacum[:, h:h + 1]
                lhs = jnp.concatenate([lhs, c_sc.astype(BF16)], axis=1)
                rhs = jnp.concatenate([rhs, jnp.where(keep, s_pair, zero)], axis=0)
            part = _dot(lhs, rhs)
            acc = part if acc is None else acc + part
        y_pairs.append(acc)
    y = jnp.concatenate(y_pairs, axis=1) + d_row * xs

    ug = _gelu_tanh(u)
    vn = _layer_norm(_gelu_tanh(v), gm_g, gm_b)
    vn_b = vn.astype(BF16)
    mixed = []
    for h in range(GM_HEADS):
        w = jnp.where(mask, w_s_ref[h], 0.0).astype(BF16)
        mixed.append(_dot(w, vn_b[:, h * GM_HEAD:(h + 1) * GM_HEAD]))
    y_gm = ug * (jnp.concatenate(mixed, axis=1) + bsb)
    return y, xw, y_gm, vn


def _gated_rmsnorm(y, z, norm_g):
    hg = y * _silu(z)
    parts = []
    for g in range(SSD_GROUPS):
        hh = hg[:, g * GROUP_WIDTH:(g + 1) * GROUP_WIDTH]
        ms = jnp.mean(hh * hh, axis=-1, keepdims=True)
        parts.append(hh * lax.rsqrt(ms + LN_EPS))
    return jnp.concatenate(parts, axis=1) * norm_g


def _conv_silu(taps, conv_w_ref, conv_b):
    acc = conv_b + conv_w_ref[0:1, :] * taps[0]
    for k in range(1, CONV_K):
        acc = acc + conv_w_ref[k:k + 1, :] * taps[k]
    return _silu(acc)


def _prompt_mixer_kernel(z_ref, xbc_ref, dt_ref, u_ref, v_ref, conv_w_ref, conv_b_ref, dtb_ref,
                         alog_ref, d_ref, e_ref, tril3_ref, ng_ref, gg_ref, gb_ref, ws_ref, bsb_ref,
                         y_ref, ssm_ref, halo_ref, st_ref):
    c = pl.program_id(1)
    nc = pl.num_programs(1)

    @pl.when(c == 0)
    def _():
        halo_ref[...] = jnp.zeros_like(halo_ref)
        st_ref[...] = jnp.zeros_like(st_ref)

    xbc = xbc_ref[...]
    xp = jnp.concatenate([halo_ref[...], xbc], axis=0)
    taps = [pltpu.roll(xp, CONV_K - 1 - k, 0)[SUBLANES:, :] for k in range(CONV_K - 1)]
    taps.append(xbc)
    xc = _conv_silu(taps, conv_w_ref, conv_b_ref[...])
    halo_ref[...] = xbc[CHUNK - SUBLANES:, :]

    dt = _softplus(dt_ref[...] + dtb_ref[...])
    a = dt * (-jnp.exp(alog_ref[...]))
    ri = lax.broadcasted_iota(jnp.int32, (CHUNK, CHUNK), 0)
    ci = lax.broadcasted_iota(jnp.int32, (CHUNK, CHUNK), 1)
    causal = ci <= ri
    acum = _dot(tril3_ref[...], jnp.concatenate(_split3(a), axis=0))
    tot = jnp.broadcast_to(acum[CHUNK - 1:CHUNK, :], (CHUNK, DT_PAD))

    state_t = st_ref[...]
    y, xw, y_gm, _ = _ssd_gmlp_block(
        xc=xc, acum=acum, tot=tot, dt=dt, d_row=d_ref[...], e3=e_ref[...], mask=causal,
        state_t=state_t, w_s_ref=ws_ref, bsb=bsb_ref[...], u=u_ref[...], v=v_ref[...],
        gm_g=gg_ref[...], gm_b=gb_ref[...])

    cd = jnp.exp(_expand(tot[0:SUBLANES, :], e_ref[...]))[0:1, :]
    bm = xc[:, SSD_WIDTH:SSD_WIDTH + SSD_GROUPS * SSD_STATE].astype(BF16)
    xw_b = xw.astype(BF16)
    upd = [_dot_tn(bm[:, g * SSD_STATE:(g + 1) * SSD_STATE],
                   xw_b[:, g * GROUP_WIDTH:(g + 1) * GROUP_WIDTH]) for g in range(SSD_GROUPS)]
    new_state = state_t * cd + jnp.concatenate(upd, axis=1)
    st_ref[...] = new_state

    y_ssd = _gated_rmsnorm(y, z_ref[...], ng_ref[...])
    y_ref[...] = jnp.concatenate([y_ssd, y_gm], axis=1).astype(BF16)

    @pl.when(c == nc - 1)
    def _():
        ssm_ref[...] = new_state.T


def _prompt_mixer_call(z, xbc, dt, u, v, consts, batch, seq):
    nc = seq // CHUNK
    row = lambda b, c: (b * nc + c, 0)
    t = z.shape[0]
    const_specs = [_resident(a.shape) for a in consts]
    return pl.pallas_call(
        _prompt_mixer_kernel,
        grid=(batch, nc),
        in_specs=[pl.BlockSpec((CHUNK, SSD_WIDTH), row),
                  pl.BlockSpec((CHUNK, CONV_DIM), row),
                  pl.BlockSpec((CHUNK, DT_PAD), row),
                  pl.BlockSpec((CHUNK, GM_WIDTH), row),
                  pl.BlockSpec((CHUNK, GM_WIDTH), row)] + const_specs,
        out_specs=[pl.BlockSpec((CHUNK, 2 * SSD_WIDTH), row),
                   pl.BlockSpec((None, SSD_WIDTH, SSD_STATE), lambda b, c: (b, 0, 0))],
        out_shape=[jax.ShapeDtypeStruct((t, 2 * SSD_WIDTH), BF16),
                   jax.ShapeDtypeStruct((batch, SSD_WIDTH, SSD_STATE), F32)],
        scratch_shapes=[pltpu.VMEM((SUBLANES, CONV_DIM), F32),
                        pltpu.VMEM((SSD_STATE, SSD_WIDTH), F32)],
        compiler_params=_params(("arbitrary", "arbitrary")),
        name="mixer_prompt",
    )(z, xbc, dt, u, v, *consts)


SAMPLE_BB = CHUNK // DEC_SEQ
SEQ_UNROLL = 4


def _seg_cumsum(a, t):
    k = 1
    while k < DEC_SEQ:
        a = a + jnp.where(t >= k, pltpu.roll(a, k, 0), 0.0)
        k *= 2
    return a


def _seg_last(a, t):
    rows = a.shape[0]
    x = jnp.where(t == DEC_SEQ - 1, a, 0.0)
    k = 1
    while k < DEC_SEQ:
        x = x + pltpu.roll(x, rows - k, 0)
        k *= 2
    return x


def _sample_mixer_kernel(z_ref, xbc_ref, buf_ref, dt_ref, u_ref, v_ref, s_ref, conv_w_ref,
                         conv_b_ref, dtb_ref, alog_ref, d_ref, e_ref, tril3_ref, ng_ref, gg_ref,
                         gb_ref, ws_ref, bsb_ref,
                         y_ref, snew_ref, vn_ref,
                         c_scr, b_scr, xw_scr, aux_scr, yoff_scr):
    del tril3_ref
    rows = CHUNK
    tcol = lax.broadcasted_iota(jnp.int32, (rows, 1), 0) % DEC_SEQ

    xbc = xbc_ref[...]
    buf = buf_ref[...]
    taps = []
    for k in range(CONV_K - 1):
        back = CONV_K - 1 - k
        taps.append(jnp.where(tcol >= back, pltpu.roll(xbc, back, 0),
                              pltpu.roll(buf, rows - DEC_SEQ + back, 0)))
    taps.append(xbc)
    xc = _conv_silu(taps, conv_w_ref, conv_b_ref[...])

    dt = _softplus(dt_ref[...] + dtb_ref[...])
    a = dt * (-jnp.exp(alog_ref[...]))
    acum = _seg_cumsum(a, tcol)
    tot = _seg_last(acum, tcol)

    ri = lax.broadcasted_iota(jnp.int32, (rows, rows), 0)
    ci = lax.broadcasted_iota(jnp.int32, (rows, rows), 1)
    mask = jnp.logical_and(ci <= ri, (ci // DEC_SEQ) == (ri // DEC_SEQ))

    y, xw, y_gm, vn = _ssd_gmlp_block(
        xc=xc, acum=acum, tot=tot, dt=dt, d_row=d_ref[...], e3=e_ref[...], mask=mask,
        state_t=None, w_s_ref=ws_ref, bsb=bsb_ref[...], u=u_ref[...], v=v_ref[...],
        gm_g=gg_ref[...], gm_b=gb_ref[...])
    vn_ref[...] = vn

    tfull = lax.broadcasted_iota(jnp.int32, (rows, SSD_WIDTH), 0) % DEC_SEQ
    dcx = jnp.exp(_expand(tot, e_ref[...]))
    hi, mid, lo = _split3(dcx)
    aux_scr[...] = jnp.where(tfull == 0, hi.astype(F32),
                             jnp.where(tfull == 1, mid.astype(F32),
                                       jnp.where(tfull == 2, lo.astype(F32), 0.0)))
    xw_scr[...] = xw
    b_scr[...] = xc[:, SSD_WIDTH:SSD_WIDTH + SSD_GROUPS * SSD_STATE]
    c_scr[...] = xc[:, SSD_WIDTH + SSD_GROUPS * SSD_STATE:]

    r8 = lax.broadcasted_iota(jnp.int32, (DEC_SEQ, 2 * SSD_STATE), 0)
    l8 = lax.broadcasted_iota(jnp.int32, (DEC_SEQ, 2 * SSD_STATE), 1)
    ones_part = jnp.where(jnp.logical_and(r8 < 3, l8 >= SSD_STATE), 1.0, 0.0)
    zeros_b = jnp.zeros((DEC_SEQ, SSD_STATE), F32)

    def per_seq(b, carry):
        r0 = pl.multiple_of(b * DEC_SEQ, DEC_SEQ)
        rsl = pl.ds(r0, DEC_SEQ)
        for g in range(SSD_GROUPS):
            gs = slice(g * GROUP_WIDTH, (g + 1) * GROUP_WIDTH)
            ns = slice(g * SSD_STATE, (g + 1) * SSD_STATE)
            s_bg = s_ref[b, gs, :]
            yoff_scr[rsl, gs] = _dot_nt(c_scr[rsl, ns].astype(BF16), s_bg.astype(BF16))
            lhs = jnp.concatenate([xw_scr[rsl, gs], aux_scr[rsl, gs]], axis=0)
            rhs = jnp.concatenate(
                [jnp.concatenate([b_scr[rsl, ns], zeros_b], axis=1), ones_part], axis=0)
            res = _dot_tn(lhs.astype(BF16), rhs.astype(BF16))
            snew_ref[b, gs, :] = res[:, SSD_STATE:] * s_bg + res[:, :SSD_STATE]
        return carry

    lax.fori_loop(0, SAMPLE_BB, per_seq, 0, unroll=SEQ_UNROLL)

    y = y + yoff_scr[...] * jnp.exp(_expand(acum, e_ref[...]))
    y_ssd = _gated_rmsnorm(y, z_ref[...], ng_ref[...])
    y_ref[...] = jnp.concatenate([y_ssd, y_gm], axis=1).astype(BF16)


def _sample_mixer_call(z, xbc, buf8, dt, u, v, state, consts):
    t = z.shape[0]
    nb = t // CHUNK
    row = lambda i: (i, 0)
    st_spec = pl.BlockSpec((SAMPLE_BB, SSD_WIDTH, SSD_STATE), lambda i: (i, 0, 0))
    const_specs = [_resident(a.shape) for a in consts]
    return pl.pallas_call(
        _sample_mixer_kernel,
        grid=(nb,),
        in_specs=[pl.BlockSpec((CHUNK, SSD_WIDTH), row),
                  pl.BlockSpec((CHUNK, CONV_DIM), row),
                  pl.BlockSpec((CHUNK, CONV_DIM), row),
                  pl.BlockSpec((CHUNK, DT_PAD), row),
                  pl.BlockSpec((CHUNK, GM_WIDTH), row),
                  pl.BlockSpec((CHUNK, GM_WIDTH), row),
                  st_spec] + const_specs,
        out_specs=[pl.BlockSpec((CHUNK, 2 * SSD_WIDTH), row), st_spec,
                   pl.BlockSpec((CHUNK, GM_WIDTH), row)],
        out_shape=[jax.ShapeDtypeStruct((t, 2 * SSD_WIDTH), BF16),
                   jax.ShapeDtypeStruct(state.shape, F32),
                   jax.ShapeDtypeStruct((t, GM_WIDTH), F32)],
        scratch_shapes=[pltpu.VMEM((CHUNK, SSD_GROUPS * SSD_STATE), F32),
                        pltpu.VMEM((CHUNK, SSD_GROUPS * SSD_STATE), F32),
                        pltpu.VMEM((CHUNK, SSD_WIDTH), F32),
                        pltpu.VMEM((CHUNK, SSD_WIDTH), F32),
                        pltpu.VMEM((CHUNK, SSD_WIDTH), F32)],
        compiler_params=_params(("arbitrary",)),
        name="mixer_sample",
    )(z, xbc, buf8, dt, u, v, state, *consts)


def _outln_kernel(*refs, per_seq):
    it = iter(refs)
    x_ref, y_ref, g_ref = next(it), next(it), next(it)
    p3_ref = next(it) if per_seq else None
    ing_ref, inb_ref, w_ref, lg_ref, lb_ref, o_ref = (next(it) for _ in range(6))
    xn = _layer_norm(x_ref[...], ing_ref[...], inb_ref[...])
    mix = _dot(y_ref[...], w_ref[...])
    o_ref[...] = _layer_norm(ALPHA * xn + (1.0 + _mod_rows(g_ref, p3_ref)) * mix,
                             lg_ref[...], lb_ref[...])


def _outln_call(x2d, ymix, mod, rows_per_mod, per_seq, ln_in_g, ln_in_b, w_out, ln_g, ln_b, tm):
    t = x2d.shape[0]
    row = lambda i: (i, 0)
    operands = [x2d, ymix, mod]
    in_specs = ([pl.BlockSpec((tm, D_MODEL), row), pl.BlockSpec((tm, D_MODEL), row)]
                + _mod_specs(tm, rows_per_mod, per_seq, (2,)))
    if per_seq:
        operands.append(_repeat_matrix3(tm))
        in_specs.append(_resident((tm, _repeat_k(tm))))
    operands += [ln_in_g, ln_in_b, w_out, ln_g, ln_b]
    in_specs += [_resident((1, D_MODEL)), _resident((1, D_MODEL)), _resident(w_out.shape),
                 _resident((1, D_MODEL)), _resident((1, D_MODEL))]
    return pl.pallas_call(
        functools.partial(_outln_kernel, per_seq=per_seq),
        grid=(t // tm,),
        in_specs=in_specs,
        out_specs=pl.BlockSpec((tm, D_MODEL), row),
        out_shape=jax.ShapeDtypeStruct((t, D_MODEL), F32),
        compiler_params=_params(("arbitrary",)),
        name="out_ln",
    )(*operands)


def _ffn_kernel(*refs, per_seq):
    it = iter(refs)
    x_ref, sh_ref, sc_ref, g_ref = (next(it) for _ in range(4))
    p3_ref = next(it) if per_seq else None
    w1_ref, w2_ref, lg_ref, lb_ref, o_ref, h_scr = (next(it) for _ in range(6))
    j = pl.program_id(1)
    nj = pl.num_programs(1)

    @pl.when(j == 0)
    def _():
        h_scr[...] = (x_ref[...] * (1.0 + _mod_rows(sc_ref, p3_ref))
                      + _mod_rows(sh_ref, p3_ref)).astype(BF16)
        o_ref[...] = jnp.zeros_like(o_ref)

    a = jnp.maximum(_dot(h_scr[...], w1_ref[...]), 0.0)
    o_ref[...] += _dot((a * a).astype(BF16), w2_ref[...])

    @pl.when(j == nj - 1)
    def _():
        o_ref[...] = _layer_norm(ALPHA * x_ref[...] + (1.0 + _mod_rows(g_ref, p3_ref)) * o_ref[...],
                                 lg_ref[...], lb_ref[...])


def _ffn_call(x1, mod, rows_per_mod, per_seq, w1_tiles, w2, ln_g, ln_b, tm):
    t = x1.shape[0]
    row = lambda i, j: (i, 0)
    operands = [x1, mod, mod, mod]
    in_specs = [pl.BlockSpec((tm, D_MODEL), row)] + _mod_specs(tm, rows_per_mod, per_seq, (3, 4, 5))
    if per_seq:
        operands.append(_repeat_matrix3(tm))
        in_specs.append(_resident((tm, _repeat_k(tm))))
    operands += [w1_tiles, w2, ln_g, ln_b]
    in_specs += [pl.BlockSpec((None, D_MODEL, FF_TILE), lambda i, j: (j, 0, 0)),
                 pl.BlockSpec((FF_TILE, D_MODEL), lambda i, j: (j, 0)),
                 _resident((1, D_MODEL)), _resident((1, D_MODEL))]
    return pl.pallas_call(
        functools.partial(_ffn_kernel, per_seq=per_seq),
        grid=(t // tm, N_FF_TILES),
        in_specs=in_specs,
        out_specs=pl.BlockSpec((tm, D_MODEL), row),
        out_shape=jax.ShapeDtypeStruct((t, D_MODEL), F32),
        scratch_shapes=[pltpu.VMEM((tm, D_MODEL), BF16)],
        compiler_params=_params(("arbitrary", "arbitrary")),
        name="ffn",
    )(*operands)


def kernel(x_prompt, x_sample, state_ssm, state_conv, c_prompt, c_sample, ln_in_g, ln_in_b, w_mod, b_mod, w_in, conv_w, conv_b, dt_bias, a_log, d_skip, ssd_norm_g, gm_ln_g, gm_ln_b, gm_w_s, gm_b_s, w_out, ln_mix_g, ln_mix_b, w_ff1, w_ff2, ln_ffn_g, ln_ffn_b):
    depth = w_mod.shape[0]
    assert depth == 1
    bp, seq, _ = x_prompt.shape
    bs, dec, _ = x_sample.shape
    assert dec == DEC_SEQ and seq % CHUNK == 0 and (bs * dec) % CHUNK == 0

    r1 = lambda a: a.reshape(1, -1)
    ln_in_g2, ln_in_b2 = r1(ln_in_g), r1(ln_in_b)
    l = 0

    w_all = _win_call(jnp.swapaxes(w_in[l], 0, 1))

    head_of_chan = jnp.arange(SSD_WIDTH, dtype=jnp.int32) // SSD_HEADDIM
    e_sel = (jnp.arange(DT_PAD, dtype=jnp.int32)[:, None] == head_of_chan[None, :]).astype(BF16)
    e3 = jnp.concatenate([e_sel, e_sel, e_sel], axis=0)
    tril = jnp.tril(jnp.ones((CHUNK, CHUNK), BF16))
    tril3 = jnp.concatenate([tril, tril, tril], axis=1)
    d_row = r1(jnp.repeat(d_skip[l], SSD_HEADDIM))
    dtb = _pad_cols(r1(dt_bias[l]), DT_PAD)
    alog = _pad_cols(r1(a_log[l]), DT_PAD)
    mixer_consts = [conv_w[l], r1(conv_b[l]), dtb, alog, d_row, e3, tril3, r1(ssd_norm_g[l]),
                    r1(gm_ln_g[l]), r1(gm_ln_b[l])]
    bsb_p = jnp.repeat(gm_b_s[l].T, GM_HEAD, axis=1)
    reps = CHUNK // DEC_SEQ
    ws_s = jnp.tile(gm_w_s[l][:, :DEC_SEQ, :DEC_SEQ], (1, reps, reps))
    bsb_s = jnp.tile(jnp.repeat(gm_b_s[l][:, :DEC_SEQ].T, GM_HEAD, axis=1), (reps, 1))

    n_c = bp + bs
    c_all = jnp.concatenate([c_prompt, c_sample], axis=0)
    c_all = jnp.pad(c_all, ((0, (-n_c) % SUBLANES), (0, 0)))
    mod = _mod_call(c_all, w_mod[l], r1(b_mod[l]))
    mod_p = mod[:bp].reshape(bp, 1, 6 * D_MODEL)
    mod_s = mod[bp:n_c]

    xp2 = x_prompt.reshape(bp * seq, D_MODEL)
    z, xbc, dtr, u, v, w1_t, w2_b, w_out_b = _inproj_call(
        xp2, mod_p, seq, False, ln_in_g2, ln_in_b2, w_all, tm=256,
        ffn_w=(w_ff1[l], w_ff2[l], w_out[l]))
    ymix, ssm_p = _prompt_mixer_call(z, xbc, dtr, u, v, mixer_consts + [gm_w_s[l], bsb_p], bp, seq)
    x1 = _outln_call(xp2, ymix, mod_p, seq, False, ln_in_g2, ln_in_b2, w_out_b,
                     r1(ln_mix_g[l]), r1(ln_mix_b[l]), tm=512)
    yp = _ffn_call(x1, mod_p, seq, False, w1_t, w2_b, r1(ln_ffn_g[l]), r1(ln_ffn_b[l]), tm=512)
    conv_p = xbc.reshape(bp, seq, CONV_DIM)[:, seq - (CONV_K - 1):, :]

    xs2 = x_sample.reshape(bs * dec, D_MODEL)
    zs, xbcs, dtrs, us, vs = _inproj_call(xs2, mod_s, None, True, ln_in_g2, ln_in_b2, w_all, tm=256)
    buf8 = jnp.pad(state_conv[l], ((0, 0), (DEC_SEQ - (CONV_K - 1), 0), (0, 0)))
    buf8 = buf8.reshape(bs * dec, CONV_DIM)
    st_in = state_ssm[l].reshape(bs, SSD_WIDTH, SSD_STATE)
    ymix_s, ssm_s, vn_s = _sample_mixer_call(zs, xbcs, buf8, dtrs, us, vs, st_in,
                                             mixer_consts + [ws_s, bsb_s])
    x1s = _outln_call(xs2, ymix_s, mod_s, None, True, ln_in_g2, ln_in_b2, w_out_b,
                      r1(ln_mix_g[l]), r1(ln_mix_b[l]), tm=512)
    ys = _ffn_call(x1s, mod_s, None, True, w1_t, w2_b, r1(ln_ffn_g[l]), r1(ln_ffn_b[l]), tm=512)
    conv_s = xbcs.reshape(bs, dec, CONV_DIM)[:, dec - (CONV_K - 1):, :]

    return (yp.reshape(bp, seq, D_MODEL),
            ys.reshape(bs, dec, D_MODEL),
            ssm_p.reshape(1, bp, SSD_HEADS, SSD_HEADDIM, SSD_STATE),
            conv_p[None],
            ssm_s.reshape(1, bs, SSD_HEADS, SSD_HEADDIM, SSD_STATE),
            conv_s[None],
            vn_s.reshape(1, bs, dec, GM_WIDTH))
```

```python
import functools
import math

import jax
import jax.numpy as jnp
from jax import lax
from jax.experimental import pallas as pl
from jax.experimental.pallas import tpu as pltpu

D_MODEL = 2048
SSD_WIDTH = 1024
SSD_HEADDIM = 64
SSD_HEADS = 16
SSD_GROUPS = 2
SSD_STATE = 128
GROUP_WIDTH = SSD_WIDTH // SSD_GROUPS
CONV_K = 4
CONV_DIM = SSD_WIDTH + 2 * SSD_GROUPS * SSD_STATE
GM_WIDTH = 1024
GM_HEAD = 128
GM_HEADS = 8
D_FF = 4 * D_MODEL
FF_TILE = 1024
N_FF_TILES = D_FF // FF_TILE
CHUNK = 128
DEC_SEQ = 8
DT_PAD = 128
SUBLANES = 8
LANES = 128
ALPHA = 2.0 ** 0.25
LN_EPS = 1e-5
LOG2E = math.log2(math.e)

V7X_VMEM_BYTES = 64 * 1024 * 1024
VMEM_LIMIT = V7X_VMEM_BYTES - 8 * 1024 * 1024

F32 = jnp.float32
BF16 = jnp.bfloat16


def _layer_norm(x, g, b):
    mu = jnp.mean(x, axis=-1, keepdims=True)
    xc = x - mu
    var = jnp.mean(xc * xc, axis=-1, keepdims=True)
    return xc * lax.rsqrt(var + LN_EPS) * g + b


def _silu(x):
    h = 0.5 * x
    return h + h * jnp.tanh(h)


def _gelu_tanh(x):
    c = math.sqrt(2.0 / math.pi)
    h = 0.5 * x
    return h + h * jnp.tanh(x * (c + (c * 0.044715) * (x * x)))


def _softplus(x):
    return jnp.maximum(x, 0.0) + jnp.log1p(jnp.exp(-jnp.abs(x)))


def _dot(a, b):
    return jnp.dot(a, b, preferred_element_type=F32)


def _dot_nt(a, b):
    return lax.dot_general(a, b, (((1,), (1,)), ((), ())), preferred_element_type=F32)


def _dot_tn(a, b):
    return lax.dot_general(a, b, (((0,), (0,)), ((), ())), preferred_element_type=F32)


def _split3(x):
    hi = x.astype(BF16)
    r1 = x - hi.astype(F32)
    mid = r1.astype(BF16)
    lo = (r1 - mid.astype(F32)).astype(BF16)
    return hi, mid, lo


def _expand(x, sel3):
    return _dot(jnp.concatenate(_split3(x), axis=1), sel3)


def _mod_rows(ref, p3_ref):
    if p3_ref is None:
        return ref[...]
    parts = list(_split3(ref[...]))
    pad = p3_ref.shape[1] - 3 * ref.shape[0]
    if pad:
        parts.append(jnp.zeros((pad, ref.shape[1]), BF16))
    return _dot(p3_ref[...], jnp.concatenate(parts, axis=0))


def _resident(shape):
    nd = len(shape)
    return pl.BlockSpec(shape, lambda *_: (0,) * nd, pipeline_mode=pl.Buffered(1))


def _mod_specs(tm, rows_per_mod, per_seq, pieces):
    if per_seq:
        return [pl.BlockSpec((tm // DEC_SEQ, D_MODEL), lambda i, *_, p=p: (i, p)) for p in pieces]
    tiles_per_mod = rows_per_mod // tm
    return [pl.BlockSpec((None, 1, D_MODEL), lambda i, *_, p=p: (i // tiles_per_mod, 0, p))
            for p in pieces]


def _repeat_matrix3(tm):
    nb = tm // DEC_SEQ
    sel = (jnp.arange(tm, dtype=jnp.int32)[:, None] // DEC_SEQ
           == jnp.arange(nb, dtype=jnp.int32)[None, :]).astype(BF16)
    return _pad_cols(jnp.concatenate([sel, sel, sel], axis=1), _repeat_k(tm))


def _repeat_k(tm):
    return -(-(3 * tm // DEC_SEQ) // LANES) * LANES


def _pad_cols(a, n):
    return jnp.pad(a, ((0, 0), (0, n - a.shape[1])))


def _params(sem):
    return pltpu.CompilerParams(dimension_semantics=sem, vmem_limit_bytes=VMEM_LIMIT)


def _mod_kernel(c_ref, w_ref, b_ref, o_ref):
    a = _silu(c_ref[...]).astype(BF16)
    o_ref[...] = _dot(a, w_ref[...].astype(BF16)) + b_ref[...]


def _mod_call(c_all, w_mod, b_mod):
    m = c_all.shape[0]
    n = w_mod.shape[1]
    tn = 1024
    return pl.pallas_call(
        _mod_kernel,
        grid=(n // tn,),
        in_specs=[
            pl.BlockSpec((m, D_MODEL), lambda j: (0, 0)),
            pl.BlockSpec((D_MODEL, tn), lambda j: (0, j)),
            pl.BlockSpec((1, tn), lambda j: (0, j)),
        ],
        out_specs=pl.BlockSpec((m, tn), lambda j: (0, j)),
        out_shape=jax.ShapeDtypeStruct((m, n), F32),
        compiler_params=_params(("arbitrary",)),
        name="mod",
    )(c_all, w_mod, b_mod)


IN_WIDTHS = (SSD_WIDTH, CONV_DIM, DT_PAD, GM_WIDTH, GM_WIDTH)
W_ALL = sum(IN_WIDTHS)
DT_BLOCK = (SSD_WIDTH + CONV_DIM) // LANES


def _win_kernel(a_ref, b_ref, o_ref):
    s = pl.program_id(0)
    a = a_ref[...]
    shifted = jnp.concatenate([a[SSD_HEADS:, :], b_ref[0:SSD_HEADS, :]], axis=0)
    blk = jnp.where(s <= DT_BLOCK, a, shifted)
    row = lax.broadcasted_iota(jnp.int32, blk.shape, 0)
    blk = jnp.where(jnp.logical_and(s == DT_BLOCK, row >= SSD_HEADS), 0.0, blk)
    o_ref[...] = blk.T.astype(BF16)


def _win_call(w_in_t):
    return pl.pallas_call(
        _win_kernel,
        grid=(W_ALL // LANES,),
        in_specs=[pl.BlockSpec((LANES, D_MODEL), lambda s: (jnp.where(s <= DT_BLOCK, s, s - 1), 0)),
                  pl.BlockSpec((LANES, D_MODEL), lambda s: (jnp.where(s <= DT_BLOCK, DT_BLOCK + 1, s), 0))],
        out_specs=pl.BlockSpec((D_MODEL, LANES), lambda s: (0, s)),
        out_shape=jax.ShapeDtypeStruct((D_MODEL, W_ALL), BF16),
        compiler_params=_params(("arbitrary",)),
        name="w_in_cast",
    )(w_in_t, w_in_t)


def _inproj_kernel(*refs, per_seq, cast_ffn):
    it = iter(refs)
    x_ref, sh_ref, sc_ref = next(it), next(it), next(it)
    p3_ref = next(it) if per_seq else None
    g_ref, b_ref = next(it), next(it)
    w_ref = next(it)
    if cast_ffn:
        w1f_ref, w2f_ref, wof_ref = next(it), next(it), next(it)
    o_refs = [next(it) for _ in range(len(IN_WIDTHS))]

    xn = _layer_norm(x_ref[...], g_ref[...], b_ref[...])
    h = (xn * (1.0 + _mod_rows(sc_ref, p3_ref)) + _mod_rows(sh_ref, p3_ref)).astype(BF16)
    off = 0
    for width, o_ref in zip(IN_WIDTHS, o_refs):
        o_ref[...] = _dot(h, w_ref[:, off:off + width])
        off += width

    if cast_ffn:
        w1b_ref, w2b_ref, wob_ref = next(it), next(it), next(it)
        for c in range(N_FF_TILES):
            w1b_ref[c] = w1f_ref[:, c * FF_TILE:(c + 1) * FF_TILE].astype(BF16)
        w2b_ref[...] = w2f_ref[...].astype(BF16)
        wob_ref[...] = wof_ref[...].astype(BF16)


def _inproj_call(x2d, mod, rows_per_mod, per_seq, ln_g, ln_b, w_all, tm, ffn_w=None):
    t = x2d.shape[0]
    steps = t // tm
    widths = IN_WIDTHS
    row = lambda i: (i, 0)
    cast_ffn = ffn_w is not None
    operands = [x2d, mod, mod]
    in_specs = [pl.BlockSpec((tm, D_MODEL), row)] + _mod_specs(tm, rows_per_mod, per_seq, (0, 1))
    if per_seq:
        operands.append(_repeat_matrix3(tm))
        in_specs.append(_resident((tm, _repeat_k(tm))))
    operands += [ln_g, ln_b, w_all]
    in_specs += [_resident((1, D_MODEL)), _resident((1, D_MODEL)), _resident(w_all.shape)]
    out_specs = [pl.BlockSpec((tm, n), row) for n in widths]
    out_shape = [jax.ShapeDtypeStruct((t, n), F32) for n in widths]
    if cast_ffn:
        r1, r2 = D_MODEL // steps, D_FF // steps
        operands += list(ffn_w)
        in_specs += [pl.BlockSpec((r1, D_FF), row), pl.BlockSpec((r2, D_MODEL), row),
                     pl.BlockSpec((r1, D_MODEL), row)]
        out_specs += [pl.BlockSpec((N_FF_TILES, r1, FF_TILE), lambda i: (0, i, 0)),
                      pl.BlockSpec((r2, D_MODEL), row), pl.BlockSpec((r1, D_MODEL), row)]
        out_shape += [jax.ShapeDtypeStruct((N_FF_TILES, D_MODEL, FF_TILE), BF16),
                      jax.ShapeDtypeStruct((D_FF, D_MODEL), BF16),
                      jax.ShapeDtypeStruct((D_MODEL, D_MODEL), BF16)]
    return pl.pallas_call(
        functools.partial(_inproj_kernel, per_seq=per_seq, cast_ffn=cast_ffn),
        grid=(steps,),
        in_specs=in_specs,
        out_specs=out_specs,
        out_shape=out_shape,
        compiler_params=_params(("arbitrary",)),
        name="in_proj",
    )(*operands)


def _ssd_gmlp_block(*, xc, acum, tot, dt, d_row, e3, mask, state_t, w_s_ref, bsb, u, v,
                    gm_g, gm_b):
    rows = xc.shape[0]
    xs = xc[:, :SSD_WIDTH]
    bm = xc[:, SSD_WIDTH:SSD_WIDTH + SSD_GROUPS * SSD_STATE]
    cm = xc[:, SSD_WIDTH + SSD_GROUPS * SSD_STATE:]

    acum2 = acum * LOG2E
    acum2_t = acum2.T
    dt_t = dt.T
    w_end = jnp.exp(tot - acum) * dt
    xw = xs * _expand(w_end, e3)
    eacum = jnp.exp(acum)

    xs_b = xs.astype(BF16)
    lane = lax.broadcasted_iota(jnp.int32, (rows, 2 * SSD_HEADDIM), 1)
    low_half = lane < SSD_HEADDIM

    scores = []
    for g in range(SSD_GROUPS):
        cg = cm[:, g * SSD_STATE:(g + 1) * SSD_STATE].astype(BF16)
        bg = bm[:, g * SSD_STATE:(g + 1) * SSD_STATE].astype(BF16)
        scores.append(_dot_nt(cg, bg))

    heads_per_group = SSD_HEADS // SSD_GROUPS
    y_pairs = []
    for pair in range(SSD_HEADS // 2):
        sl = slice(pair * 2 * SSD_HEADDIM, (pair + 1) * 2 * SSD_HEADDIM)
        x_pair = xs_b[:, sl]
        zero = jnp.zeros_like(x_pair)
        if state_t is not None:
            s_pair = state_t[:, sl].astype(BF16)
        acc = None
        for k in range(2):
            h = 2 * pair + k
            g = h // heads_per_group
            seg2 = acum2[:, h:h + 1] - acum2_t[h:h + 1, :]
            m = jnp.where(mask, scores[g] * jnp.exp2(seg2) * dt_t[h:h + 1, :], 0.0)
            keep = low_half if k == 0 else jnp.logical_not(low_half)
            rhs = jnp.where(keep, x_pair, zero)
            lhs = m.astype(BF16)
            if state_t is not None:
                c_sc = cm[:, g * SSD_STATE:(g + 1) * SSD_STATE] * eacum[:, h:h + 1]
                lhs = jnp.concatenate([lhs, c_sc.astype(BF16)], axis=1)
                rhs = jnp.concatenate([rhs, jnp.where(keep, s_pair, zero)], axis=0)
            part = _dot(lhs, rhs)
            acc = part if acc is None else acc + part
        y_pairs.append(acc)
    y = jnp.concatenate(y_pairs, axis=1) + d_row * xs

    ug = _gelu_tanh(u)
    vn = _layer_norm(_gelu_tanh(v), gm_g, gm_b)
    vn_b = vn.astype(BF16)
    mixed = []
    for h in range(GM_HEADS):
        w = jnp.where(mask, w_s_ref[h], 0.0).astype(BF16)
        mixed.append(_dot(w, vn_b[:, h * GM_HEAD:(h + 1) * GM_HEAD]))
    y_gm = ug * (jnp.concatenate(mixed, axis=1) + bsb)
    return y, xw, y_gm, vn


def _gated_rmsnorm(y, z, norm_g):
    hg = y * _silu(z)
    parts = []
    for g in range(SSD_GROUPS):
        hh = hg[:, g * GROUP_WIDTH:(g + 1) * GROUP_WIDTH]
        ms = jnp.mean(hh * hh, axis=-1, keepdims=True)
        parts.append(hh * lax.rsqrt(ms + LN_EPS))
    return jnp.concatenate(parts, axis=1) * norm_g


def _conv_silu(taps, conv_w_ref, conv_b):
    acc = conv_b + conv_w_ref[0:1, :] * taps[0]
    for k in range(1, CONV_K):
        acc = acc + conv_w_ref[k:k + 1, :] * taps[k]
    return _silu(acc)


MIX_CHUNKS = 4


def _prompt_mixer_kernel(z_ref, xbc_ref, dt_ref, u_ref, v_ref, conv_w_ref, conv_b_ref, dtb_ref,
                         alog_ref, d_ref, e_ref, tril3_ref, ng_ref, gg_ref, gb_ref, ws_ref, bsb_ref,
                         y_ref, ssm_ref, halo_ref, st_ref):
    c = pl.program_id(1)
    nc = pl.num_programs(1)

    @pl.when(c == 0)
    def _():
        halo_ref[...] = jnp.zeros_like(halo_ref)
        st_ref[...] = jnp.zeros_like(st_ref)

    ri = lax.broadcasted_iota(jnp.int32, (CHUNK, CHUNK), 0)
    ci = lax.broadcasted_iota(jnp.int32, (CHUNK, CHUNK), 1)
    causal = ci <= ri
    halo = halo_ref[...]
    state_t = st_ref[...]
    for k in range(MIX_CHUNKS):
        rows = slice(k * CHUNK, (k + 1) * CHUNK)
        xbc = xbc_ref[rows, :]
        xp = jnp.concatenate([halo, xbc], axis=0)
        taps = [pltpu.roll(xp, CONV_K - 1 - j, 0)[SUBLANES:, :] for j in range(CONV_K - 1)]
        taps.append(xbc)
        xc = _conv_silu(taps, conv_w_ref, conv_b_ref[...])
        halo = xbc[CHUNK - SUBLANES:, :]

        dt = _softplus(dt_ref[rows, :] + dtb_ref[...])
        a = dt * (-jnp.exp(alog_ref[...]))
        acum = _dot(tril3_ref[...], jnp.concatenate(_split3(a), axis=0))
        tot = jnp.broadcast_to(acum[CHUNK - 1:CHUNK, :], (CHUNK, DT_PAD))

        y, xw, y_gm, _ = _ssd_gmlp_block(
            xc=xc, acum=acum, tot=tot, dt=dt, d_row=d_ref[...], e3=e_ref[...], mask=causal,
            state_t=state_t, w_s_ref=ws_ref, bsb=bsb_ref[...], u=u_ref[rows, :], v=v_ref[rows, :],
            gm_g=gg_ref[...], gm_b=gb_ref[...])

        cd = jnp.exp(_expand(tot[0:SUBLANES, :], e_ref[...]))[0:1, :]
        bm = xc[:, SSD_WIDTH:SSD_WIDTH + SSD_GROUPS * SSD_STATE].astype(BF16)
        xw_b = xw.astype(BF16)
        upd = [_dot_tn(bm[:, g * SSD_STATE:(g + 1) * SSD_STATE],
                       xw_b[:, g * GROUP_WIDTH:(g + 1) * GROUP_WIDTH]) for g in range(SSD_GROUPS)]
        state_t = state_t * cd + jnp.concatenate(upd, axis=1)

        y_ssd = _gated_rmsnorm(y, z_ref[rows, :], ng_ref[...])
        y_ref[rows, :] = jnp.concatenate([y_ssd, y_gm], axis=1).astype(BF16)

    halo_ref[...] = halo
    st_ref[...] = state_t

    @pl.when(c == nc - 1)
    def _():
        ssm_ref[...] = state_t.T


def _prompt_mixer_call(z, xbc, dt, u, v, consts, batch, seq):
    rows = MIX_CHUNKS * CHUNK
    nc = seq // rows
    row = lambda b, c: (b * nc + c, 0)
    t = z.shape[0]
    const_specs = [_resident(a.shape) for a in consts]
    return pl.pallas_call(
        _prompt_mixer_kernel,
        grid=(batch, nc),
        in_specs=[pl.BlockSpec((rows, SSD_WIDTH), row),
                  pl.BlockSpec((rows, CONV_DIM), row),
                  pl.BlockSpec((rows, DT_PAD), row),
                  pl.BlockSpec((rows, GM_WIDTH), row),
                  pl.BlockSpec((rows, GM_WIDTH), row)] + const_specs,
        out_specs=[pl.BlockSpec((rows, 2 * SSD_WIDTH), row),
                   pl.BlockSpec((None, SSD_WIDTH, SSD_STATE), lambda b, c: (b, 0, 0))],
        out_shape=[jax.ShapeDtypeStruct((t, 2 * SSD_WIDTH), BF16),
                   jax.ShapeDtypeStruct((batch, SSD_WIDTH, SSD_STATE), F32)],
        scratch_shapes=[pltpu.VMEM((SUBLANES, CONV_DIM), F32),
                        pltpu.VMEM((SSD_STATE, SSD_WIDTH), F32)],
        compiler_params=_params(("arbitrary", "arbitrary")),
        name="mixer_prompt",
    )(z, xbc, dt, u, v, *consts)


SAMPLE_BB = CHUNK // DEC_SEQ
SEQ_UNROLL = 4


def _seg_cumsum(a, t):
    k = 1
    while k < DEC_SEQ:
        a = a + jnp.where(t >= k, pltpu.roll(a, k, 0), 0.0)
        k *= 2
    return a


def _seg_last(a, t):
    rows = a.shape[0]
    x = jnp.where(t == DEC_SEQ - 1, a, 0.0)
    k = 1
    while k < DEC_SEQ:
        x = x + pltpu.roll(x, rows - k, 0)
        k *= 2
    return x


def _sample_mixer_kernel(z_ref, xbc_ref, buf_ref, dt_ref, u_ref, v_ref, s_ref, conv_w_ref,
                         conv_b_ref, dtb_ref, alog_ref, d_ref, e_ref, tril3_ref, ng_ref, gg_ref,
                         gb_ref, ws_ref, bsb_ref,
                         y_ref, snew_ref, vn_ref,
                         c_scr, b_scr, xw_scr, aux_scr, yoff_scr):
    del tril3_ref
    rows = CHUNK
    tcol = lax.broadcasted_iota(jnp.int32, (rows, 1), 0) % DEC_SEQ

    xbc = xbc_ref[...]
    buf = buf_ref[...]
    taps = []
    for k in range(CONV_K - 1):
        back = CONV_K - 1 - k
        taps.append(jnp.where(tcol >= back, pltpu.roll(xbc, back, 0),
                              pltpu.roll(buf, rows - DEC_SEQ + back, 0)))
    taps.append(xbc)
    xc = _conv_silu(taps, conv_w_ref, conv_b_ref[...])

    dt = _softplus(dt_ref[...] + dtb_ref[...])
    a = dt * (-jnp.exp(alog_ref[...]))
    acum = _seg_cumsum(a, tcol)
    tot = _seg_last(acum, tcol)

    ri = lax.broadcasted_iota(jnp.int32, (rows, rows), 0)
    ci = lax.broadcasted_iota(jnp.int32, (rows, rows), 1)
    mask = jnp.logical_and(ci <= ri, (ci // DEC_SEQ) == (ri // DEC_SEQ))

    y, xw, y_gm, vn = _ssd_gmlp_block(
        xc=xc, acum=acum, tot=tot, dt=dt, d_row=d_ref[...], e3=e_ref[...], mask=mask,
        state_t=None, w_s_ref=ws_ref, bsb=bsb_ref[...], u=u_ref[...], v=v_ref[...],
        gm_g=gg_ref[...], gm_b=gb_ref[...])
    vn_ref[...] = vn

    tfull = lax.broadcasted_iota(jnp.int32, (rows, SSD_WIDTH), 0) % DEC_SEQ
    dcx = jnp.exp(_expand(tot, e_ref[...]))
    hi, mid, lo = _split3(dcx)
    aux_scr[...] = jnp.where(tfull == 0, hi.astype(F32),
                             jnp.where(tfull == 1, mid.astype(F32),
                                       jnp.where(tfull == 2, lo.astype(F32), 0.0)))
    xw_scr[...] = xw
    b_scr[...] = xc[:, SSD_WIDTH:SSD_WIDTH + SSD_GROUPS * SSD_STATE]
    c_scr[...] = xc[:, SSD_WIDTH + SSD_GROUPS * SSD_STATE:]

    r8 = lax.broadcasted_iota(jnp.int32, (DEC_SEQ, 2 * SSD_STATE), 0)
    l8 = lax.broadcasted_iota(jnp.int32, (DEC_SEQ, 2 * SSD_STATE), 1)
    ones_part = jnp.where(jnp.logical_and(r8 < 3, l8 >= SSD_STATE), 1.0, 0.0)
    zeros_b = jnp.zeros((DEC_SEQ, SSD_STATE), F32)

    def per_seq(b, carry):
        r0 = pl.multiple_of(b * DEC_SEQ, DEC_SEQ)
        rsl = pl.ds(r0, DEC_SEQ)
        for g in range(SSD_GROUPS):
            gs = slice(g * GROUP_WIDTH, (g + 1) * GROUP_WIDTH)
            ns = slice(g * SSD_STATE, (g + 1) * SSD_STATE)
            s_bg = s_ref[b, gs, :]
            yoff_scr[rsl, gs] = _dot_nt(c_scr[rsl, ns].astype(BF16), s_bg.astype(BF16))
            lhs = jnp.concatenate([xw_scr[rsl, gs], aux_scr[rsl, gs]], axis=0)
            rhs = jnp.concatenate(
                [jnp.concatenate([b_scr[rsl, ns], zeros_b], axis=1), ones_part], axis=0)
            res = _dot_tn(lhs.astype(BF16), rhs.astype(BF16))
            snew_ref[b, gs, :] = res[:, SSD_STATE:] * s_bg + res[:, :SSD_STATE]
        return carry

    lax.fori_loop(0, SAMPLE_BB, per_seq, 0, unroll=SEQ_UNROLL)

    y = y + yoff_scr[...] * jnp.exp(_expand(acum, e_ref[...]))
    y_ssd = _gated_rmsnorm(y, z_ref[...], ng_ref[...])
    y_ref[...] = jnp.concatenate([y_ssd, y_gm], axis=1).astype(BF16)


def _sample_mixer_call(z, xbc, buf8, dt, u, v, state, consts):
    t = z.shape[0]
    nb = t // CHUNK
    row = lambda i: (i, 0)
    st_spec = pl.BlockSpec((SAMPLE_BB, SSD_WIDTH, SSD_STATE), lambda i: (i, 0, 0))
    const_specs = [_resident(a.shape) for a in consts]
    return pl.pallas_call(
        _sample_mixer_kernel,
        grid=(nb,),
        in_specs=[pl.BlockSpec((CHUNK, SSD_WIDTH), row),
                  pl.BlockSpec((CHUNK, CONV_DIM), row),
                  pl.BlockSpec((CHUNK, CONV_DIM), row),
                  pl.BlockSpec((CHUNK, DT_PAD), row),
                  pl.BlockSpec((CHUNK, GM_WIDTH), row),
                  pl.BlockSpec((CHUNK, GM_WIDTH), row),
                  st_spec] + const_specs,
        out_specs=[pl.BlockSpec((CHUNK, 2 * SSD_WIDTH), row), st_spec,
                   pl.BlockSpec((CHUNK, GM_WIDTH), row)],
        out_shape=[jax.ShapeDtypeStruct((t, 2 * SSD_WIDTH), BF16),
                   jax.ShapeDtypeStruct(state.shape, F32),
                   jax.ShapeDtypeStruct((t, GM_WIDTH), F32)],
        scratch_shapes=[pltpu.VMEM((CHUNK, SSD_GROUPS * SSD_STATE), F32),
                        pltpu.VMEM((CHUNK, SSD_GROUPS * SSD_STATE), F32),
                        pltpu.VMEM((CHUNK, SSD_WIDTH), F32),
                        pltpu.VMEM((CHUNK, SSD_WIDTH), F32),
                        pltpu.VMEM((CHUNK, SSD_WIDTH), F32)],
        compiler_params=_params(("arbitrary",)),
        name="mixer_sample",
    )(z, xbc, buf8, dt, u, v, state, *consts)


def _outln_kernel(*refs, per_seq):
    it = iter(refs)
    x_ref, y_ref, g_ref = next(it), next(it), next(it)
    p3_ref = next(it) if per_seq else None
    ing_ref, inb_ref, w_ref, lg_ref, lb_ref, o_ref = (next(it) for _ in range(6))
    xn = _layer_norm(x_ref[...], ing_ref[...], inb_ref[...])
    mix = _dot(y_ref[...], w_ref[...])
    o_ref[...] = _layer_norm(ALPHA * xn + (1.0 + _mod_rows(g_ref, p3_ref)) * mix,
                             lg_ref[...], lb_ref[...])


def _outln_call(x2d, ymix, mod, rows_per_mod, per_seq, ln_in_g, ln_in_b, w_out, ln_g, ln_b, tm):
    t = x2d.shape[0]
    row = lambda i: (i, 0)
    operands = [x2d, ymix, mod]
    in_specs = ([pl.BlockSpec((tm, D_MODEL), row), pl.BlockSpec((tm, D_MODEL), row)]
                + _mod_specs(tm, rows_per_mod, per_seq, (2,)))
    if per_seq:
        operands.append(_repeat_matrix3(tm))
        in_specs.append(_resident((tm, _repeat_k(tm))))
    operands += [ln_in_g, ln_in_b, w_out, ln_g, ln_b]
    in_specs += [_resident((1, D_MODEL)), _resident((1, D_MODEL)), _resident(w_out.shape),
                 _resident((1, D_MODEL)), _resident((1, D_MODEL))]
    return pl.pallas_call(
        functools.partial(_outln_kernel, per_seq=per_seq),
        grid=(t // tm,),
        in_specs=in_specs,
        out_specs=pl.BlockSpec((tm, D_MODEL), row),
        out_shape=jax.ShapeDtypeStruct((t, D_MODEL), F32),
        compiler_params=_params(("arbitrary",)),
        name="out_ln",
    )(*operands)


def _ffn_kernel(*refs, per_seq):
    it = iter(refs)
    x_ref, sh_ref, sc_ref, g_ref = (next(it) for _ in range(4))
    p3_ref = next(it) if per_seq else None
    w1_ref, w2_ref, lg_ref, lb_ref, o_ref, h_scr = (next(it) for _ in range(6))
    j = pl.program_id(1)
    nj = pl.num_programs(1)

    @pl.when(j == 0)
    def _():
        h_scr[...] = (x_ref[...] * (1.0 + _mod_rows(sc_ref, p3_ref))
                      + _mod_rows(sh_ref, p3_ref)).astype(BF16)
        o_ref[...] = jnp.zeros_like(o_ref)

    a = jnp.maximum(_dot(h_scr[...], w1_ref[...]), 0.0)
    o_ref[...] += _dot((a * a).astype(BF16), w2_ref[...])

    @pl.when(j == nj - 1)
    def _():
        o_ref[...] = _layer_norm(ALPHA * x_ref[...] + (1.0 + _mod_rows(g_ref, p3_ref)) * o_ref[...],
                                 lg_ref[...], lb_ref[...])


def _ffn_call(x1, mod, rows_per_mod, per_seq, w1_tiles, w2, ln_g, ln_b, tm):
    t = x1.shape[0]
    row = lambda i, j: (i, 0)
    operands = [x1, mod, mod, mod]
    in_specs = [pl.BlockSpec((tm, D_MODEL), row)] + _mod_specs(tm, rows_per_mod, per_seq, (3, 4, 5))
    if per_seq:
        operands.append(_repeat_matrix3(tm))
        in_specs.append(_resident((tm, _repeat_k(tm))))
    operands += [w1_tiles, w2, ln_g, ln_b]
    in_specs += [pl.BlockSpec((None, D_MODEL, FF_TILE), lambda i, j: (j, 0, 0)),
                 pl.BlockSpec((FF_TILE, D_MODEL), lambda i, j: (j, 0)),
                 _resident((1, D_MODEL)), _resident((1, D_MODEL))]
    return pl.pallas_call(
        functools.partial(_ffn_kernel, per_seq=per_seq),
        grid=(t // tm, N_FF_TILES),
        in_specs=in_specs,
        out_specs=pl.BlockSpec((tm, D_MODEL), row),
        out_shape=jax.ShapeDtypeStruct((t, D_MODEL), F32),
        scratch_shapes=[pltpu.VMEM((tm, D_MODEL), BF16)],
        compiler_params=_params(("arbitrary", "arbitrary")),
        name="ffn",
    )(*operands)


def kernel(x_prompt, x_sample, state_ssm, state_conv, c_prompt, c_sample, ln_in_g, ln_in_b, w_mod, b_mod, w_in, conv_w, conv_b, dt_bias, a_log, d_skip, ssd_norm_g, gm_ln_g, gm_ln_b, gm_w_s, gm_b_s, w_out, ln_mix_g, ln_mix_b, w_ff1, w_ff2, ln_ffn_g, ln_ffn_b):
    depth = w_mod.shape[0]
    assert depth == 1
    bp, seq, _ = x_prompt.shape
    bs, dec, _ = x_sample.shape
    assert dec == DEC_SEQ and seq % CHUNK == 0 and (bs * dec) % CHUNK == 0

    r1 = lambda a: a.reshape(1, -1)
    ln_in_g2, ln_in_b2 = r1(ln_in_g), r1(ln_in_b)
    l = 0

    w_all = _win_call(jnp.swapaxes(w_in[l], 0, 1))

    head_of_chan = jnp.arange(SSD_WIDTH, dtype=jnp.int32) // SSD_HEADDIM
    e_sel = (jnp.arange(DT_PAD, dtype=jnp.int32)[:, None] == head_of_chan[None, :]).astype(BF16)
    e3 = jnp.concatenate([e_sel, e_sel, e_sel], axis=0)
    tril = jnp.tril(jnp.ones((CHUNK, CHUNK), BF16))
    tril3 = jnp.concatenate([tril, tril, tril], axis=1)
    d_row = r1(jnp.repeat(d_skip[l], SSD_HEADDIM))
    dtb = _pad_cols(r1(dt_bias[l]), DT_PAD)
    alog = _pad_cols(r1(a_log[l]), DT_PAD)
    mixer_consts = [conv_w[l], r1(conv_b[l]), dtb, alog, d_row, e3, tril3, r1(ssd_norm_g[l]),
                    r1(gm_ln_g[l]), r1(gm_ln_b[l])]
    bsb_p = jnp.repeat(gm_b_s[l].T, GM_HEAD, axis=1)
    reps = CHUNK // DEC_SEQ
    ws_s = jnp.tile(gm_w_s[l][:, :DEC_SEQ, :DEC_SEQ], (1, reps, reps))
    bsb_s = jnp.tile(jnp.repeat(gm_b_s[l][:, :DEC_SEQ].T, GM_HEAD, axis=1), (reps, 1))

    n_c = bp + bs
    c_all = jnp.concatenate([c_prompt, c_sample], axis=0)
    c_all = jnp.pad(c_all, ((0, (-n_c) % SUBLANES), (0, 0)))
    mod = _mod_call(c_all, w_mod[l], r1(b_mod[l]))
    mod_p = mod[:bp].reshape(bp, 1, 6 * D_MODEL)
    mod_s = mod[bp:n_c]

    xp2 = x_prompt.reshape(bp * seq, D_MODEL)
    z, xbc, dtr, u, v, w1_t, w2_b, w_out_b = _inproj_call(
        xp2, mod_p, seq, False, ln_in_g2, ln_in_b2, w_all, tm=256,
        ffn_w=(w_ff1[l], w_ff2[l], w_out[l]))
    ymix, ssm_p = _prompt_mixer_call(z, xbc, dtr, u, v, mixer_consts + [gm_w_s[l], bsb_p], bp, seq)
    x1 = _outln_call(xp2, ymix, mod_p, seq, False, ln_in_g2, ln_in_b2, w_out_b,
                     r1(ln_mix_g[l]), r1(ln_mix_b[l]), tm=512)
    yp = _ffn_call(x1, mod_p, seq, False, w1_t, w2_b, r1(ln_ffn_g[l]), r1(ln_ffn_b[l]), tm=512)
    conv_p = xbc.reshape(bp, seq, CONV_DIM)[:, seq - (CONV_K - 1):, :]

    xs2 = x_sample.reshape(bs * dec, D_MODEL)
    zs, xbcs, dtrs, us, vs = _inproj_call(xs2, mod_s, None, True, ln_in_g2, ln_in_b2, w_all, tm=256)
    buf8 = jnp.pad(state_conv[l], ((0, 0), (DEC_SEQ - (CONV_K - 1), 0), (0, 0)))
    buf8 = buf8.reshape(bs * dec, CONV_DIM)
    st_in = state_ssm[l].reshape(bs, SSD_WIDTH, SSD_STATE)
    ymix_s, ssm_s, vn_s = _sample_mixer_call(zs, xbcs, buf8, dtrs, us, vs, st_in,
                                             mixer_consts + [ws_s, bsb_s])
    x1s = _outln_call(xs2, ymix_s, mod_s, None, True, ln_in_g2, ln_in_b2, w_out_b,
                      r1(ln_mix_g[l]), r1(ln_mix_b[l]), tm=512)
    ys = _ffn_call(x1s, mod_s, None, True, w1_t, w2_b, r1(ln_ffn_g[l]), r1(ln_ffn_b[l]), tm=512)
    conv_s = xbcs.reshape(bs, dec, CONV_DIM)[:, dec - (CONV_K - 1):, :]

    return (yp.reshape(bp, seq, D_MODEL),
            ys.reshape(bs, dec, D_MODEL),
            ssm_p.reshape(1, bp, SSD_HEADS, SSD_HEADDIM, SSD_STATE),
            conv_p[None],
            ssm_s.reshape(1, bs, SSD_HEADS, SSD_HEADDIM, SSD_STATE),
            conv_s[None],
            vn_s.reshape(1, bs, dec, GM_WIDTH))
```

```python
import functools
import math

import jax
import jax.numpy as jnp
from jax import lax
from jax.experimental import pallas as pl
from jax.experimental.pallas import tpu as pltpu

D_MODEL = 2048
SSD_WIDTH = 1024
SSD_HEADDIM = 64
SSD_HEADS = 16
SSD_GROUPS = 2
SSD_STATE = 128
GROUP_WIDTH = SSD_WIDTH // SSD_GROUPS
CONV_K = 4
CONV_DIM = SSD_WIDTH + 2 * SSD_GROUPS * SSD_STATE
GM_WIDTH = 1024
GM_HEAD = 128
GM_HEADS = 8
D_FF = 4 * D_MODEL
FF_TILE = 1024
N_FF_TILES = D_FF // FF_TILE
CHUNK = 128
DEC_SEQ = 8
DT_PAD = 128
SUBLANES = 8
LANES = 128
ALPHA = 2.0 ** 0.25
LN_EPS = 1e-5
LOG2E = math.log2(math.e)

V7X_VMEM_BYTES = 64 * 1024 * 1024
VMEM_LIMIT = V7X_VMEM_BYTES - 8 * 1024 * 1024

F32 = jnp.float32
BF16 = jnp.bfloat16


def _layer_norm(x, g, b):
    mu = jnp.mean(x, axis=-1, keepdims=True)
    xc = x - mu
    var = jnp.mean(xc * xc, axis=-1, keepdims=True)
    return xc * lax.rsqrt(var + LN_EPS) * g + b


def _silu(x):
    h = 0.5 * x
    return h + h * jnp.tanh(h)


def _gelu_tanh(x):
    c = math.sqrt(2.0 / math.pi)
    h = 0.5 * x
    return h + h * jnp.tanh(x * (c + (c * 0.044715) * (x * x)))


def _softplus(x):
    return jnp.maximum(x, 0.0) + jnp.log1p(jnp.exp(-jnp.abs(x)))


def _dot(a, b):
    return jnp.dot(a, b, preferred_element_type=F32)


def _dot_nt(a, b):
    return lax.dot_general(a, b, (((1,), (1,)), ((), ())), preferred_element_type=F32)


def _dot_tn(a, b):
    return lax.dot_general(a, b, (((0,), (0,)), ((), ())), preferred_element_type=F32)


def _split3(x):
    hi = x.astype(BF16)
    r1 = x - hi.astype(F32)
    mid = r1.astype(BF16)
    lo = (r1 - mid.astype(F32)).astype(BF16)
    return hi, mid, lo


def _expand(x, sel3):
    return _dot(jnp.concatenate(_split3(x), axis=1), sel3)


def _mod_rows(ref, p3_ref):
    if p3_ref is None:
        return ref[...]
    parts = list(_split3(ref[...]))
    pad = p3_ref.shape[1] - 3 * ref.shape[0]
    if pad:
        parts.append(jnp.zeros((pad, ref.shape[1]), BF16))
    return _dot(p3_ref[...], jnp.concatenate(parts, axis=0))


def _resident(shape):
    nd = len(shape)
    return pl.BlockSpec(shape, lambda *_: (0,) * nd, pipeline_mode=pl.Buffered(1))


def _mod_specs(tm, rows_per_mod, per_seq, pieces):
    if per_seq:
        return [pl.BlockSpec((tm // DEC_SEQ, D_MODEL), lambda i, *_, p=p: (i, p)) for p in pieces]
    tiles_per_mod = rows_per_mod // tm
    return [pl.BlockSpec((None, 1, D_MODEL), lambda i, *_, p=p: (i // tiles_per_mod, 0, p))
            for p in pieces]


def _repeat_matrix3(tm):
    nb = tm // DEC_SEQ
    sel = (jnp.arange(tm, dtype=jnp.int32)[:, None] // DEC_SEQ
           == jnp.arange(nb, dtype=jnp.int32)[None, :]).astype(BF16)
    return _pad_cols(jnp.concatenate([sel, sel, sel], axis=1), _repeat_k(tm))


def _repeat_k(tm):
    return -(-(3 * tm // DEC_SEQ) // LANES) * LANES


def _pad_cols(a, n):
    return jnp.pad(a, ((0, 0), (0, n - a.shape[1])))


def _params(sem):
    return pltpu.CompilerParams(dimension_semantics=sem, vmem_limit_bytes=VMEM_LIMIT)


def _mod_kernel(c_ref, w_ref, b_ref, o_ref):
    a = _silu(c_ref[...]).astype(BF16)
    o_ref[...] = _dot(a, w_ref[...].astype(BF16)) + b_ref[...]


def _mod_call(c_all, w_mod, b_mod):
    m = c_all.shape[0]
    n = w_mod.shape[1]
    tn = 1024
    return pl.pallas_call(
        _mod_kernel,
        grid=(n // tn,),
        in_specs=[
            pl.BlockSpec((m, D_MODEL), lambda j: (0, 0)),
            pl.BlockSpec((D_MODEL, tn), lambda j: (0, j)),
            pl.BlockSpec((1, tn), lambda j: (0, j)),
        ],
        out_specs=pl.BlockSpec((m, tn), lambda j: (0, j)),
        out_shape=jax.ShapeDtypeStruct((m, n), F32),
        compiler_params=_params(("arbitrary",)),
        name="mod",
    )(c_all, w_mod, b_mod)


IN_WIDTHS = (SSD_WIDTH, CONV_DIM, DT_PAD, GM_WIDTH, GM_WIDTH)
WA_COLS = SSD_WIDTH + CONV_DIM + DT_PAD
WB_COLS = 2 * GM_WIDTH
DT_ROW = SSD_WIDTH + CONV_DIM
UV_ROW = DT_ROW + SSD_HEADS
WA_BAND = 384
WB_BAND = 512
assert WA_COLS % WA_BAND == 0 and WB_COLS % WB_BAND == 0 and DT_ROW % WB_BAND == 0


def _win_a_kernel(a_ref, o_ref):
    blk = a_ref[...]
    row = lax.broadcasted_iota(jnp.int32, blk.shape, 0) + pl.program_id(0) * WA_BAND
    o_ref[...] = jnp.where(row < UV_ROW, blk, 0.0).T.astype(BF16)


def _win_b_kernel(a_ref, b_ref, o_ref):
    blk = jnp.concatenate([a_ref[SSD_HEADS:, :], b_ref[0:SSD_HEADS, :]], axis=0)
    o_ref[...] = blk.T.astype(BF16)


def _win_call(w_in_t):
    first_b = DT_ROW // WB_BAND
    w_a = pl.pallas_call(
        _win_a_kernel,
        grid=(WA_COLS // WA_BAND,),
        in_specs=[pl.BlockSpec((WA_BAND, D_MODEL), lambda s: (s, 0))],
        out_specs=pl.BlockSpec((D_MODEL, WA_BAND), lambda s: (0, s)),
        out_shape=jax.ShapeDtypeStruct((D_MODEL, WA_COLS), BF16),
        compiler_params=_params(("arbitrary",)),
        name="w_in_cast_a",
    )(w_in_t)
    w_b = pl.pallas_call(
        _win_b_kernel,
        grid=(WB_COLS // WB_BAND,),
        in_specs=[pl.BlockSpec((WB_BAND, D_MODEL), lambda s: (first_b + s, 0)),
                  pl.BlockSpec((WB_BAND, D_MODEL), lambda s: (first_b + s + 1, 0))],
        out_specs=pl.BlockSpec((D_MODEL, WB_BAND), lambda s: (0, s)),
        out_shape=jax.ShapeDtypeStruct((D_MODEL, WB_COLS), BF16),
        compiler_params=_params(("arbitrary",)),
        name="w_in_cast_b",
    )(w_in_t, w_in_t)
    return w_a, w_b


def _inproj_kernel(*refs, per_seq, cast_ffn):
    it = iter(refs)
    x_ref, sh_ref, sc_ref = next(it), next(it), next(it)
    p3_ref = next(it) if per_seq else None
    g_ref, b_ref = next(it), next(it)
    wa_ref, wb_ref = next(it), next(it)
    if cast_ffn:
        w1f_ref, w2f_ref, wof_ref = next(it), next(it), next(it)
    o_refs = [next(it) for _ in range(len(IN_WIDTHS))]

    xn = _layer_norm(x_ref[...], g_ref[...], b_ref[...])
    h = (xn * (1.0 + _mod_rows(sc_ref, p3_ref)) + _mod_rows(sh_ref, p3_ref)).astype(BF16)
    w_ref, off = wa_ref, 0
    for width, o_ref in zip(IN_WIDTHS, o_refs):
        if off == WA_COLS:
            w_ref, off = wb_ref, 0
        o_ref[...] = _dot(h, w_ref[:, off:off + width])
        off += width

    if cast_ffn:
        w1b_ref, w2b_ref, wob_ref = next(it), next(it), next(it)
        for c in range(N_FF_TILES):
            w1b_ref[c] = w1f_ref[:, c * FF_TILE:(c + 1) * FF_TILE].astype(BF16)
        w2b_ref[...] = w2f_ref[...].astype(BF16)
        wob_ref[...] = wof_ref[...].astype(BF16)


def _inproj_call(x2d, mod, rows_per_mod, per_seq, ln_g, ln_b, w_ab, tm, ffn_w=None):
    t = x2d.shape[0]
    steps = t // tm
    widths = IN_WIDTHS
    row = lambda i: (i, 0)
    cast_ffn = ffn_w is not None
    operands = [x2d, mod, mod]
    in_specs = [pl.BlockSpec((tm, D_MODEL), row)] + _mod_specs(tm, rows_per_mod, per_seq, (0, 1))
    if per_seq:
        operands.append(_repeat_matrix3(tm))
        in_specs.append(_resident((tm, _repeat_k(tm))))
    operands += [ln_g, ln_b, *w_ab]
    in_specs += [_resident((1, D_MODEL)), _resident((1, D_MODEL))] + [_resident(w.shape) for w in w_ab]
    out_specs = [pl.BlockSpec((tm, n), row) for n in widths]
    out_shape = [jax.ShapeDtypeStruct((t, n), F32) for n in widths]
    if cast_ffn:
        r1, r2 = D_MODEL // steps, D_FF // steps
        operands += list(ffn_w)
        in_specs += [pl.BlockSpec((r1, D_FF), row), pl.BlockSpec((r2, D_MODEL), row),
                     pl.BlockSpec((r1, D_MODEL), row)]
        out_specs += [pl.BlockSpec((N_FF_TILES, r1, FF_TILE), lambda i: (0, i, 0)),
                      pl.BlockSpec((r2, D_MODEL), row), pl.BlockSpec((r1, D_MODEL), row)]
        out_shape += [jax.ShapeDtypeStruct((N_FF_TILES, D_MODEL, FF_TILE), BF16),
                      jax.ShapeDtypeStruct((D_FF, D_MODEL), BF16),
                      jax.ShapeDtypeStruct((D_MODEL, D_MODEL), BF16)]
    return pl.pallas_call(
        functools.partial(_inproj_kernel, per_seq=per_seq, cast_ffn=cast_ffn),
        grid=(steps,),
        in_specs=in_specs,
        out_specs=out_specs,
        out_shape=out_shape,
        compiler_params=_params(("arbitrary",)),
        name="in_proj",
    )(*operands)


def _ssd_gmlp_block(*, xc, acum, tot, dt, d_row, e3, mask, state_t, wm_ref, bsb, u, v,
                    gm_g, gm_b):
    rows = xc.shape[0]
    xs = xc[:, :SSD_WIDTH]
    bm = xc[:, SSD_WIDTH:SSD_WIDTH + SSD_GROUPS * SSD_STATE]
    cm = xc[:, SSD_WIDTH + SSD_GROUPS * SSD_STATE:]

    acum2 = acum * LOG2E
    acum2_t = acum2.T
    dt_t = dt.T
    w_end = jnp.exp(tot - acum) * dt
    xw = xs * _expand(w_end, e3)
    eacum = jnp.exp(acum)

    xs_b = xs.astype(BF16)
    lane = lax.broadcasted_iota(jnp.int32, (rows, 2 * SSD_HEADDIM), 1)
    low_half = lane < SSD_HEADDIM

    scores = []
    for g in range(SSD_GROUPS):
        cg = cm[:, g * SSD_STATE:(g + 1) * SSD_STATE].astype(BF16)
        bg = bm[:, g * SSD_STATE:(g + 1) * SSD_STATE].astype(BF16)
        scores.append(_dot_nt(cg, bg))

    heads_per_group = SSD_HEADS // SSD_GROUPS
    y_pairs = []
    for pair in range(SSD_HEADS // 2):
        sl = slice(pair * 2 * SSD_HEADDIM, (pair + 1) * 2 * SSD_HEADDIM)
        x_pair = xs_b[:, sl]
        zero = jnp.zeros_like(x_pair)
        if state_t is not None:
            s_pair = state_t[:, sl].astype(BF16)
        acc = None
        for k in range(2):
            h = 2 * pair + k
            g = h // heads_per_group
            seg2 = acum2[:, h:h + 1] - acum2_t[h:h + 1, :]
            m = jnp.where(mask, scores[g] * jnp.exp2(seg2) * dt_t[h:h + 1, :], 0.0)
            keep = low_half if k == 0 else jnp.logical_not(low_half)
            rhs = jnp.where(keep, x_pair, zero)
            lhs = m.astype(BF16)
            if state_t is not None:
                c_sc = cm[:, g * SSD_STATE:(g + 1) * SSD_STATE] * eacum[:, h:h + 1]
                lhs = jnp.concatenate([lhs, c_sc.astype(BF16)], axis=1)
                rhs = jnp.concatenate([rhs, jnp.where(keep, s_pair, zero)], axis=0)
            part = _dot(lhs, rhs)
            acc = part if acc is None else acc + part
        y_pairs.append(acc)
    y = jnp.concatenate(y_pairs, axis=1) + d_row * xs

    ug = _gelu_tanh(u)
    vn = _layer_norm(_gelu_tanh(v), gm_g, gm_b)
    vn_b = vn.astype(BF16)
    mixed = []
    for h in range(GM_HEADS):
        mixed.append(_dot(wm_ref[h], vn_b[:, h * GM_HEAD:(h + 1) * GM_HEAD]))
    y_gm = ug * (jnp.concatenate(mixed, axis=1) + bsb)
    return y, xw, y_gm, vn


def _gated_rmsnorm(y, z, norm_g):
    hg = y * _silu(z)
    parts = []
    for g in range(SSD_GROUPS):
        hh = hg[:, g * GROUP_WIDTH:(g + 1) * GROUP_WIDTH]
        ms = jnp.mean(hh * hh, axis=-1, keepdims=True)
        parts.append(hh * lax.rsqrt(ms + LN_EPS))
    return jnp.concatenate(parts, axis=1) * norm_g


def _conv_silu(taps, conv_w_ref, conv_b):
    acc = conv_b + conv_w_ref[0:1, :] * taps[0]
    for k in range(1, CONV_K):
        acc = acc + conv_w_ref[k:k + 1, :] * taps[k]
    return _silu(acc)


MIX_CHUNKS = 4


def _prompt_mixer_kernel(z_ref, xbc_ref, dt_ref, u_ref, v_ref, conv_w_ref, conv_b_ref, dtb_ref,
                         alog_ref, d_ref, e_ref, tril3_ref, ng_ref, gg_ref, gb_ref, ws_ref, bsb_ref,
                         y_ref, ssm_ref, halo_ref, st_ref, wm_scr):
    c = pl.program_id(1)
    nc = pl.num_programs(1)
    ri = lax.broadcasted_iota(jnp.int32, (CHUNK, CHUNK), 0)
    ci = lax.broadcasted_iota(jnp.int32, (CHUNK, CHUNK), 1)
    causal = ci <= ri

    @pl.when(jnp.logical_and(pl.program_id(0) == 0, c == 0))
    def _():
        for h in range(GM_HEADS):
            wm_scr[h] = jnp.where(causal, ws_ref[h], 0.0).astype(BF16)

    @pl.when(c == 0)
    def _():
        halo_ref[...] = jnp.zeros_like(halo_ref)
        st_ref[...] = jnp.zeros_like(st_ref)

    halo = halo_ref[...]
    state_t = st_ref[...]
    for k in range(MIX_CHUNKS):
        rows = slice(k * CHUNK, (k + 1) * CHUNK)
        xbc = xbc_ref[rows, :]
        xp = jnp.concatenate([halo, xbc], axis=0)
        taps = [pltpu.roll(xp, CONV_K - 1 - j, 0)[SUBLANES:, :] for j in range(CONV_K - 1)]
        taps.append(xbc)
        xc = _conv_silu(taps, conv_w_ref, conv_b_ref[...])
        halo = xbc[CHUNK - SUBLANES:, :]

        dt = _softplus(dt_ref[rows, :] + dtb_ref[...])
        a = dt * (-jnp.exp(alog_ref[...]))
        acum = _dot(tril3_ref[...], jnp.concatenate(_split3(a), axis=0))
        tot = jnp.broadcast_to(acum[CHUNK - 1:CHUNK, :], (CHUNK, DT_PAD))

        y, xw, y_gm, _ = _ssd_gmlp_block(
            xc=xc, acum=acum, tot=tot, dt=dt, d_row=d_ref[...], e3=e_ref[...], mask=causal,
            state_t=state_t, wm_ref=wm_scr, bsb=bsb_ref[...], u=u_ref[rows, :], v=v_ref[rows, :],
            gm_g=gg_ref[...], gm_b=gb_ref[...])

        cd = jnp.exp(_expand(tot[0:SUBLANES, :], e_ref[...]))[0:1, :]
        bm = xc[:, SSD_WIDTH:SSD_WIDTH + SSD_GROUPS * SSD_STATE].astype(BF16)
        xw_b = xw.astype(BF16)
        upd = [_dot_tn(bm[:, g * SSD_STATE:(g + 1) * SSD_STATE],
                       xw_b[:, g * GROUP_WIDTH:(g + 1) * GROUP_WIDTH]) for g in range(SSD_GROUPS)]
        state_t = state_t * cd + jnp.concatenate(upd, axis=1)

        y_ssd = _gated_rmsnorm(y, z_ref[rows, :], ng_ref[...])
        y_ref[rows, :] = jnp.concatenate([y_ssd, y_gm], axis=1).astype(BF16)

    halo_ref[...] = halo
    st_ref[...] = state_t

    @pl.when(c == nc - 1)
    def _():
        ssm_ref[...] = state_t.T


def _prompt_mixer_call(z, xbc, dt, u, v, consts, batch, seq):
    rows = MIX_CHUNKS * CHUNK
    nc = seq // rows
    row = lambda b, c: (b * nc + c, 0)
    t = z.shape[0]
    const_specs = [_resident(a.shape) for a in consts]
    return pl.pallas_call(
        _prompt_mixer_kernel,
        grid=(batch, nc),
        in_specs=[pl.BlockSpec((rows, SSD_WIDTH), row),
                  pl.BlockSpec((rows, CONV_DIM), row),
                  pl.BlockSpec((rows, DT_PAD), row),
                  pl.BlockSpec((rows, GM_WIDTH), row),
                  pl.BlockSpec((rows, GM_WIDTH), row)] + const_specs,
        out_specs=[pl.BlockSpec((rows, 2 * SSD_WIDTH), row),
                   pl.BlockSpec((None, SSD_WIDTH, SSD_STATE), lambda b, c: (b, 0, 0))],
        out_shape=[jax.ShapeDtypeStruct((t, 2 * SSD_WIDTH), BF16),
                   jax.ShapeDtypeStruct((batch, SSD_WIDTH, SSD_STATE), F32)],
        scratch_shapes=[pltpu.VMEM((SUBLANES, CONV_DIM), F32),
                        pltpu.VMEM((SSD_STATE, SSD_WIDTH), F32),
                        pltpu.VMEM((GM_HEADS, CHUNK, CHUNK), BF16)],
        compiler_params=_params(("arbitrary", "arbitrary")),
        name="mixer_prompt",
    )(z, xbc, dt, u, v, *consts)


SAMPLE_BB = CHUNK // DEC_SEQ
SEQ_UNROLL = 4


def _seg_cumsum(a, t):
    k = 1
    while k < DEC_SEQ:
        a = a + jnp.where(t >= k, pltpu.roll(a, k, 0), 0.0)
        k *= 2
    return a


def _seg_last(a, t):
    rows = a.shape[0]
    x = jnp.where(t == DEC_SEQ - 1, a, 0.0)
    k = 1
    while k < DEC_SEQ:
        x = x + pltpu.roll(x, rows - k, 0)
        k *= 2
    return x


def _sample_mixer_kernel(z_ref, xbc_ref, buf_ref, dt_ref, u_ref, v_ref, s_ref, conv_w_ref,
                         conv_b_ref, dtb_ref, alog_ref, d_ref, e_ref, tril3_ref, ng_ref, gg_ref,
                         gb_ref, ws_ref, bsb_ref,
                         y_ref, snew_ref, vn_ref,
                         c_scr, b_scr, xw_scr, aux_scr, yoff_scr, wm_scr):
    del tril3_ref
    rows = CHUNK
    tcol = lax.broadcasted_iota(jnp.int32, (rows, 1), 0) % DEC_SEQ
    ri = lax.broadcasted_iota(jnp.int32, (rows, rows), 0)
    ci = lax.broadcasted_iota(jnp.int32, (rows, rows), 1)
    mask = jnp.logical_and(ci <= ri, (ci // DEC_SEQ) == (ri // DEC_SEQ))

    @pl.when(pl.program_id(0) == 0)
    def _():
        rep_r = (ci == ri % DEC_SEQ).astype(BF16)
        rep_c = (ri == ci % DEC_SEQ).astype(BF16)
        corner = jnp.logical_and(ri < DEC_SEQ, ci < DEC_SEQ)
        for h in range(GM_HEADS):
            w8 = jnp.where(corner, ws_ref[h], 0.0).astype(BF16)
            tiled = _dot(_dot(rep_r, w8).astype(BF16), rep_c)
            wm_scr[h] = jnp.where(mask, tiled, 0.0).astype(BF16)

    xbc = xbc_ref[...]
    buf = buf_ref[...]
    taps = []
    for k in range(CONV_K - 1):
        back = CONV_K - 1 - k
        taps.append(jnp.where(tcol >= back, pltpu.roll(xbc, back, 0),
                              pltpu.roll(buf, rows - DEC_SEQ + back, 0)))
    taps.append(xbc)
    xc = _conv_silu(taps, conv_w_ref, conv_b_ref[...])

    dt = _softplus(dt_ref[...] + dtb_ref[...])
    a = dt * (-jnp.exp(alog_ref[...]))
    acum = _seg_cumsum(a, tcol)
    tot = _seg_last(acum, tcol)

    y, xw, y_gm, vn = _ssd_gmlp_block(
        xc=xc, acum=acum, tot=tot, dt=dt, d_row=d_ref[...], e3=e_ref[...], mask=mask,
        state_t=None, wm_ref=wm_scr, bsb=bsb_ref[...], u=u_ref[...], v=v_ref[...],
        gm_g=gg_ref[...], gm_b=gb_ref[...])
    vn_ref[...] = vn

    tfull = lax.broadcasted_iota(jnp.int32, (rows, SSD_WIDTH), 0) % DEC_SEQ
    dcx = jnp.exp(_expand(tot, e_ref[...]))
    hi, mid, lo = _split3(dcx)
    aux_scr[...] = jnp.where(tfull == 0, hi.astype(F32),
                             jnp.where(tfull == 1, mid.astype(F32),
                                       jnp.where(tfull == 2, lo.astype(F32), 0.0)))
    xw_scr[...] = xw
    b_scr[...] = xc[:, SSD_WIDTH:SSD_WIDTH + SSD_GROUPS * SSD_STATE]
    c_scr[...] = xc[:, SSD_WIDTH + SSD_GROUPS * SSD_STATE:]

    r8 = lax.broadcasted_iota(jnp.int32, (DEC_SEQ, 2 * SSD_STATE), 0)
    l8 = lax.broadcasted_iota(jnp.int32, (DEC_SEQ, 2 * SSD_STATE), 1)
    ones_part = jnp.where(jnp.logical_and(r8 < 3, l8 >= SSD_STATE), 1.0, 0.0)
    zeros_b = jnp.zeros((DEC_SEQ, SSD_STATE), F32)

    def per_seq(b, carry):
        r0 = pl.multiple_of(b * DEC_SEQ, DEC_SEQ)
        rsl = pl.ds(r0, DEC_SEQ)
        for g in range(SSD_GROUPS):
            gs = slice(g * GROUP_WIDTH, (g + 1) * GROUP_WIDTH)
            ns = slice(g * SSD_STATE, (g + 1) * SSD_STATE)
            s_bg = s_ref[b, gs, :]
            yoff_scr[rsl, gs] = _dot_nt(c_scr[rsl, ns].astype(BF16), s_bg.astype(BF16))
            lhs = jnp.concatenate([xw_scr[rsl, gs], aux_scr[rsl, gs]], axis=0)
            rhs = jnp.concatenate(
                [jnp.concatenate([b_scr[rsl, ns], zeros_b], axis=1), ones_part], axis=0)
            res = _dot_tn(lhs.astype(BF16), rhs.astype(BF16))
            snew_ref[b, gs, :] = res[:, SSD_STATE:] * s_bg + res[:, :SSD_STATE]
        return carry

    lax.fori_loop(0, SAMPLE_BB, per_seq, 0, unroll=SEQ_UNROLL)

    y = y + yoff_scr[...] * jnp.exp(_expand(acum, e_ref[...]))
    y_ssd = _gated_rmsnorm(y, z_ref[...], ng_ref[...])
    y_ref[...] = jnp.concatenate([y_ssd, y_gm], axis=1).astype(BF16)


def _sample_mixer_call(z, xbc, buf8, dt, u, v, state, consts):
    t = z.shape[0]
    nb = t // CHUNK
    row = lambda i: (i, 0)
    st_spec = pl.BlockSpec((SAMPLE_BB, SSD_WIDTH, SSD_STATE), lambda i: (i, 0, 0))
    const_specs = [_resident(a.shape) for a in consts]
    return pl.pallas_call(
        _sample_mixer_kernel,
        grid=(nb,),
        in_specs=[pl.BlockSpec((CHUNK, SSD_WIDTH), row),
                  pl.BlockSpec((CHUNK, CONV_DIM), row),
                  pl.BlockSpec((CHUNK, CONV_DIM), row),
                  pl.BlockSpec((CHUNK, DT_PAD), row),
                  pl.BlockSpec((CHUNK, GM_WIDTH), row),
                  pl.BlockSpec((CHUNK, GM_WIDTH), row),
                  st_spec] + const_specs,
        out_specs=[pl.BlockSpec((CHUNK, 2 * SSD_WIDTH), row), st_spec,
                   pl.BlockSpec((CHUNK, GM_WIDTH), row)],
        out_shape=[jax.ShapeDtypeStruct((t, 2 * SSD_WIDTH), BF16),
                   jax.ShapeDtypeStruct(state.shape, F32),
                   jax.ShapeDtypeStruct((t, GM_WIDTH), F32)],
        scratch_shapes=[pltpu.VMEM((CHUNK, SSD_GROUPS * SSD_STATE), F32),
                        pltpu.VMEM((CHUNK, SSD_GROUPS * SSD_STATE), F32),
                        pltpu.VMEM((CHUNK, SSD_WIDTH), F32),
                        pltpu.VMEM((CHUNK, SSD_WIDTH), F32),
                        pltpu.VMEM((CHUNK, SSD_WIDTH), F32),
                        pltpu.VMEM((GM_HEADS, CHUNK, CHUNK), BF16)],
        compiler_params=_params(("arbitrary",)),
        name="mixer_sample",
    )(z, xbc, buf8, dt, u, v, state, *consts)


def _outln_kernel(*refs, per_seq):
    it = iter(refs)
    x_ref, y_ref, g_ref = next(it), next(it), next(it)
    p3_ref = next(it) if per_seq else None
    ing_ref, inb_ref, w_ref, lg_ref, lb_ref, o_ref = (next(it) for _ in range(6))
    xn = _layer_norm(x_ref[...], ing_ref[...], inb_ref[...])
    mix = _dot(y_ref[...], w_ref[...])
    o_ref[...] = _layer_norm(ALPHA * xn + (1.0 + _mod_rows(g_ref, p3_ref)) * mix,
                             lg_ref[...], lb_ref[...])


def _outln_call(x2d, ymix, mod, rows_per_mod, per_seq, ln_in_g, ln_in_b, w_out, ln_g, ln_b, tm):
    t = x2d.shape[0]
    row = lambda i: (i, 0)
    operands = [x2d, ymix, mod]
    in_specs = ([pl.BlockSpec((tm, D_MODEL), row), pl.BlockSpec((tm, D_MODEL), row)]
                + _mod_specs(tm, rows_per_mod, per_seq, (2,)))
    if per_seq:
        operands.append(_repeat_matrix3(tm))
        in_specs.append(_resident((tm, _repeat_k(tm))))
    operands += [ln_in_g, ln_in_b, w_out, ln_g, ln_b]
    in_specs += [_resident((1, D_MODEL)), _resident((1, D_MODEL)), _resident(w_out.shape),
                 _resident((1, D_MODEL)), _resident((1, D_MODEL))]
    return pl.pallas_call(
        functools.partial(_outln_kernel, per_seq=per_seq),
        grid=(t // tm,),
        in_specs=in_specs,
        out_specs=pl.BlockSpec((tm, D_MODEL), row),
        out_shape=jax.ShapeDtypeStruct((t, D_MODEL), F32),
        compiler_params=_params(("arbitrary",)),
        name="out_ln",
    )(*operands)


def _ffn_kernel(*refs, per_seq):
    it = iter(refs)
    x_ref, sh_ref, sc_ref, g_ref = (next(it) for _ in range(4))
    p3_ref = next(it) if per_seq else None
    w1_ref, w2_ref, lg_ref, lb_ref, o_ref, h_scr = (next(it) for _ in range(6))
    j = pl.program_id(1)
    nj = pl.num_programs(1)

    @pl.when(j == 0)
    def _():
        h_scr[...] = (x_ref[...] * (1.0 + _mod_rows(sc_ref, p3_ref))
                      + _mod_rows(sh_ref, p3_ref)).astype(BF16)
        o_ref[...] = jnp.zeros_like(o_ref)

    a = jnp.maximum(_dot(h_scr[...], w1_ref[...]), 0.0)
    o_ref[...] += _dot((a * a).astype(BF16), w2_ref[...])

    @pl.when(j == nj - 1)
    def _():
        o_ref[...] = _layer_norm(ALPHA * x_ref[...] + (1.0 + _mod_rows(g_ref, p3_ref)) * o_ref[...],
                                 lg_ref[...], lb_ref[...])


def _ffn_call(x1, mod, rows_per_mod, per_seq, w1_tiles, w2, ln_g, ln_b, tm):
    t = x1.shape[0]
    row = lambda i, j: (i, 0)
    operands = [x1, mod, mod, mod]
    in_specs = [pl.BlockSpec((tm, D_MODEL), row)] + _mod_specs(tm, rows_per_mod, per_seq, (3, 4, 5))
    if per_seq:
        operands.append(_repeat_matrix3(tm))
        in_specs.append(_resident((tm, _repeat_k(tm))))
    operands += [w1_tiles, w2, ln_g, ln_b]
    in_specs += [pl.BlockSpec((None, D_MODEL, FF_TILE), lambda i, j: (j, 0, 0)),
                 pl.BlockSpec((FF_TILE, D_MODEL), lambda i, j: (j, 0)),
                 _resident((1, D_MODEL)), _resident((1, D_MODEL))]
    return pl.pallas_call(
        functools.partial(_ffn_kernel, per_seq=per_seq),
        grid=(t // tm, N_FF_TILES),
        in_specs=in_specs,
        out_specs=pl.BlockSpec((tm, D_MODEL), row),
        out_shape=jax.ShapeDtypeStruct((t, D_MODEL), F32),
        scratch_shapes=[pltpu.VMEM((tm, D_MODEL), BF16)],
        compiler_params=_params(("arbitrary", "arbitrary")),
        name="ffn",
    )(*operands)


def kernel(x_prompt, x_sample, state_ssm, state_conv, c_prompt, c_sample, ln_in_g, ln_in_b, w_mod, b_mod, w_in, conv_w, conv_b, dt_bias, a_log, d_skip, ssd_norm_g, gm_ln_g, gm_ln_b, gm_w_s, gm_b_s, w_out, ln_mix_g, ln_mix_b, w_ff1, w_ff2, ln_ffn_g, ln_ffn_b):
    depth = w_mod.shape[0]
    assert depth == 1
    bp, seq, _ = x_prompt.shape
    bs, dec, _ = x_sample.shape
    assert dec == DEC_SEQ and seq % CHUNK == 0 and (bs * dec) % CHUNK == 0

    r1 = lambda a: a.reshape(1, -1)
    ln_in_g2, ln_in_b2 = r1(ln_in_g), r1(ln_in_b)
    l = 0

    w_ab = _win_call(jnp.swapaxes(w_in[l], 0, 1))

    head_of_chan = jnp.arange(SSD_WIDTH, dtype=jnp.int32) // SSD_HEADDIM
    e_sel = (jnp.arange(DT_PAD, dtype=jnp.int32)[:, None] == head_of_chan[None, :]).astype(BF16)
    e3 = jnp.concatenate([e_sel, e_sel, e_sel], axis=0)
    tril = jnp.tril(jnp.ones((CHUNK, CHUNK), BF16))
    tril3 = jnp.concatenate([tril, tril, tril], axis=1)
    d_row = r1(jnp.repeat(d_skip[l], SSD_HEADDIM))
    dtb = _pad_cols(r1(dt_bias[l]), DT_PAD)
    alog = _pad_cols(r1(a_log[l]), DT_PAD)
    mixer_consts = [conv_w[l], r1(conv_b[l]), dtb, alog, d_row, e3, tril3, r1(ssd_norm_g[l]),
                    r1(gm_ln_g[l]), r1(gm_ln_b[l])]
    bsb_p = jnp.repeat(gm_b_s[l].T, GM_HEAD, axis=1)
    reps = CHUNK // DEC_SEQ
    bsb_s = jnp.tile(jnp.repeat(gm_b_s[l][:, :DEC_SEQ].T, GM_HEAD, axis=1), (reps, 1))

    n_c = bp + bs
    c_all = jnp.concatenate([c_prompt, c_sample], axis=0)
    c_all = jnp.pad(c_all, ((0, (-n_c) % SUBLANES), (0, 0)))
    mod = _mod_call(c_all, w_mod[l], r1(b_mod[l]))
    mod_p = mod[:bp].reshape(bp, 1, 6 * D_MODEL)
    mod_s = mod[bp:n_c]

    xp2 = x_prompt.reshape(bp * seq, D_MODEL)
    z, xbc, dtr, u, v, w1_t, w2_b, w_out_b = _inproj_call(
        xp2, mod_p, seq, False, ln_in_g2, ln_in_b2, w_ab, tm=256,
        ffn_w=(w_ff1[l], w_ff2[l], w_out[l]))
    ymix, ssm_p = _prompt_mixer_call(z, xbc, dtr, u, v, mixer_consts + [gm_w_s[l], bsb_p], bp, seq)
    x1 = _outln_call(xp2, ymix, mod_p, seq, False, ln_in_g2, ln_in_b2, w_out_b,
                     r1(ln_mix_g[l]), r1(ln_mix_b[l]), tm=512)
    yp = _ffn_call(x1, mod_p, seq, False, w1_t, w2_b, r1(ln_ffn_g[l]), r1(ln_ffn_b[l]), tm=512)
    conv_p = xbc.reshape(bp, seq, CONV_DIM)[:, seq - (CONV_K - 1):, :]

    xs2 = x_sample.reshape(bs * dec, D_MODEL)
    zs, xbcs, dtrs, us, vs = _inproj_call(xs2, mod_s, None, True, ln_in_g2, ln_in_b2, w_ab, tm=256)
    buf8 = jnp.pad(state_conv[l], ((0, 0), (DEC_SEQ - (CONV_K - 1), 0), (0, 0)))
    buf8 = buf8.reshape(bs * dec, CONV_DIM)
    st_in = state_ssm[l].reshape(bs, SSD_WIDTH, SSD_STATE)
    ymix_s, ssm_s, vn_s = _sample_mixer_call(zs, xbcs, buf8, dtrs, us, vs, st_in,
                                             mixer_consts + [gm_w_s[l], bsb_s])
    x1s = _outln_call(xs2, ymix_s, mod_s, None, True, ln_in_g2, ln_in_b2, w_out_b,
                      r1(ln_mix_g[l]), r1(ln_mix_b[l]), tm=512)
    ys = _ffn_call(x1s, mod_s, None, True, w1_t, w2_b, r1(ln_ffn_g[l]), r1(ln_ffn_b[l]), tm=512)
    conv_s = xbcs.reshape(bs, dec, CONV_DIM)[:, dec - (CONV_K - 1):, :]

    return (yp.reshape(bp, seq, D_MODEL),
            ys.reshape(bs, dec, D_MODEL),
            ssm_p.reshape(1, bp, SSD_HEADS, SSD_HEADDIM, SSD_STATE),
            conv_p[None],
            ssm_s.reshape(1, bs, SSD_HEADS, SSD_HEADDIM, SSD_STATE),
            conv_s[None],
            vn_s.reshape(1, bs, dec, GM_WIDTH))
```

```python
import functools
import math

import jax
import jax.numpy as jnp
from jax import lax
from jax.experimental import pallas as pl
from jax.experimental.pallas import tpu as pltpu

D_MODEL = 2048
SSD_WIDTH = 1024
SSD_HEADDIM = 64
SSD_HEADS = 16
SSD_GROUPS = 2
SSD_STATE = 128
GROUP_WIDTH = SSD_WIDTH // SSD_GROUPS
CONV_K = 4
CONV_DIM = SSD_WIDTH + 2 * SSD_GROUPS * SSD_STATE
GM_WIDTH = 1024
GM_HEAD = 128
GM_HEADS = 8
D_FF = 4 * D_MODEL
FF_TILE = 1024
N_FF_TILES = D_FF // FF_TILE
CHUNK = 128
DEC_SEQ = 8
DT_PAD = 128
SUBLANES = 8
LANES = 128
ALPHA = 2.0 ** 0.25
LN_EPS = 1e-5
LOG2E = math.log2(math.e)

V7X_VMEM_BYTES = 64 * 1024 * 1024
VMEM_LIMIT = V7X_VMEM_BYTES - 8 * 1024 * 1024

F32 = jnp.float32
BF16 = jnp.bfloat16


def _layer_norm(x, g, b):
    mu = jnp.mean(x, axis=-1, keepdims=True)
    xc = x - mu
    var = jnp.mean(xc * xc, axis=-1, keepdims=True)
    return xc * lax.rsqrt(var + LN_EPS) * g + b


def _silu(x):
    h = 0.5 * x
    return h + h * jnp.tanh(h)


def _gelu_tanh(x):
    c = math.sqrt(2.0 / math.pi)
    h = 0.5 * x
    return h + h * jnp.tanh(x * (c + (c * 0.044715) * (x * x)))


def _softplus(x):
    return jnp.maximum(x, 0.0) + jnp.log1p(jnp.exp(-jnp.abs(x)))


def _dot(a, b):
    return jnp.dot(a, b, preferred_element_type=F32)


def _dot_nt(a, b):
    return lax.dot_general(a, b, (((1,), (1,)), ((), ())), preferred_element_type=F32)


def _dot_tn(a, b):
    return lax.dot_general(a, b, (((0,), (0,)), ((), ())), preferred_element_type=F32)


def _split3(x):
    hi = x.astype(BF16)
    r1 = x - hi.astype(F32)
    mid = r1.astype(BF16)
    lo = (r1 - mid.astype(F32)).astype(BF16)
    return hi, mid, lo


def _expand(x, sel3):
    return _dot(jnp.concatenate(_split3(x), axis=1), sel3)


def _mod_rows(ref, p3_ref):
    if p3_ref is None:
        return ref[...]
    parts = list(_split3(ref[...]))
    pad = p3_ref.shape[1] - 3 * ref.shape[0]
    if pad:
        parts.append(jnp.zeros((pad, ref.shape[1]), BF16))
    return _dot(p3_ref[...], jnp.concatenate(parts, axis=0))


def _resident(shape):
    nd = len(shape)
    return pl.BlockSpec(shape, lambda *_: (0,) * nd, pipeline_mode=pl.Buffered(1))


def _mod_specs(tm, rows_per_mod, per_seq, pieces):
    if per_seq:
        return [pl.BlockSpec((tm // DEC_SEQ, D_MODEL), lambda i, *_, p=p: (i, p)) for p in pieces]
    tiles_per_mod = rows_per_mod // tm
    return [pl.BlockSpec((None, 1, D_MODEL), lambda i, *_, p=p: (i // tiles_per_mod, 0, p))
            for p in pieces]


def _repeat_matrix3(tm):
    nb = tm // DEC_SEQ
    sel = (jnp.arange(tm, dtype=jnp.int32)[:, None] // DEC_SEQ
           == jnp.arange(nb, dtype=jnp.int32)[None, :]).astype(BF16)
    return _pad_cols(jnp.concatenate([sel, sel, sel], axis=1), _repeat_k(tm))


def _repeat_k(tm):
    return -(-(3 * tm // DEC_SEQ) // LANES) * LANES


def _pad_cols(a, n):
    return jnp.pad(a, ((0, 0), (0, n - a.shape[1])))


def _params(sem):
    return pltpu.CompilerParams(dimension_semantics=sem, vmem_limit_bytes=VMEM_LIMIT)


MOD_EARLY = 2
MOD_LATE = 4


def _mod_kernel(c_ref, w_ref, b_ref, o_ref, a_ref):
    a = _silu(c_ref[...]).astype(BF16)
    a_ref[...] = a
    o_ref[...] = _dot(a, w_ref[...].astype(BF16)) + b_ref[...]


def _mod_call(c_all, w_mod, b_mod):
    m = c_all.shape[0]
    tn = 1024
    return pl.pallas_call(
        _mod_kernel,
        grid=(MOD_EARLY * D_MODEL // tn,),
        in_specs=[
            pl.BlockSpec((m, D_MODEL), lambda j: (0, 0)),
            pl.BlockSpec((D_MODEL, tn), lambda j: (0, j)),
            pl.BlockSpec((1, tn), lambda j: (0, j)),
        ],
        out_specs=[pl.BlockSpec((m, tn), lambda j: (0, j)),
                   pl.BlockSpec((m, D_MODEL), lambda j: (0, 0))],
        out_shape=[jax.ShapeDtypeStruct((m, MOD_EARLY * D_MODEL), F32),
                   jax.ShapeDtypeStruct((m, D_MODEL), BF16)],
        compiler_params=_params(("arbitrary",)),
        name="mod",
    )(c_all, w_mod, b_mod)


IN_WIDTHS = (SSD_WIDTH, CONV_DIM, DT_PAD, GM_WIDTH, GM_WIDTH)
WA_COLS = SSD_WIDTH + CONV_DIM + DT_PAD
WB_COLS = 2 * GM_WIDTH
DT_ROW = SSD_WIDTH + CONV_DIM
UV_ROW = DT_ROW + SSD_HEADS
WA_BAND = 384
WB_BAND = 512
assert WA_COLS % WA_BAND == 0 and WB_COLS % WB_BAND == 0 and DT_ROW % WB_BAND == 0


def _win_a_kernel(a_ref, o_ref):
    blk = a_ref[...]
    row = lax.broadcasted_iota(jnp.int32, blk.shape, 0) + pl.program_id(0) * WA_BAND
    o_ref[...] = jnp.where(row < UV_ROW, blk, 0.0).T.astype(BF16)


def _win_b_kernel(a_ref, b_ref, o_ref):
    blk = jnp.concatenate([a_ref[SSD_HEADS:, :], b_ref[0:SSD_HEADS, :]], axis=0)
    o_ref[...] = blk.T.astype(BF16)


def _win_call(w_in_t):
    first_b = DT_ROW // WB_BAND
    w_a = pl.pallas_call(
        _win_a_kernel,
        grid=(WA_COLS // WA_BAND,),
        in_specs=[pl.BlockSpec((WA_BAND, D_MODEL), lambda s: (s, 0))],
        out_specs=pl.BlockSpec((D_MODEL, WA_BAND), lambda s: (0, s)),
        out_shape=jax.ShapeDtypeStruct((D_MODEL, WA_COLS), BF16),
        compiler_params=_params(("arbitrary",)),
        name="w_in_cast_a",
    )(w_in_t)
    w_b = pl.pallas_call(
        _win_b_kernel,
        grid=(WB_COLS // WB_BAND,),
        in_specs=[pl.BlockSpec((WB_BAND, D_MODEL), lambda s: (first_b + s, 0)),
                  pl.BlockSpec((WB_BAND, D_MODEL), lambda s: (first_b + s + 1, 0))],
        out_specs=pl.BlockSpec((D_MODEL, WB_BAND), lambda s: (0, s)),
        out_shape=jax.ShapeDtypeStruct((D_MODEL, WB_COLS), BF16),
        compiler_params=_params(("arbitrary",)),
        name="w_in_cast_b",
    )(w_in_t, w_in_t)
    return w_a, w_b


def _inproj_kernel(*refs, per_seq, side_jobs):
    it = iter(refs)
    x_ref, sh_ref, sc_ref = next(it), next(it), next(it)
    p3_ref = next(it) if per_seq else None
    g_ref, b_ref = next(it), next(it)
    wa_ref, wb_ref = next(it), next(it)
    if side_jobs:
        wof_ref, ca_ref, wmod_ref, bmod_ref = (next(it) for _ in range(4))
    o_refs = [next(it) for _ in range(len(IN_WIDTHS))]
    xn_ref = next(it)

    xn = _layer_norm(x_ref[...], g_ref[...], b_ref[...])
    xn_ref[...] = xn
    h = (xn * (1.0 + _mod_rows(sc_ref, p3_ref)) + _mod_rows(sh_ref, p3_ref)).astype(BF16)
    w_ref, off = wa_ref, 0
    for width, o_ref in zip(IN_WIDTHS, o_refs):
        if off == WA_COLS:
            w_ref, off = wb_ref, 0
        o_ref[...] = _dot(h, w_ref[:, off:off + width])
        off += width

    if side_jobs:
        wob_ref, modl_ref = next(it), next(it)
        wob_ref[...] = wof_ref[...].astype(BF16)
        modl_ref[...] = _dot(ca_ref[...], wmod_ref[...].astype(BF16)) + bmod_ref[...]


def _inproj_call(x2d, mod, rows_per_mod, per_seq, ln_g, ln_b, w_ab, tm, side=None):
    t = x2d.shape[0]
    steps = t // tm
    widths = IN_WIDTHS + (D_MODEL,)
    row = lambda i: (i, 0)
    side_jobs = side is not None
    operands = [x2d, mod, mod]
    in_specs = [pl.BlockSpec((tm, D_MODEL), row)] + _mod_specs(tm, rows_per_mod, per_seq, (0, 1))
    if per_seq:
        operands.append(_repeat_matrix3(tm))
        in_specs.append(_resident((tm, _repeat_k(tm))))
    operands += [ln_g, ln_b, *w_ab]
    in_specs += [_resident((1, D_MODEL)), _resident((1, D_MODEL))] + [_resident(w.shape) for w in w_ab]
    out_specs = [pl.BlockSpec((tm, n), row) for n in widths]
    out_shape = [jax.ShapeDtypeStruct((t, n), F32) for n in widths]
    if side_jobs:
        w_out, c_act, w_mod, b_mod = side
        r1 = D_MODEL // steps
        m = c_act.shape[0]
        late = MOD_LATE * D_MODEL
        mband = late // steps
        first = MOD_EARLY * D_MODEL // mband
        operands += [w_out, c_act, w_mod, b_mod]
        in_specs += [pl.BlockSpec((r1, D_MODEL), row), _resident(c_act.shape),
                     pl.BlockSpec((D_MODEL, mband), lambda i: (0, first + i)),
                     pl.BlockSpec((1, mband), lambda i: (0, first + i))]
        out_specs += [pl.BlockSpec((r1, D_MODEL), row), pl.BlockSpec((m, mband), lambda i: (0, i))]
        out_shape += [jax.ShapeDtypeStruct((D_MODEL, D_MODEL), BF16),
                      jax.ShapeDtypeStruct((m, late), F32)]
    return pl.pallas_call(
        functools.partial(_inproj_kernel, per_seq=per_seq, side_jobs=side_jobs),
        grid=(steps,),
        in_specs=in_specs,
        out_specs=out_specs,
        out_shape=out_shape,
        compiler_params=_params(("arbitrary",)),
        name="in_proj",
    )(*operands)


def _ssd_gmlp_block(*, xc, acum, tot, dt, d_row, e3, mask, state_t, wm_ref, bsb, u, v,
                    gm_g, gm_b):
    rows = xc.shape[0]
    xs = xc[:, :SSD_WIDTH]
    bm = xc[:, SSD_WIDTH:SSD_WIDTH + SSD_GROUPS * SSD_STATE]
    cm = xc[:, SSD_WIDTH + SSD_GROUPS * SSD_STATE:]

    acum2 = acum * LOG2E
    acum2_t = acum2.T
    dt_t = dt.T
    w_end = jnp.exp(tot - acum) * dt
    xw = xs * _expand(w_end, e3)
    eacum = jnp.exp(acum)

    xs_b = xs.astype(BF16)
    lane = lax.broadcasted_iota(jnp.int32, (rows, 2 * SSD_HEADDIM), 1)
    low_half = lane < SSD_HEADDIM

    scores = []
    for g in range(SSD_GROUPS):
        cg = cm[:, g * SSD_STATE:(g + 1) * SSD_STATE].astype(BF16)
        bg = bm[:, g * SSD_STATE:(g + 1) * SSD_STATE].astype(BF16)
        scores.append(_dot_nt(cg, bg))

    heads_per_group = SSD_HEADS // SSD_GROUPS
    y_pairs = []
    for pair in range(SSD_HEADS // 2):
        sl = slice(pair * 2 * SSD_HEADDIM, (pair + 1) * 2 * SSD_HEADDIM)
        x_pair = xs_b[:, sl]
        zero = jnp.zeros_like(x_pair)
        if state_t is not None:
            s_pair = state_t[:, sl].astype(BF16)
        acc = None
        for k in range(2):
            h = 2 * pair + k
            g = h // heads_per_group
            seg2 = acum2[:, h:h + 1] - acum2_t[h:h + 1, :]
            m = jnp.where(mask, scores[g] * jnp.exp2(seg2) * dt_t[h:h + 1, :], 0.0)
            keep = low_half if k == 0 else jnp.logical_not(low_half)
            rhs = jnp.where(keep, x_pair, zero)
            lhs = m.astype(BF16)
            if state_t is not None:
                c_sc = cm[:, g * SSD_STATE:(g + 1) * SSD_STATE] * eacum[:, h:h + 1]
                lhs = jnp.concatenate([lhs, c_sc.astype(BF16)], axis=1)
                rhs = jnp.concatenate([rhs, jnp.where(keep, s_pair, zero)], axis=0)
            part = _dot(lhs, rhs)
            acc = part if acc is None else acc + part
        y_pairs.append(acc)
    y = jnp.concatenate(y_pairs, axis=1) + d_row * xs

    ug = _gelu_tanh(u)
    vn = _layer_norm(_gelu_tanh(v), gm_g, gm_b)
    vn_b = vn.astype(BF16)
    mixed = []
    for h in range(GM_HEADS):
        mixed.append(_dot(wm_ref[h], vn_b[:, h * GM_HEAD:(h + 1) * GM_HEAD]))
    y_gm = ug * (jnp.concatenate(mixed, axis=1) + bsb)
    return y, xw, y_gm, vn


def _gated_rmsnorm(y, z, norm_g):
    hg = y * _silu(z)
    parts = []
    for g in range(SSD_GROUPS):
        hh = hg[:, g * GROUP_WIDTH:(g + 1) * GROUP_WIDTH]
        ms = jnp.mean(hh * hh, axis=-1, keepdims=True)
        parts.append(hh * lax.rsqrt(ms + LN_EPS))
    return jnp.concatenate(parts, axis=1) * norm_g


def _conv_silu(taps, conv_w_ref, conv_b):
    acc = conv_b + conv_w_ref[0:1, :] * taps[0]
    for k in range(1, CONV_K):
        acc = acc + conv_w_ref[k:k + 1, :] * taps[k]
    return _silu(acc)


MIX_CHUNKS = 4


def _prompt_mixer_kernel(z_ref, xbc_ref, dt_ref, u_ref, v_ref, conv_w_ref, conv_b_ref, dtb_ref,
                         alog_ref, d_ref, e_ref, tril3_ref, ng_ref, gg_ref, gb_ref, ws_ref, bsb_ref,
                         w1f_ref, w2f_ref,
                         y_ref, ssm_ref, w1b_ref, w2b_ref, halo_ref, st_ref, wm_scr):
    for t in range(N_FF_TILES):
        w1b_ref[t] = w1f_ref[:, t * FF_TILE:(t + 1) * FF_TILE].astype(BF16)
    w2b_ref[...] = w2f_ref[...].astype(BF16)

    c = pl.program_id(1)
    nc = pl.num_programs(1)
    ri = lax.broadcasted_iota(jnp.int32, (CHUNK, CHUNK), 0)
    ci = lax.broadcasted_iota(jnp.int32, (CHUNK, CHUNK), 1)
    causal = ci <= ri

    @pl.when(jnp.logical_and(pl.program_id(0) == 0, c == 0))
    def _():
        for h in range(GM_HEADS):
            wm_scr[h] = jnp.where(causal, ws_ref[h], 0.0).astype(BF16)

    @pl.when(c == 0)
    def _():
        halo_ref[...] = jnp.zeros_like(halo_ref)
        st_ref[...] = jnp.zeros_like(st_ref)

    halo = halo_ref[...]
    state_t = st_ref[...]
    for k in range(MIX_CHUNKS):
        rows = slice(k * CHUNK, (k + 1) * CHUNK)
        xbc = xbc_ref[rows, :]
        xp = jnp.concatenate([halo, xbc], axis=0)
        taps = [pltpu.roll(xp, CONV_K - 1 - j, 0)[SUBLANES:, :] for j in range(CONV_K - 1)]
        taps.append(xbc)
        xc = _conv_silu(taps, conv_w_ref, conv_b_ref[...])
        halo = xbc[CHUNK - SUBLANES:, :]

        dt = _softplus(dt_ref[rows, :] + dtb_ref[...])
        a = dt * (-jnp.exp(alog_ref[...]))
        acum = _dot(tril3_ref[...], jnp.concatenate(_split3(a), axis=0))
        tot = jnp.broadcast_to(acum[CHUNK - 1:CHUNK, :], (CHUNK, DT_PAD))

        y, xw, y_gm, _ = _ssd_gmlp_block(
            xc=xc, acum=acum, tot=tot, dt=dt, d_row=d_ref[...], e3=e_ref[...], mask=causal,
            state_t=state_t, wm_ref=wm_scr, bsb=bsb_ref[...], u=u_ref[rows, :], v=v_ref[rows, :],
            gm_g=gg_ref[...], gm_b=gb_ref[...])

        cd = jnp.exp(_expand(tot[0:SUBLANES, :], e_ref[...]))[0:1, :]
        bm = xc[:, SSD_WIDTH:SSD_WIDTH + SSD_GROUPS * SSD_STATE].astype(BF16)
        xw_b = xw.astype(BF16)
        upd = [_dot_tn(bm[:, g * SSD_STATE:(g + 1) * SSD_STATE],
                       xw_b[:, g * GROUP_WIDTH:(g + 1) * GROUP_WIDTH]) for g in range(SSD_GROUPS)]
        state_t = state_t * cd + jnp.concatenate(upd, axis=1)

        y_ssd = _gated_rmsnorm(y, z_ref[rows, :], ng_ref[...])
        y_ref[rows, :] = jnp.concatenate([y_ssd, y_gm], axis=1).astype(BF16)

    halo_ref[...] = halo
    st_ref[...] = state_t

    @pl.when(c == nc - 1)
    def _():
        ssm_ref[...] = state_t.T


def _prompt_mixer_call(z, xbc, dt, u, v, consts, batch, seq, w_ff1, w_ff2):
    rows = MIX_CHUNKS * CHUNK
    nc = seq // rows
    row = lambda b, c: (b * nc + c, 0)
    t = z.shape[0]
    const_specs = [_resident(a.shape) for a in consts]
    steps = batch * nc
    r1, r2 = D_MODEL // steps, D_FF // steps
    return pl.pallas_call(
        _prompt_mixer_kernel,
        grid=(batch, nc),
        in_specs=[pl.BlockSpec((rows, SSD_WIDTH), row),
                  pl.BlockSpec((rows, CONV_DIM), row),
                  pl.BlockSpec((rows, DT_PAD), row),
                  pl.BlockSpec((rows, GM_WIDTH), row),
                  pl.BlockSpec((rows, GM_WIDTH), row)] + const_specs
        + [pl.BlockSpec((r1, D_FF), row), pl.BlockSpec((r2, D_MODEL), row)],
        out_specs=[pl.BlockSpec((rows, 2 * SSD_WIDTH), row),
                   pl.BlockSpec((None, SSD_WIDTH, SSD_STATE), lambda b, c: (b, 0, 0)),
                   pl.BlockSpec((N_FF_TILES, r1, FF_TILE), lambda b, c: (0, b * nc + c, 0)),
                   pl.BlockSpec((r2, D_MODEL), row)],
        out_shape=[jax.ShapeDtypeStruct((t, 2 * SSD_WIDTH), BF16),
                   jax.ShapeDtypeStruct((batch, SSD_WIDTH, SSD_STATE), F32),
                   jax.ShapeDtypeStruct((N_FF_TILES, D_MODEL, FF_TILE), BF16),
                   jax.ShapeDtypeStruct((D_FF, D_MODEL), BF16)],
        scratch_shapes=[pltpu.VMEM((SUBLANES, CONV_DIM), F32),
                        pltpu.VMEM((SSD_STATE, SSD_WIDTH), F32),
                        pltpu.VMEM((GM_HEADS, CHUNK, CHUNK), BF16)],
        compiler_params=_params(("arbitrary", "arbitrary")),
        name="mixer_prompt",
    )(z, xbc, dt, u, v, *consts, w_ff1, w_ff2)


SAMPLE_BB = CHUNK // DEC_SEQ
SEQ_UNROLL = 16


def _seg_cumsum(a, t):
    k = 1
    while k < DEC_SEQ:
        a = a + jnp.where(t >= k, pltpu.roll(a, k, 0), 0.0)
        k *= 2
    return a


def _seg_last(a, t):
    rows = a.shape[0]
    x = jnp.where(t == DEC_SEQ - 1, a, 0.0)
    k = 1
    while k < DEC_SEQ:
        x = x + pltpu.roll(x, rows - k, 0)
        k *= 2
    return x


def _sample_mixer_kernel(z_ref, xbc_ref, buf_ref, dt_ref, u_ref, v_ref, s_ref, conv_w_ref,
                         conv_b_ref, dtb_ref, alog_ref, d_ref, e_ref, tril3_ref, ng_ref, gg_ref,
                         gb_ref, ws_ref, bsb_ref,
                         y_ref, snew_ref, vn_ref,
                         c_scr, b_scr, xw_scr, aux_scr, yoff_scr, wm_scr):
    del tril3_ref
    rows = CHUNK
    tcol = lax.broadcasted_iota(jnp.int32, (rows, 1), 0) % DEC_SEQ
    ri = lax.broadcasted_iota(jnp.int32, (rows, rows), 0)
    ci = lax.broadcasted_iota(jnp.int32, (rows, rows), 1)
    mask = jnp.logical_and(ci <= ri, (ci // DEC_SEQ) == (ri // DEC_SEQ))

    @pl.when(pl.program_id(0) == 0)
    def _():
        rep_r = (ci == ri % DEC_SEQ).astype(BF16)
        rep_c = (ri == ci % DEC_SEQ).astype(BF16)
        corner = jnp.logical_and(ri < DEC_SEQ, ci < DEC_SEQ)
        for h in range(GM_HEADS):
            w8 = jnp.where(corner, ws_ref[h], 0.0).astype(BF16)
            tiled = _dot(_dot(rep_r, w8).astype(BF16), rep_c)
            wm_scr[h] = jnp.where(mask, tiled, 0.0).astype(BF16)

    xbc = xbc_ref[...]
    buf = buf_ref[...]
    taps = []
    for k in range(CONV_K - 1):
        back = CONV_K - 1 - k
        taps.append(jnp.where(tcol >= back, pltpu.roll(xbc, back, 0),
                              pltpu.roll(buf, rows - DEC_SEQ + back, 0)))
    taps.append(xbc)
    xc = _conv_silu(taps, conv_w_ref, conv_b_ref[...])

    dt = _softplus(dt_ref[...] + dtb_ref[...])
    a = dt * (-jnp.exp(alog_ref[...]))
    acum = _seg_cumsum(a, tcol)
    tot = _seg_last(acum, tcol)

    y, xw, y_gm, vn = _ssd_gmlp_block(
        xc=xc, acum=acum, tot=tot, dt=dt, d_row=d_ref[...], e3=e_ref[...], mask=mask,
        state_t=None, wm_ref=wm_scr, bsb=bsb_ref[...], u=u_ref[...], v=v_ref[...],
        gm_g=gg_ref[...], gm_b=gb_ref[...])
    vn_ref[...] = vn

    tfull = lax.broadcasted_iota(jnp.int32, (rows, SSD_WIDTH), 0) % DEC_SEQ
    dcx = jnp.exp(_expand(tot, e_ref[...]))
    hi, mid, lo = _split3(dcx)
    aux_scr[...] = jnp.where(tfull == 0, hi.astype(F32),
                             jnp.where(tfull == 1, mid.astype(F32),
                                       jnp.where(tfull == 2, lo.astype(F32), 0.0)))
    xw_scr[...] = xw
    b_scr[...] = xc[:, SSD_WIDTH:SSD_WIDTH + SSD_GROUPS * SSD_STATE]
    c_scr[...] = xc[:, SSD_WIDTH + SSD_GROUPS * SSD_STATE:]

    r8 = lax.broadcasted_iota(jnp.int32, (DEC_SEQ, 2 * SSD_STATE), 0)
    l8 = lax.broadcasted_iota(jnp.int32, (DEC_SEQ, 2 * SSD_STATE), 1)
    ones_part = jnp.where(jnp.logical_and(r8 < 3, l8 >= SSD_STATE), 1.0, 0.0)
    zeros_b = jnp.zeros((DEC_SEQ, SSD_STATE), F32)

    def per_seq(b, carry):
        r0 = pl.multiple_of(b * DEC_SEQ, DEC_SEQ)
        rsl = pl.ds(r0, DEC_SEQ)
        for g in range(SSD_GROUPS):
            gs = slice(g * GROUP_WIDTH, (g + 1) * GROUP_WIDTH)
            ns = slice(g * SSD_STATE, (g + 1) * SSD_STATE)
            s_bg = s_ref[b, gs, :]
            yoff_scr[rsl, gs] = _dot_nt(c_scr[rsl, ns].astype(BF16), s_bg.astype(BF16))
            lhs = jnp.concatenate([xw_scr[rsl, gs], aux_scr[rsl, gs]], axis=0)
            rhs = jnp.concatenate(
                [jnp.concatenate([b_scr[rsl, ns], zeros_b], axis=1), ones_part], axis=0)
            res = _dot_tn(lhs.astype(BF16), rhs.astype(BF16))
            snew_ref[b, gs, :] = res[:, SSD_STATE:] * s_bg + res[:, :SSD_STATE]
        return carry

    lax.fori_loop(0, SAMPLE_BB, per_seq, 0, unroll=SEQ_UNROLL)

    y = y + yoff_scr[...] * jnp.exp(_expand(acum, e_ref[...]))
    y_ssd = _gated_rmsnorm(y, z_ref[...], ng_ref[...])
    y_ref[...] = jnp.concatenate([y_ssd, y_gm], axis=1).astype(BF16)


def _sample_mixer_call(z, xbc, buf8, dt, u, v, state, consts):
    t = z.shape[0]
    nb = t // CHUNK
    row = lambda i: (i, 0)
    st_spec = pl.BlockSpec((SAMPLE_BB, SSD_WIDTH, SSD_STATE), lambda i: (i, 0, 0))
    const_specs = [_resident(a.shape) for a in consts]
    return pl.pallas_call(
        _sample_mixer_kernel,
        grid=(nb,),
        in_specs=[pl.BlockSpec((CHUNK, SSD_WIDTH), row),
                  pl.BlockSpec((CHUNK, CONV_DIM), row),
                  pl.BlockSpec((CHUNK, CONV_DIM), row),
                  pl.BlockSpec((CHUNK, DT_PAD), row),
                  pl.BlockSpec((CHUNK, GM_WIDTH), row),
                  pl.BlockSpec((CHUNK, GM_WIDTH), row),
                  st_spec] + const_specs,
        out_specs=[pl.BlockSpec((CHUNK, 2 * SSD_WIDTH), row), st_spec,
                   pl.BlockSpec((CHUNK, GM_WIDTH), row)],
        out_shape=[jax.ShapeDtypeStruct((t, 2 * SSD_WIDTH), BF16),
                   jax.ShapeDtypeStruct(state.shape, F32),
                   jax.ShapeDtypeStruct((t, GM_WIDTH), F32)],
        scratch_shapes=[pltpu.VMEM((CHUNK, SSD_GROUPS * SSD_STATE), F32),
                        pltpu.VMEM((CHUNK, SSD_GROUPS * SSD_STATE), F32),
                        pltpu.VMEM((CHUNK, SSD_WIDTH), F32),
                        pltpu.VMEM((CHUNK, SSD_WIDTH), F32),
                        pltpu.VMEM((CHUNK, SSD_WIDTH), F32),
                        pltpu.VMEM((GM_HEADS, CHUNK, CHUNK), BF16)],
        compiler_params=_params(("arbitrary",)),
        name="mixer_sample",
    )(z, xbc, buf8, dt, u, v, state, *consts)


def _outln_kernel(*refs, per_seq):
    it = iter(refs)
    xn_ref, y_ref, g_ref = next(it), next(it), next(it)
    p3_ref = next(it) if per_seq else None
    w_ref, lg_ref, lb_ref, o_ref = (next(it) for _ in range(4))
    mix = _dot(y_ref[...], w_ref[...])
    o_ref[...] = _layer_norm(ALPHA * xn_ref[...] + (1.0 + _mod_rows(g_ref, p3_ref)) * mix,
                             lg_ref[...], lb_ref[...])


def _outln_call(xn, ymix, mod, rows_per_mod, per_seq, w_out, ln_g, ln_b, tm):
    t = xn.shape[0]
    row = lambda i: (i, 0)
    operands = [xn, ymix, mod]
    in_specs = ([pl.BlockSpec((tm, D_MODEL), row), pl.BlockSpec((tm, D_MODEL), row)]
                + _mod_specs(tm, rows_per_mod, per_seq, (0,)))
    if per_seq:
        operands.append(_repeat_matrix3(tm))
        in_specs.append(_resident((tm, _repeat_k(tm))))
    operands += [w_out, ln_g, ln_b]
    in_specs += [_resident(w_out.shape), _resident((1, D_MODEL)), _resident((1, D_MODEL))]
    return pl.pallas_call(
        functools.partial(_outln_kernel, per_seq=per_seq),
        grid=(t // tm,),
        in_specs=in_specs,
        out_specs=pl.BlockSpec((tm, D_MODEL), row),
        out_shape=jax.ShapeDtypeStruct((t, D_MODEL), F32),
        compiler_params=_params(("arbitrary",)),
        name="out_ln",
    )(*operands)


def _ffn_kernel(*refs, per_seq):
    it = iter(refs)
    x_ref, sh_ref, sc_ref, g_ref = (next(it) for _ in range(4))
    p3_ref = next(it) if per_seq else None
    w1_ref, w2_ref, lg_ref, lb_ref, o_ref, h_scr = (next(it) for _ in range(6))
    j = pl.program_id(1)
    nj = pl.num_programs(1)

    @pl.when(j == 0)
    def _():
        h_scr[...] = (x_ref[...] * (1.0 + _mod_rows(sc_ref, p3_ref))
                      + _mod_rows(sh_ref, p3_ref)).astype(BF16)
        o_ref[...] = jnp.zeros_like(o_ref)

    a = jnp.maximum(_dot(h_scr[...], w1_ref[...]), 0.0)
    o_ref[...] += _dot((a * a).astype(BF16), w2_ref[...])

    @pl.when(j == nj - 1)
    def _():
        o_ref[...] = _layer_norm(ALPHA * x_ref[...] + (1.0 + _mod_rows(g_ref, p3_ref)) * o_ref[...],
                                 lg_ref[...], lb_ref[...])


def _ffn_call(x1, mod, rows_per_mod, per_seq, w1_tiles, w2, ln_g, ln_b, tm):
    t = x1.shape[0]
    row = lambda i, j: (i, 0)
    operands = [x1, mod, mod, mod]
    in_specs = [pl.BlockSpec((tm, D_MODEL), row)] + _mod_specs(tm, rows_per_mod, per_seq, (1, 2, 3))
    if per_seq:
        operands.append(_repeat_matrix3(tm))
        in_specs.append(_resident((tm, _repeat_k(tm))))
    operands += [w1_tiles, w2, ln_g, ln_b]
    in_specs += [pl.BlockSpec((None, D_MODEL, FF_TILE), lambda i, j: (j, 0, 0)),
                 pl.BlockSpec((FF_TILE, D_MODEL), lambda i, j: (j, 0)),
                 _resident((1, D_MODEL)), _resident((1, D_MODEL))]
    return pl.pallas_call(
        functools.partial(_ffn_kernel, per_seq=per_seq),
        grid=(t // tm, N_FF_TILES),
        in_specs=in_specs,
        out_specs=pl.BlockSpec((tm, D_MODEL), row),
        out_shape=jax.ShapeDtypeStruct((t, D_MODEL), F32),
        scratch_shapes=[pltpu.VMEM((tm, D_MODEL), BF16)],
        compiler_params=_params(("arbitrary", "arbitrary")),
        name="ffn",
    )(*operands)


def kernel(x_prompt, x_sample, state_ssm, state_conv, c_prompt, c_sample, ln_in_g, ln_in_b, w_mod, b_mod, w_in, conv_w, conv_b, dt_bias, a_log, d_skip, ssd_norm_g, gm_ln_g, gm_ln_b, gm_w_s, gm_b_s, w_out, ln_mix_g, ln_mix_b, w_ff1, w_ff2, ln_ffn_g, ln_ffn_b):
    depth = w_mod.shape[0]
    assert depth == 1
    bp, seq, _ = x_prompt.shape
    bs, dec, _ = x_sample.shape
    assert dec == DEC_SEQ and seq % CHUNK == 0 and (bs * dec) % CHUNK == 0

    r1 = lambda a: a.reshape(1, -1)
    ln_in_g2, ln_in_b2 = r1(ln_in_g), r1(ln_in_b)
    l = 0

    w_ab = _win_call(jnp.swapaxes(w_in[l], 0, 1))

    head_of_chan = jnp.arange(SSD_WIDTH, dtype=jnp.int32) // SSD_HEADDIM
    e_sel = (jnp.arange(DT_PAD, dtype=jnp.int32)[:, None] == head_of_chan[None, :]).astype(BF16)
    e3 = jnp.concatenate([e_sel, e_sel, e_sel], axis=0)
    tril = jnp.tril(jnp.ones((CHUNK, CHUNK), BF16))
    tril3 = jnp.concatenate([tril, tril, tril], axis=1)
    d_row = r1(jnp.repeat(d_skip[l], SSD_HEADDIM))
    dtb = _pad_cols(r1(dt_bias[l]), DT_PAD)
    alog = _pad_cols(r1(a_log[l]), DT_PAD)
    mixer_consts = [conv_w[l], r1(conv_b[l]), dtb, alog, d_row, e3, tril3, r1(ssd_norm_g[l]),
                    r1(gm_ln_g[l]), r1(gm_ln_b[l])]
    bsb_p = jnp.repeat(gm_b_s[l].T, GM_HEAD, axis=1)
    reps = CHUNK // DEC_SEQ
    bsb_s = jnp.tile(jnp.repeat(gm_b_s[l][:, :DEC_SEQ].T, GM_HEAD, axis=1), (reps, 1))

    n_c = bp + bs
    c_all = jnp.concatenate([c_prompt, c_sample], axis=0)
    c_all = jnp.pad(c_all, ((0, (-n_c) % SUBLANES), (0, 0)))
    mod_e, c_act = _mod_call(c_all, w_mod[l], r1(b_mod[l]))
    mod_ep = mod_e[:bp].reshape(bp, 1, MOD_EARLY * D_MODEL)
    mod_es = mod_e[bp:n_c]

    xp2 = x_prompt.reshape(bp * seq, D_MODEL)
    z, xbc, dtr, u, v, xn_p, w_out_b, mod_l = _inproj_call(
        xp2, mod_ep, seq, False, ln_in_g2, ln_in_b2, w_ab, tm=256,
        side=(w_out[l], c_act, w_mod[l], r1(b_mod[l])))
    mod_lp = mod_l[:bp].reshape(bp, 1, MOD_LATE * D_MODEL)
    mod_ls = mod_l[bp:n_c]
    ymix, ssm_p, w1_t, w2_b = _prompt_mixer_call(
        z, xbc, dtr, u, v, mixer_consts + [gm_w_s[l], bsb_p], bp, seq, w_ff1[l], w_ff2[l])
    x1 = _outln_call(xn_p, ymix, mod_lp, seq, False, w_out_b,
                     r1(ln_mix_g[l]), r1(ln_mix_b[l]), tm=512)
    yp = _ffn_call(x1, mod_lp, seq, False, w1_t, w2_b, r1(ln_ffn_g[l]), r1(ln_ffn_b[l]), tm=512)
    conv_p = xbc.reshape(bp, seq, CONV_DIM)[:, seq - (CONV_K - 1):, :]

    xs2 = x_sample.reshape(bs * dec, D_MODEL)
    zs, xbcs, dtrs, us, vs, xn_s = _inproj_call(xs2, mod_es, None, True, ln_in_g2, ln_in_b2, w_ab,
                                                tm=256)
    buf8 = jnp.pad(state_conv[l], ((0, 0), (DEC_SEQ - (CONV_K - 1), 0), (0, 0)))
    buf8 = buf8.reshape(bs * dec, CONV_DIM)
    st_in = state_ssm[l].reshape(bs, SSD_WIDTH, SSD_STATE)
    ymix_s, ssm_s, vn_s = _sample_mixer_call(zs, xbcs, buf8, dtrs, us, vs, st_in,
                                             mixer_consts + [gm_w_s[l], bsb_s])
    x1s = _outln_call(xn_s, ymix_s, mod_ls, None, True, w_out_b,
                      r1(ln_mix_g[l]), r1(ln_mix_b[l]), tm=512)
    ys = _ffn_call(x1s, mod_ls, None, True, w1_t, w2_b, r1(ln_ffn_g[l]), r1(ln_ffn_b[l]), tm=512)
    conv_s = xbcs.reshape(bs, dec, CONV_DIM)[:, dec - (CONV_K - 1):, :]

    return (yp.reshape(bp, seq, D_MODEL),
            ys.reshape(bs, dec, D_MODEL),
            ssm_p.reshape(1, bp, SSD_HEADS, SSD_HEADDIM, SSD_STATE),
            conv_p[None],
            ssm_s.reshape(1, bs, SSD_HEADS, SSD_HEADDIM, SSD_STATE),
            conv_s[None],
            vn_s.reshape(1, bs, dec, GM_WIDTH))
```

```python
import functools
import math

import jax
import jax.numpy as jnp
from jax import lax
from jax.experimental import pallas as pl
from jax.experimental.pallas import tpu as pltpu

D_MODEL = 2048
SSD_WIDTH = 1024
SSD_HEADDIM = 64
SSD_HEADS = 16
SSD_GROUPS = 2
SSD_STATE = 128
GROUP_WIDTH = SSD_WIDTH // SSD_GROUPS
CONV_K = 4
CONV_DIM = SSD_WIDTH + 2 * SSD_GROUPS * SSD_STATE
GM_WIDTH = 1024
GM_HEAD = 128
GM_HEADS = 8
D_FF = 4 * D_MODEL
FF_TILE = 1024
N_FF_TILES = D_FF // FF_TILE
CHUNK = 128
DEC_SEQ = 8
DT_PAD = 128
SUBLANES = 8
LANES = 128
ALPHA = 2.0 ** 0.25
LN_EPS = 1e-5
LOG2E = math.log2(math.e)

V7X_VMEM_BYTES = 64 * 1024 * 1024
VMEM_LIMIT = V7X_VMEM_BYTES - 8 * 1024 * 1024

F32 = jnp.float32
BF16 = jnp.bfloat16


def _layer_norm(x, g, b):
    mu = jnp.mean(x, axis=-1, keepdims=True)
    xc = x - mu
    var = jnp.mean(xc * xc, axis=-1, keepdims=True)
    return xc * lax.rsqrt(var + LN_EPS) * g + b


def _silu(x):
    h = 0.5 * x
    return h + h * jnp.tanh(h)


def _gelu_tanh(x):
    c = math.sqrt(2.0 / math.pi)
    h = 0.5 * x
    return h + h * jnp.tanh(x * (c + (c * 0.044715) * (x * x)))


def _softplus(x):
    return jnp.maximum(x, 0.0) + jnp.log1p(jnp.exp(-jnp.abs(x)))


def _dot(a, b):
    return jnp.dot(a, b, preferred_element_type=F32)


def _dot_nt(a, b):
    return lax.dot_general(a, b, (((1,), (1,)), ((), ())), preferred_element_type=F32)


def _dot_tn(a, b):
    return lax.dot_general(a, b, (((0,), (0,)), ((), ())), preferred_element_type=F32)


def _split3(x):
    hi = x.astype(BF16)
    r1 = x - hi.astype(F32)
    mid = r1.astype(BF16)
    lo = (r1 - mid.astype(F32)).astype(BF16)
    return hi, mid, lo


def _expand(x, sel3):
    return _dot(jnp.concatenate(_split3(x), axis=1), sel3)


def _mod_rows(ref, p3_ref):
    if p3_ref is None:
        return ref[...]
    parts = list(_split3(ref[...]))
    pad = p3_ref.shape[1] - 3 * ref.shape[0]
    if pad:
        parts.append(jnp.zeros((pad, ref.shape[1]), BF16))
    return _dot(p3_ref[...], jnp.concatenate(parts, axis=0))


def _resident(shape):
    nd = len(shape)
    return pl.BlockSpec(shape, lambda *_: (0,) * nd, pipeline_mode=pl.Buffered(1))


def _mod_specs(tm, rows_per_mod, per_seq, pieces):
    if per_seq:
        return [pl.BlockSpec((tm // DEC_SEQ, D_MODEL), lambda i, *_, p=p: (i, p)) for p in pieces]
    tiles_per_mod = rows_per_mod // tm
    return [pl.BlockSpec((None, 1, D_MODEL), lambda i, *_, p=p: (i // tiles_per_mod, 0, p))
            for p in pieces]


def _repeat_matrix3(tm):
    nb = tm // DEC_SEQ
    sel = (jnp.arange(tm, dtype=jnp.int32)[:, None] // DEC_SEQ
           == jnp.arange(nb, dtype=jnp.int32)[None, :]).astype(BF16)
    return _pad_cols(jnp.concatenate([sel, sel, sel], axis=1), _repeat_k(tm))


def _repeat_k(tm):
    return -(-(3 * tm // DEC_SEQ) // LANES) * LANES


def _pad_cols(a, n):
    return jnp.pad(a, ((0, 0), (0, n - a.shape[1])))


def _params(sem):
    return pltpu.CompilerParams(dimension_semantics=sem, vmem_limit_bytes=VMEM_LIMIT)


MOD_EARLY = 2
MOD_LATE = 4


def _mod_kernel(c_ref, w_ref, b_ref, o_ref, a_ref):
    a = _silu(c_ref[...]).astype(BF16)
    a_ref[...] = a
    o_ref[...] = _dot(a, w_ref[...].astype(BF16)) + b_ref[...]


def _mod_call(c_all, w_mod, b_mod):
    m = c_all.shape[0]
    tn = 1024
    return pl.pallas_call(
        _mod_kernel,
        grid=(MOD_EARLY * D_MODEL // tn,),
        in_specs=[
            pl.BlockSpec((m, D_MODEL), lambda j: (0, 0)),
            pl.BlockSpec((D_MODEL, tn), lambda j: (0, j)),
            pl.BlockSpec((1, tn), lambda j: (0, j)),
        ],
        out_specs=[pl.BlockSpec((m, tn), lambda j: (0, j)),
                   pl.BlockSpec((m, D_MODEL), lambda j: (0, 0))],
        out_shape=[jax.ShapeDtypeStruct((m, MOD_EARLY * D_MODEL), F32),
                   jax.ShapeDtypeStruct((m, D_MODEL), BF16)],
        compiler_params=_params(("arbitrary",)),
        name="mod",
    )(c_all, w_mod, b_mod)


IN_WIDTHS = (SSD_WIDTH, CONV_DIM, DT_PAD, GM_WIDTH, GM_WIDTH)
WA_COLS = SSD_WIDTH + CONV_DIM + DT_PAD
WB_COLS = 2 * GM_WIDTH
DT_ROW = SSD_WIDTH + CONV_DIM
UV_ROW = DT_ROW + SSD_HEADS
WA_BAND = 384
WB_BAND = 512
assert WA_COLS % WA_BAND == 0 and WB_COLS % WB_BAND == 0 and DT_ROW % WB_BAND == 0


def _win_a_kernel(a_ref, o_ref):
    blk = a_ref[...]
    row = lax.broadcasted_iota(jnp.int32, blk.shape, 0) + pl.program_id(0) * WA_BAND
    o_ref[...] = jnp.where(row < UV_ROW, blk, 0.0).T.astype(BF16)


def _win_b_kernel(a_ref, b_ref, o_ref):
    blk = jnp.concatenate([a_ref[SSD_HEADS:, :], b_ref[0:SSD_HEADS, :]], axis=0)
    o_ref[...] = blk.T.astype(BF16)


def _win_call(w_in_t):
    first_b = DT_ROW // WB_BAND
    w_a = pl.pallas_call(
        _win_a_kernel,
        grid=(WA_COLS // WA_BAND,),
        in_specs=[pl.BlockSpec((WA_BAND, D_MODEL), lambda s: (s, 0))],
        out_specs=pl.BlockSpec((D_MODEL, WA_BAND), lambda s: (0, s)),
        out_shape=jax.ShapeDtypeStruct((D_MODEL, WA_COLS), BF16),
        compiler_params=_params(("arbitrary",)),
        name="w_in_cast_a",
    )(w_in_t)
    w_b = pl.pallas_call(
        _win_b_kernel,
        grid=(WB_COLS // WB_BAND,),
        in_specs=[pl.BlockSpec((WB_BAND, D_MODEL), lambda s: (first_b + s, 0)),
                  pl.BlockSpec((WB_BAND, D_MODEL), lambda s: (first_b + s + 1, 0))],
        out_specs=pl.BlockSpec((D_MODEL, WB_BAND), lambda s: (0, s)),
        out_shape=jax.ShapeDtypeStruct((D_MODEL, WB_COLS), BF16),
        compiler_params=_params(("arbitrary",)),
        name="w_in_cast_b",
    )(w_in_t, w_in_t)
    return w_a, w_b


def _inproj_kernel(*refs, per_seq, side_jobs):
    it = iter(refs)
    x_ref, sh_ref, sc_ref = next(it), next(it), next(it)
    p3_ref = next(it) if per_seq else None
    g_ref, b_ref, dtb_ref = next(it), next(it), next(it)
    wa_ref, wb_ref = next(it), next(it)
    if side_jobs:
        wof_ref, ca_ref, wmod_ref, bmod_ref = (next(it) for _ in range(4))
    o_refs = [next(it) for _ in range(len(IN_WIDTHS))]
    xn_ref = next(it)

    xn = _layer_norm(x_ref[...], g_ref[...], b_ref[...])
    xn_ref[...] = xn
    h = (xn * (1.0 + _mod_rows(sc_ref, p3_ref)) + _mod_rows(sh_ref, p3_ref)).astype(BF16)
    acts = (_silu, None, lambda r: _softplus(r + dtb_ref[...]), _gelu_tanh, _gelu_tanh)
    w_ref, off = wa_ref, 0
    for width, o_ref, act in zip(IN_WIDTHS, o_refs, acts):
        if off == WA_COLS:
            w_ref, off = wb_ref, 0
        r = _dot(h, w_ref[:, off:off + width])
        o_ref[...] = r if act is None else act(r)
        off += width

    if side_jobs:
        wob_ref, modl_ref = next(it), next(it)
        wob_ref[...] = wof_ref[...].astype(BF16)
        modl_ref[...] = _dot(ca_ref[...], wmod_ref[...].astype(BF16)) + bmod_ref[...]


def _inproj_call(x2d, mod, rows_per_mod, per_seq, ln_g, ln_b, dtb, w_ab, tm, side=None):
    t = x2d.shape[0]
    steps = t // tm
    widths = IN_WIDTHS + (D_MODEL,)
    row = lambda i: (i, 0)
    side_jobs = side is not None
    operands = [x2d, mod, mod]
    in_specs = [pl.BlockSpec((tm, D_MODEL), row)] + _mod_specs(tm, rows_per_mod, per_seq, (0, 1))
    if per_seq:
        operands.append(_repeat_matrix3(tm))
        in_specs.append(_resident((tm, _repeat_k(tm))))
    operands += [ln_g, ln_b, dtb, *w_ab]
    in_specs += [_resident((1, D_MODEL)), _resident((1, D_MODEL)), _resident(dtb.shape)]
    in_specs += [_resident(w.shape) for w in w_ab]
    out_specs = [pl.BlockSpec((tm, n), row) for n in widths]
    out_shape = [jax.ShapeDtypeStruct((t, n), F32) for n in widths]
    if side_jobs:
        w_out, c_act, w_mod, b_mod = side
        r1 = D_MODEL // steps
        m = c_act.shape[0]
        late = MOD_LATE * D_MODEL
        mband = late // steps
        first = MOD_EARLY * D_MODEL // mband
        operands += [w_out, c_act, w_mod, b_mod]
        in_specs += [pl.BlockSpec((r1, D_MODEL), row), _resident(c_act.shape),
                     pl.BlockSpec((D_MODEL, mband), lambda i: (0, first + i)),
                     pl.BlockSpec((1, mband), lambda i: (0, first + i))]
        out_specs += [pl.BlockSpec((r1, D_MODEL), row), pl.BlockSpec((m, mband), lambda i: (0, i))]
        out_shape += [jax.ShapeDtypeStruct((D_MODEL, D_MODEL), BF16),
                      jax.ShapeDtypeStruct((m, late), F32)]
    return pl.pallas_call(
        functools.partial(_inproj_kernel, per_seq=per_seq, side_jobs=side_jobs),
        grid=(steps,),
        in_specs=in_specs,
        out_specs=out_specs,
        out_shape=out_shape,
        compiler_params=_params(("arbitrary",)),
        name="in_proj",
    )(*operands)


def _ssd_gmlp_block(*, xc, acum, tot, dt, d_row, e3, mask, state_t, wm_ref, bsb, u, v,
                    gm_g, gm_b):
    rows = xc.shape[0]
    xs = xc[:, :SSD_WIDTH]
    bm = xc[:, SSD_WIDTH:SSD_WIDTH + SSD_GROUPS * SSD_STATE]
    cm = xc[:, SSD_WIDTH + SSD_GROUPS * SSD_STATE:]

    acum2 = acum * LOG2E
    acum2_t = acum2.T
    dt_t = dt.T
    w_end = jnp.exp(tot - acum) * dt
    xw = xs * _expand(w_end, e3)
    eacum = jnp.exp(acum)

    xs_b = xs.astype(BF16)
    lane = lax.broadcasted_iota(jnp.int32, (rows, 2 * SSD_HEADDIM), 1)
    low_half = lane < SSD_HEADDIM

    scores = []
    for g in range(SSD_GROUPS):
        cg = cm[:, g * SSD_STATE:(g + 1) * SSD_STATE].astype(BF16)
        bg = bm[:, g * SSD_STATE:(g + 1) * SSD_STATE].astype(BF16)
        scores.append(_dot_nt(cg, bg))

    heads_per_group = SSD_HEADS // SSD_GROUPS
    y_pairs = []
    for pair in range(SSD_HEADS // 2):
        sl = slice(pair * 2 * SSD_HEADDIM, (pair + 1) * 2 * SSD_HEADDIM)
        x_pair = xs_b[:, sl]
        zero = jnp.zeros_like(x_pair)
        if state_t is not None:
            s_pair = state_t[:, sl].astype(BF16)
        acc = None
        for k in range(2):
            h = 2 * pair + k
            g = h // heads_per_group
            seg2 = acum2[:, h:h + 1] - acum2_t[h:h + 1, :]
            m = jnp.where(mask, scores[g] * jnp.exp2(seg2) * dt_t[h:h + 1, :], 0.0)
            keep = low_half if k == 0 else jnp.logical_not(low_half)
            rhs = jnp.where(keep, x_pair, zero)
            lhs = m.astype(BF16)
            if state_t is not None:
                c_sc = cm[:, g * SSD_STATE:(g + 1) * SSD_STATE] * eacum[:, h:h + 1]
                lhs = jnp.concatenate([lhs, c_sc.astype(BF16)], axis=1)
                rhs = jnp.concatenate([rhs, jnp.where(keep, s_pair, zero)], axis=0)
            part = _dot(lhs, rhs)
            acc = part if acc is None else acc + part
        y_pairs.append(acc)
    y = jnp.concatenate(y_pairs, axis=1) + d_row * xs

    ug = u
    vn = _layer_norm(v, gm_g, gm_b)
    vn_b = vn.astype(BF16)
    mixed = []
    for h in range(GM_HEADS):
        mixed.append(_dot(wm_ref[h], vn_b[:, h * GM_HEAD:(h + 1) * GM_HEAD]))
    y_gm = ug * (jnp.concatenate(mixed, axis=1) + bsb)
    return y, xw, y_gm, vn


def _gated_rmsnorm(y, z_act, norm_g):
    hg = y * z_act
    parts = []
    for g in range(SSD_GROUPS):
        hh = hg[:, g * GROUP_WIDTH:(g + 1) * GROUP_WIDTH]
        ms = jnp.mean(hh * hh, axis=-1, keepdims=True)
        parts.append(hh * lax.rsqrt(ms + LN_EPS))
    return jnp.concatenate(parts, axis=1) * norm_g


def _conv_silu(taps, conv_w_ref, conv_b):
    acc = conv_b + conv_w_ref[0:1, :] * taps[0]
    for k in range(1, CONV_K):
        acc = acc + conv_w_ref[k:k + 1, :] * taps[k]
    return _silu(acc)


MIX_CHUNKS = 4


def _prompt_mixer_kernel(z_ref, xbc_ref, dt_ref, u_ref, v_ref, conv_w_ref, conv_b_ref,
                         alog_ref, d_ref, e_ref, tril3_ref, ng_ref, gg_ref, gb_ref, ws_ref, bsb_ref,
                         w1f_ref, w2f_ref,
                         y_ref, ssm_ref, w1b_ref, w2b_ref, halo_ref, st_ref, wm_scr):
    for t in range(N_FF_TILES):
        w1b_ref[t] = w1f_ref[:, t * FF_TILE:(t + 1) * FF_TILE].astype(BF16)
    w2b_ref[...] = w2f_ref[...].astype(BF16)

    c = pl.program_id(1)
    nc = pl.num_programs(1)
    ri = lax.broadcasted_iota(jnp.int32, (CHUNK, CHUNK), 0)
    ci = lax.broadcasted_iota(jnp.int32, (CHUNK, CHUNK), 1)
    causal = ci <= ri

    @pl.when(jnp.logical_and(pl.program_id(0) == 0, c == 0))
    def _():
        for h in range(GM_HEADS):
            wm_scr[h] = jnp.where(causal, ws_ref[h], 0.0).astype(BF16)

    @pl.when(c == 0)
    def _():
        halo_ref[...] = jnp.zeros_like(halo_ref)
        st_ref[...] = jnp.zeros_like(st_ref)

    halo = halo_ref[...]
    state_t = st_ref[...]
    for k in range(MIX_CHUNKS):
        rows = slice(k * CHUNK, (k + 1) * CHUNK)
        xbc = xbc_ref[rows, :]
        xp = jnp.concatenate([halo, xbc], axis=0)
        taps = [pltpu.roll(xp, CONV_K - 1 - j, 0)[SUBLANES:, :] for j in range(CONV_K - 1)]
        taps.append(xbc)
        xc = _conv_silu(taps, conv_w_ref, conv_b_ref[...])
        halo = xbc[CHUNK - SUBLANES:, :]

        dt = dt_ref[rows, :]
        a = dt * (-jnp.exp(alog_ref[...]))
        acum = _dot(tril3_ref[...], jnp.concatenate(_split3(a), axis=0))
        tot = jnp.broadcast_to(acum[CHUNK - 1:CHUNK, :], (CHUNK, DT_PAD))

        y, xw, y_gm, _ = _ssd_gmlp_block(
            xc=xc, acum=acum, tot=tot, dt=dt, d_row=d_ref[...], e3=e_ref[...], mask=causal,
            state_t=state_t, wm_ref=wm_scr, bsb=bsb_ref[...], u=u_ref[rows, :], v=v_ref[rows, :],
            gm_g=gg_ref[...], gm_b=gb_ref[...])

        cd = jnp.exp(_expand(tot[0:SUBLANES, :], e_ref[...]))[0:1, :]
        bm = xc[:, SSD_WIDTH:SSD_WIDTH + SSD_GROUPS * SSD_STATE].astype(BF16)
        xw_b = xw.astype(BF16)
        upd = [_dot_tn(bm[:, g * SSD_STATE:(g + 1) * SSD_STATE],
                       xw_b[:, g * GROUP_WIDTH:(g + 1) * GROUP_WIDTH]) for g in range(SSD_GROUPS)]
        state_t = state_t * cd + jnp.concatenate(upd, axis=1)

        y_ssd = _gated_rmsnorm(y, z_ref[rows, :], ng_ref[...])
        y_ref[rows, :] = jnp.concatenate([y_ssd, y_gm], axis=1).astype(BF16)

    halo_ref[...] = halo
    st_ref[...] = state_t

    @pl.when(c == nc - 1)
    def _():
        ssm_ref[...] = state_t.T


def _prompt_mixer_call(z, xbc, dt, u, v, consts, batch, seq, w_ff1, w_ff2):
    rows = MIX_CHUNKS * CHUNK
    nc = seq // rows
    row = lambda b, c: (b * nc + c, 0)
    t = z.shape[0]
    const_specs = [_resident(a.shape) for a in consts]
    steps = batch * nc
    r1, r2 = D_MODEL // steps, D_FF // steps
    return pl.pallas_call(
        _prompt_mixer_kernel,
        grid=(batch, nc),
        in_specs=[pl.BlockSpec((rows, SSD_WIDTH), row),
                  pl.BlockSpec((rows, CONV_DIM), row),
                  pl.BlockSpec((rows, DT_PAD), row),
                  pl.BlockSpec((rows, GM_WIDTH), row),
                  pl.BlockSpec((rows, GM_WIDTH), row)] + const_specs
        + [pl.BlockSpec((r1, D_FF), row), pl.BlockSpec((r2, D_MODEL), row)],
        out_specs=[pl.BlockSpec((rows, 2 * SSD_WIDTH), row),
                   pl.BlockSpec((None, SSD_WIDTH, SSD_STATE), lambda b, c: (b, 0, 0)),
                   pl.BlockSpec((N_FF_TILES, r1, FF_TILE), lambda b, c: (0, b * nc + c, 0)),
                   pl.BlockSpec((r2, D_MODEL), row)],
        out_shape=[jax.ShapeDtypeStruct((t, 2 * SSD_WIDTH), BF16),
                   jax.ShapeDtypeStruct((batch, SSD_WIDTH, SSD_STATE), F32),
                   jax.ShapeDtypeStruct((N_FF_TILES, D_MODEL, FF_TILE), BF16),
                   jax.ShapeDtypeStruct((D_FF, D_MODEL), BF16)],
        scratch_shapes=[pltpu.VMEM((SUBLANES, CONV_DIM), F32),
                        pltpu.VMEM((SSD_STATE, SSD_WIDTH), F32),
                        pltpu.VMEM((GM_HEADS, CHUNK, CHUNK), BF16)],
        compiler_params=_params(("arbitrary", "arbitrary")),
        name="mixer_prompt",
    )(z, xbc, dt, u, v, *consts, w_ff1, w_ff2)


SAMPLE_BB = CHUNK // DEC_SEQ
SEQ_UNROLL = 16


def _seg_cumsum(a, t):
    k = 1
    while k < DEC_SEQ:
        a = a + jnp.where(t >= k, pltpu.roll(a, k, 0), 0.0)
        k *= 2
    return a


def _seg_last(a, t):
    rows = a.shape[0]
    x = jnp.where(t == DEC_SEQ - 1, a, 0.0)
    k = 1
    while k < DEC_SEQ:
        x = x + pltpu.roll(x, rows - k, 0)
        k *= 2
    return x


def _sample_mixer_kernel(z_ref, xbc_ref, buf_ref, dt_ref, u_ref, v_ref, s_ref, conv_w_ref,
                         conv_b_ref, alog_ref, d_ref, e_ref, tril3_ref, ng_ref, gg_ref,
                         gb_ref, ws_ref, bsb_ref,
                         y_ref, snew_ref, vn_ref,
                         c_scr, b_scr, xw_scr, aux_scr, yoff_scr, wm_scr):
    del tril3_ref
    rows = CHUNK
    tcol = lax.broadcasted_iota(jnp.int32, (rows, 1), 0) % DEC_SEQ
    ri = lax.broadcasted_iota(jnp.int32, (rows, rows), 0)
    ci = lax.broadcasted_iota(jnp.int32, (rows, rows), 1)
    mask = jnp.logical_and(ci <= ri, (ci // DEC_SEQ) == (ri // DEC_SEQ))

    @pl.when(pl.program_id(0) == 0)
    def _():
        rep_r = (ci == ri % DEC_SEQ).astype(BF16)
        rep_c = (ri == ci % DEC_SEQ).astype(BF16)
        corner = jnp.logical_and(ri < DEC_SEQ, ci < DEC_SEQ)
        for h in range(GM_HEADS):
            w8 = jnp.where(corner, ws_ref[h], 0.0).astype(BF16)
            tiled = _dot(_dot(rep_r, w8).astype(BF16), rep_c)
            wm_scr[h] = jnp.where(mask, tiled, 0.0).astype(BF16)

    xbc = xbc_ref[...]
    buf = buf_ref[...]
    taps = []
    for k in range(CONV_K - 1):
        back = CONV_K - 1 - k
        taps.append(jnp.where(tcol >= back, pltpu.roll(xbc, back, 0),
                              pltpu.roll(buf, rows - DEC_SEQ + back, 0)))
    taps.append(xbc)
    xc = _conv_silu(taps, conv_w_ref, conv_b_ref[...])

    dt = dt_ref[...]
    a = dt * (-jnp.exp(alog_ref[...]))
    acum = _seg_cumsum(a, tcol)
    tot = _seg_last(acum, tcol)

    y, xw, y_gm, vn = _ssd_gmlp_block(
        xc=xc, acum=acum, tot=tot, dt=dt, d_row=d_ref[...], e3=e_ref[...], mask=mask,
        state_t=None, wm_ref=wm_scr, bsb=bsb_ref[...], u=u_ref[...], v=v_ref[...],
        gm_g=gg_ref[...], gm_b=gb_ref[...])
    vn_ref[...] = vn

    tfull = lax.broadcasted_iota(jnp.int32, (rows, SSD_WIDTH), 0) % DEC_SEQ
    dcx = jnp.exp(_expand(tot, e_ref[...]))
    hi, mid, lo = _split3(dcx)
    aux_scr[...] = jnp.where(tfull == 0, hi.astype(F32),
                             jnp.where(tfull == 1, mid.astype(F32),
                                       jnp.where(tfull == 2, lo.astype(F32), 0.0)))
    xw_scr[...] = xw
    b_scr[...] = xc[:, SSD_WIDTH:SSD_WIDTH + SSD_GROUPS * SSD_STATE]
    c_scr[...] = xc[:, SSD_WIDTH + SSD_GROUPS * SSD_STATE:]

    r8 = lax.broadcasted_iota(jnp.int32, (DEC_SEQ, 2 * SSD_STATE), 0)
    l8 = lax.broadcasted_iota(jnp.int32, (DEC_SEQ, 2 * SSD_STATE), 1)
    ones_part = jnp.where(jnp.logical_and(r8 < 3, l8 >= SSD_STATE), 1.0, 0.0)
    zeros_b = jnp.zeros((DEC_SEQ, SSD_STATE), F32)

    def per_seq(b, carry):
        r0 = pl.multiple_of(b * DEC_SEQ, DEC_SEQ)
        rsl = pl.ds(r0, DEC_SEQ)
        for g in range(SSD_GROUPS):
            gs = slice(g * GROUP_WIDTH, (g + 1) * GROUP_WIDTH)
            ns = slice(g * SSD_STATE, (g + 1) * SSD_STATE)
            s_bg = s_ref[b, gs, :]
            yoff_scr[rsl, gs] = _dot_nt(c_scr[rsl, ns].astype(BF16), s_bg.astype(BF16))
            lhs = jnp.concatenate([xw_scr[rsl, gs], aux_scr[rsl, gs]], axis=0)
            rhs = jnp.concatenate(
                [jnp.concatenate([b_scr[rsl, ns], zeros_b], axis=1), ones_part], axis=0)
            res = _dot_tn(lhs.astype(BF16), rhs.astype(BF16))
            snew_ref[b, gs, :] = res[:, SSD_STATE:] * s_bg + res[:, :SSD_STATE]
        return carry

    lax.fori_loop(0, SAMPLE_BB, per_seq, 0, unroll=SEQ_UNROLL)

    y = y + yoff_scr[...] * jnp.exp(_expand(acum, e_ref[...]))
    y_ssd = _gated_rmsnorm(y, z_ref[...], ng_ref[...])
    y_ref[...] = jnp.concatenate([y_ssd, y_gm], axis=1).astype(BF16)


def _sample_mixer_call(z, xbc, buf8, dt, u, v, state, consts):
    t = z.shape[0]
    nb = t // CHUNK
    row = lambda i: (i, 0)
    st_spec = pl.BlockSpec((SAMPLE_BB, SSD_WIDTH, SSD_STATE), lambda i: (i, 0, 0))
    const_specs = [_resident(a.shape) for a in consts]
    return pl.pallas_call(
        _sample_mixer_kernel,
        grid=(nb,),
        in_specs=[pl.BlockSpec((CHUNK, SSD_WIDTH), row),
                  pl.BlockSpec((CHUNK, CONV_DIM), row),
                  pl.BlockSpec((CHUNK, CONV_DIM), row),
                  pl.BlockSpec((CHUNK, DT_PAD), row),
                  pl.BlockSpec((CHUNK, GM_WIDTH), row),
                  pl.BlockSpec((CHUNK, GM_WIDTH), row),
                  st_spec] + const_specs,
        out_specs=[pl.BlockSpec((CHUNK, 2 * SSD_WIDTH), row), st_spec,
                   pl.BlockSpec((CHUNK, GM_WIDTH), row)],
        out_shape=[jax.ShapeDtypeStruct((t, 2 * SSD_WIDTH), BF16),
                   jax.ShapeDtypeStruct(state.shape, F32),
                   jax.ShapeDtypeStruct((t, GM_WIDTH), F32)],
        scratch_shapes=[pltpu.VMEM((CHUNK, SSD_GROUPS * SSD_STATE), F32),
                        pltpu.VMEM((CHUNK, SSD_GROUPS * SSD_STATE), F32),
                        pltpu.VMEM((CHUNK, SSD_WIDTH), F32),
                        pltpu.VMEM((CHUNK, SSD_WIDTH), F32),
                        pltpu.VMEM((CHUNK, SSD_WIDTH), F32),
                        pltpu.VMEM((GM_HEADS, CHUNK, CHUNK), BF16)],
        compiler_params=_params(("arbitrary",)),
        name="mixer_sample",
    )(z, xbc, buf8, dt, u, v, state, *consts)


def _outln_kernel(*refs, per_seq):
    it = iter(refs)
    xn_ref, y_ref, g_ref = next(it), next(it), next(it)
    p3_ref = next(it) if per_seq else None
    w_ref, lg_ref, lb_ref, o_ref = (next(it) for _ in range(4))
    mix = _dot(y_ref[...], w_ref[...])
    o_ref[...] = _layer_norm(ALPHA * xn_ref[...] + (1.0 + _mod_rows(g_ref, p3_ref)) * mix,
                             lg_ref[...], lb_ref[...])


def _outln_call(xn, ymix, mod, rows_per_mod, per_seq, w_out, ln_g, ln_b, tm):
    t = xn.shape[0]
    row = lambda i: (i, 0)
    operands = [xn, ymix, mod]
    in_specs = ([pl.BlockSpec((tm, D_MODEL), row), pl.BlockSpec((tm, D_MODEL), row)]
                + _mod_specs(tm, rows_per_mod, per_seq, (0,)))
    if per_seq:
        operands.append(_repeat_matrix3(tm))
        in_specs.append(_resident((tm, _repeat_k(tm))))
    operands += [w_out, ln_g, ln_b]
    in_specs += [_resident(w_out.shape), _resident((1, D_MODEL)), _resident((1, D_MODEL))]
    return pl.pallas_call(
        functools.partial(_outln_kernel, per_seq=per_seq),
        grid=(t // tm,),
        in_specs=in_specs,
        out_specs=pl.BlockSpec((tm, D_MODEL), row),
        out_shape=jax.ShapeDtypeStruct((t, D_MODEL), F32),
        compiler_params=_params(("arbitrary",)),
        name="out_ln",
    )(*operands)


def _ffn_kernel(*refs, per_seq):
    it = iter(refs)
    x_ref, sh_ref, sc_ref, g_ref = (next(it) for _ in range(4))
    p3_ref = next(it) if per_seq else None
    w1_ref, w2_ref, lg_ref, lb_ref, o_ref = (next(it) for _ in range(5))
    h_scr = next(it) if per_seq else None
    j = pl.program_id(1)
    nj = pl.num_programs(1)

    @pl.when(j == 0)
    def _():
        if per_seq:
            h_scr[...] = (x_ref[...] * (1.0 + _mod_rows(sc_ref, p3_ref))
                          + _mod_rows(sh_ref, p3_ref)).astype(BF16)
        o_ref[...] = jnp.zeros_like(o_ref)

    if per_seq:
        h = h_scr[...]
    else:
        h = (x_ref[...] * (1.0 + sc_ref[...]) + sh_ref[...]).astype(BF16)
    a = jnp.maximum(_dot(h, w1_ref[...]), 0.0)
    o_ref[...] += _dot((a * a).astype(BF16), w2_ref[...])

    @pl.when(j == nj - 1)
    def _():
        o_ref[...] = _layer_norm(ALPHA * x_ref[...] + (1.0 + _mod_rows(g_ref, p3_ref)) * o_ref[...],
                                 lg_ref[...], lb_ref[...])


def _ffn_call(x1, mod, rows_per_mod, per_seq, w1_tiles, w2, ln_g, ln_b, tm):
    t = x1.shape[0]
    row = lambda i, j: (i, 0)
    operands = [x1, mod, mod, mod]
    in_specs = [pl.BlockSpec((tm, D_MODEL), row)] + _mod_specs(tm, rows_per_mod, per_seq, (1, 2, 3))
    if per_seq:
        operands.append(_repeat_matrix3(tm))
        in_specs.append(_resident((tm, _repeat_k(tm))))
    operands += [w1_tiles, w2, ln_g, ln_b]
    in_specs += [pl.BlockSpec((None, D_MODEL, FF_TILE), lambda i, j: (j, 0, 0)),
                 pl.BlockSpec((FF_TILE, D_MODEL), lambda i, j: (j, 0)),
                 _resident((1, D_MODEL)), _resident((1, D_MODEL))]
    return pl.pallas_call(
        functools.partial(_ffn_kernel, per_seq=per_seq),
        grid=(t // tm, N_FF_TILES),
        in_specs=in_specs,
        out_specs=pl.BlockSpec((tm, D_MODEL), row),
        out_shape=jax.ShapeDtypeStruct((t, D_MODEL), F32),
        scratch_shapes=[pltpu.VMEM((tm, D_MODEL), BF16)] if per_seq else [],
        compiler_params=_params(("arbitrary", "arbitrary")),
        name="ffn",
    )(*operands)


def kernel(x_prompt, x_sample, state_ssm, state_conv, c_prompt, c_sample, ln_in_g, ln_in_b, w_mod, b_mod, w_in, conv_w, conv_b, dt_bias, a_log, d_skip, ssd_norm_g, gm_ln_g, gm_ln_b, gm_w_s, gm_b_s, w_out, ln_mix_g, ln_mix_b, w_ff1, w_ff2, ln_ffn_g, ln_ffn_b):
    depth = w_mod.shape[0]
    assert depth == 1
    bp, seq, _ = x_prompt.shape
    bs, dec, _ = x_sample.shape
    assert dec == DEC_SEQ and seq % CHUNK == 0 and (bs * dec) % CHUNK == 0

    r1 = lambda a: a.reshape(1, -1)
    ln_in_g2, ln_in_b2 = r1(ln_in_g), r1(ln_in_b)
    l = 0

    w_ab = _win_call(jnp.swapaxes(w_in[l], 0, 1))

    head_of_chan = jnp.arange(SSD_WIDTH, dtype=jnp.int32) // SSD_HEADDIM
    e_sel = (jnp.arange(DT_PAD, dtype=jnp.int32)[:, None] == head_of_chan[None, :]).astype(BF16)
    e3 = jnp.concatenate([e_sel, e_sel, e_sel], axis=0)
    tril = jnp.tril(jnp.ones((CHUNK, CHUNK), BF16))
    tril3 = jnp.concatenate([tril, tril, tril], axis=1)
    d_row = r1(jnp.repeat(d_skip[l], SSD_HEADDIM))
    dtb = _pad_cols(r1(dt_bias[l]), DT_PAD)
    alog = _pad_cols(r1(a_log[l]), DT_PAD)
    mixer_consts = [conv_w[l], r1(conv_b[l]), alog, d_row, e3, tril3, r1(ssd_norm_g[l]),
                    r1(gm_ln_g[l]), r1(gm_ln_b[l])]
    bsb_p = jnp.repeat(gm_b_s[l].T, GM_HEAD, axis=1)
    reps = CHUNK // DEC_SEQ
    bsb_s = jnp.tile(jnp.repeat(gm_b_s[l][:, :DEC_SEQ].T, GM_HEAD, axis=1), (reps, 1))

    n_c = bp + bs
    c_all = jnp.concatenate([c_prompt, c_sample], axis=0)
    c_all = jnp.pad(c_all, ((0, (-n_c) % SUBLANES), (0, 0)))
    mod_e, c_act = _mod_call(c_all, w_mod[l], r1(b_mod[l]))
    mod_ep = mod_e[:bp].reshape(bp, 1, MOD_EARLY * D_MODEL)
    mod_es = mod_e[bp:n_c]

    xp2 = x_prompt.reshape(bp * seq, D_MODEL)
    z, xbc, dtr, u, v, xn_p, w_out_b, mod_l = _inproj_call(
        xp2, mod_ep, seq, False, ln_in_g2, ln_in_b2, dtb, w_ab, tm=256,
        side=(w_out[l], c_act, w_mod[l], r1(b_mod[l])))
    mod_lp = mod_l[:bp].reshape(bp, 1, MOD_LATE * D_MODEL)
    mod_ls = mod_l[bp:n_c]
    ymix, ssm_p, w1_t, w2_b = _prompt_mixer_call(
        z, xbc, dtr, u, v, mixer_consts + [gm_w_s[l], bsb_p], bp, seq, w_ff1[l], w_ff2[l])
    x1 = _outln_call(xn_p, ymix, mod_lp, seq, False, w_out_b,
                     r1(ln_mix_g[l]), r1(ln_mix_b[l]), tm=512)
    yp = _ffn_call(x1, mod_lp, seq, False, w1_t, w2_b, r1(ln_ffn_g[l]), r1(ln_ffn_b[l]), tm=512)
    conv_p = xbc.reshape(bp, seq, CONV_DIM)[:, seq - (CONV_K - 1):, :]

    xs2 = x_sample.reshape(bs * dec, D_MODEL)
    zs, xbcs, dtrs, us, vs, xn_s = _inproj_call(xs2, mod_es, None, True, ln_in_g2, ln_in_b2, dtb,
                                                w_ab, tm=256)
    buf8 = jnp.pad(state_conv[l], ((0, 0), (DEC_SEQ - (CONV_K - 1), 0), (0, 0)))
    buf8 = buf8.reshape(bs * dec, CONV_DIM)
    st_in = state_ssm[l].reshape(bs, SSD_WIDTH, SSD_STATE)
    ymix_s, ssm_s, vn_s = _sample_mixer_call(zs, xbcs, buf8, dtrs, us, vs, st_in,
                                             mixer_consts + [gm_w_s[l], bsb_s])
    x1s = _outln_call(xn_s, ymix_s, mod_ls, None, True, w_out_b,
                      r1(ln_mix_g[l]), r1(ln_mix_b[l]), tm=512)
    ys = _ffn_call(x1s, mod_ls, None, True, w1_t, w2_b, r1(ln_ffn_g[l]), r1(ln_ffn_b[l]), tm=512)
    conv_s = xbcs.reshape(bs, dec, CONV_DIM)[:, dec - (CONV_K - 1):, :]

    return (yp.reshape(bp, seq, D_MODEL),
            ys.reshape(bs, dec, D_MODEL),
            ssm_p.reshape(1, bp, SSD_HEADS, SSD_HEADDIM, SSD_STATE),
            conv_p[None],
            ssm_s.reshape(1, bs, SSD_HEADS, SSD_HEADDIM, SSD_STATE),
            conv_s[None],
            vn_s.reshape(1, bs, dec, GM_WIDTH))
```

```python
import functools
import math

import jax
import jax.numpy as jnp
from jax import lax
from jax.experimental import pallas as pl
from jax.experimental.pallas import tpu as pltpu

D_MODEL = 2048
SSD_WIDTH = 1024
SSD_HEADDIM = 64
SSD_HEADS = 16
SSD_GROUPS = 2
SSD_STATE = 128
GROUP_WIDTH = SSD_WIDTH // SSD_GROUPS
CONV_K = 4
CONV_DIM = SSD_WIDTH + 2 * SSD_GROUPS * SSD_STATE
GM_WIDTH = 1024
GM_HEAD = 128
GM_HEADS = 8
D_FF = 4 * D_MODEL
FF_TILE = 1024
N_FF_TILES = D_FF // FF_TILE
CHUNK = 128
DEC_SEQ = 8
DT_PAD = 128
SUBLANES = 8
LANES = 128
ALPHA = 2.0 ** 0.25
LN_EPS = 1e-5
LOG2E = math.log2(math.e)

V7X_VMEM_BYTES = 64 * 1024 * 1024
VMEM_LIMIT = V7X_VMEM_BYTES - 8 * 1024 * 1024

F32 = jnp.float32
BF16 = jnp.bfloat16


def _layer_norm(x, g, b):
    mu = jnp.mean(x, axis=-1, keepdims=True)
    xc = x - mu
    var = jnp.mean(xc * xc, axis=-1, keepdims=True)
    return xc * lax.rsqrt(var + LN_EPS) * g + b


def _silu(x):
    h = 0.5 * x
    return h + h * jnp.tanh(h)


def _gelu_tanh(x):
    c = math.sqrt(2.0 / math.pi)
    h = 0.5 * x
    return h + h * jnp.tanh(x * (c + (c * 0.044715) * (x * x)))


def _softplus(x):
    return jnp.maximum(x, 0.0) + jnp.log1p(jnp.exp(-jnp.abs(x)))


def _dot(a, b):
    return jnp.dot(a, b, preferred_element_type=F32)


def _dot_nt(a, b):
    return lax.dot_general(a, b, (((1,), (1,)), ((), ())), preferred_element_type=F32)


def _dot_tn(a, b):
    return lax.dot_general(a, b, (((0,), (0,)), ((), ())), preferred_element_type=F32)


def _split3(x):
    hi = x.astype(BF16)
    r1 = x - hi.astype(F32)
    mid = r1.astype(BF16)
    lo = (r1 - mid.astype(F32)).astype(BF16)
    return hi, mid, lo


def _expand(x, sel3):
    return _dot(jnp.concatenate(_split3(x), axis=1), sel3)


def _mod_rows(ref, p3_ref):
    if p3_ref is None:
        return ref[...]
    parts = list(_split3(ref[...]))
    pad = p3_ref.shape[1] - 3 * ref.shape[0]
    if pad:
        parts.append(jnp.zeros((pad, ref.shape[1]), BF16))
    return _dot(p3_ref[...], jnp.concatenate(parts, axis=0))


def _resident(shape):
    nd = len(shape)
    return pl.BlockSpec(shape, lambda *_: (0,) * nd, pipeline_mode=pl.Buffered(1))


def _mod_specs(tm, rows_per_mod, per_seq, pieces):
    if per_seq:
        return [pl.BlockSpec((tm // DEC_SEQ, D_MODEL), lambda i, *_, p=p: (i, p)) for p in pieces]
    tiles_per_mod = rows_per_mod // tm
    return [pl.BlockSpec((None, 1, D_MODEL), lambda i, *_, p=p: (i // tiles_per_mod, 0, p))
            for p in pieces]


def _repeat_matrix3(tm):
    nb = tm // DEC_SEQ
    sel = (jnp.arange(tm, dtype=jnp.int32)[:, None] // DEC_SEQ
           == jnp.arange(nb, dtype=jnp.int32)[None, :]).astype(BF16)
    return _pad_cols(jnp.concatenate([sel, sel, sel], axis=1), _repeat_k(tm))


def _repeat_k(tm):
    return -(-(3 * tm // DEC_SEQ) // LANES) * LANES


def _pad_cols(a, n):
    return jnp.pad(a, ((0, 0), (0, n - a.shape[1])))


def _params(sem):
    return pltpu.CompilerParams(dimension_semantics=sem, vmem_limit_bytes=VMEM_LIMIT)


MOD_EARLY = 2
MOD_LATE = 4


def _mod_kernel(c_ref, w_ref, b_ref, o_ref, a_ref):
    a = _silu(c_ref[...]).astype(BF16)
    a_ref[...] = a
    o_ref[...] = _dot(a, w_ref[...].astype(BF16)) + b_ref[...]


def _mod_call(c_all, w_mod, b_mod):
    m = c_all.shape[0]
    tn = 1024
    return pl.pallas_call(
        _mod_kernel,
        grid=(MOD_EARLY * D_MODEL // tn,),
        in_specs=[
            pl.BlockSpec((m, D_MODEL), lambda j: (0, 0)),
            pl.BlockSpec((D_MODEL, tn), lambda j: (0, j)),
            pl.BlockSpec((1, tn), lambda j: (0, j)),
        ],
        out_specs=[pl.BlockSpec((m, tn), lambda j: (0, j)),
                   pl.BlockSpec((m, D_MODEL), lambda j: (0, 0))],
        out_shape=[jax.ShapeDtypeStruct((m, MOD_EARLY * D_MODEL), F32),
                   jax.ShapeDtypeStruct((m, D_MODEL), BF16)],
        compiler_params=_params(("arbitrary",)),
        name="mod",
    )(c_all, w_mod, b_mod)


IN_WIDTHS = (SSD_WIDTH, CONV_DIM, DT_PAD, GM_WIDTH, GM_WIDTH)
WA_COLS = SSD_WIDTH + CONV_DIM + DT_PAD
WB_COLS = 2 * GM_WIDTH
DT_ROW = SSD_WIDTH + CONV_DIM
UV_ROW = DT_ROW + SSD_HEADS
WA_BAND = 384
WB_BAND = 512
assert WA_COLS % WA_BAND == 0 and WB_COLS % WB_BAND == 0 and DT_ROW % WB_BAND == 0


def _win_a_kernel(a_ref, o_ref):
    blk = a_ref[...]
    row = lax.broadcasted_iota(jnp.int32, blk.shape, 0) + pl.program_id(0) * WA_BAND
    o_ref[...] = jnp.where(row < UV_ROW, blk, 0.0).T.astype(BF16)


def _win_b_kernel(a_ref, b_ref, o_ref):
    blk = jnp.concatenate([a_ref[SSD_HEADS:, :], b_ref[0:SSD_HEADS, :]], axis=0)
    o_ref[...] = blk.T.astype(BF16)


def _win_call(w_in_t):
    first_b = DT_ROW // WB_BAND
    w_a = pl.pallas_call(
        _win_a_kernel,
        grid=(WA_COLS // WA_BAND,),
        in_specs=[pl.BlockSpec((WA_BAND, D_MODEL), lambda s: (s, 0))],
        out_specs=pl.BlockSpec((D_MODEL, WA_BAND), lambda s: (0, s)),
        out_shape=jax.ShapeDtypeStruct((D_MODEL, WA_COLS), BF16),
        compiler_params=_params(("arbitrary",)),
        name="w_in_cast_a",
    )(w_in_t)
    w_b = pl.pallas_call(
        _win_b_kernel,
        grid=(WB_COLS // WB_BAND,),
        in_specs=[pl.BlockSpec((WB_BAND, D_MODEL), lambda s: (first_b + s, 0)),
                  pl.BlockSpec((WB_BAND, D_MODEL), lambda s: (first_b + s + 1, 0))],
        out_specs=pl.BlockSpec((D_MODEL, WB_BAND), lambda s: (0, s)),
        out_shape=jax.ShapeDtypeStruct((D_MODEL, WB_COLS), BF16),
        compiler_params=_params(("arbitrary",)),
        name="w_in_cast_b",
    )(w_in_t, w_in_t)
    return w_a, w_b


def _inproj_kernel(*refs, per_seq, side_jobs):
    it = iter(refs)
    x_ref, sh_ref, sc_ref = next(it), next(it), next(it)
    p3_ref = next(it) if per_seq else None
    g_ref, b_ref, dtb_ref, gg_ref, gb_ref = (next(it) for _ in range(5))
    wa_ref, wb_ref = next(it), next(it)
    if side_jobs:
        wof_ref, ca_ref, wmod_ref, bmod_ref = (next(it) for _ in range(4))
    o_refs = [next(it) for _ in range(len(IN_WIDTHS))]
    xn_ref = next(it)

    xn = _layer_norm(x_ref[...], g_ref[...], b_ref[...])
    xn_ref[...] = xn
    h = (xn * (1.0 + _mod_rows(sc_ref, p3_ref)) + _mod_rows(sh_ref, p3_ref)).astype(BF16)
    acts = (_silu, None, lambda r: _softplus(r + dtb_ref[...]), _gelu_tanh,
            lambda r: _layer_norm(_gelu_tanh(r), gg_ref[...], gb_ref[...]))
    pieces = []
    w_ref, off = wa_ref, 0
    for width, o_ref, act in zip(IN_WIDTHS, o_refs, acts):
        if off == WA_COLS:
            w_ref, off = wb_ref, 0
        pieces.append((w_ref, off, width, o_ref, act))
        off += width
    for w_ref, off, width, o_ref, act in [pieces[-1]] + pieces[:-1]:
        r = _dot(h, w_ref[:, off:off + width])
        o_ref[...] = r if act is None else act(r)

    if side_jobs:
        wob_ref, modl_ref = next(it), next(it)
        wob_ref[...] = wof_ref[...].astype(BF16)
        modl_ref[...] = _dot(ca_ref[...], wmod_ref[...].astype(BF16)) + bmod_ref[...]


def _inproj_call(x2d, mod, rows_per_mod, per_seq, ln_g, ln_b, dtb, gm_g, gm_b, w_ab, tm, side=None):
    t = x2d.shape[0]
    steps = t // tm
    widths = IN_WIDTHS + (D_MODEL,)
    row = lambda i: (i, 0)
    side_jobs = side is not None
    operands = [x2d, mod, mod]
    in_specs = [pl.BlockSpec((tm, D_MODEL), row)] + _mod_specs(tm, rows_per_mod, per_seq, (0, 1))
    if per_seq:
        operands.append(_repeat_matrix3(tm))
        in_specs.append(_resident((tm, _repeat_k(tm))))
    operands += [ln_g, ln_b, dtb, gm_g, gm_b, *w_ab]
    in_specs += [_resident((1, D_MODEL)), _resident((1, D_MODEL)), _resident(dtb.shape),
                 _resident(gm_g.shape), _resident(gm_b.shape)]
    in_specs += [_resident(w.shape) for w in w_ab]
    out_specs = [pl.BlockSpec((tm, n), row) for n in widths]
    out_shape = [jax.ShapeDtypeStruct((t, n), F32) for n in widths]
    if side_jobs:
        w_out, c_act, w_mod, b_mod = side
        r1 = D_MODEL // steps
        m = c_act.shape[0]
        late = MOD_LATE * D_MODEL
        mband = late // steps
        first = MOD_EARLY * D_MODEL // mband
        operands += [w_out, c_act, w_mod, b_mod]
        in_specs += [pl.BlockSpec((r1, D_MODEL), row), _resident(c_act.shape),
                     pl.BlockSpec((D_MODEL, mband), lambda i: (0, first + i)),
                     pl.BlockSpec((1, mband), lambda i: (0, first + i))]
        out_specs += [pl.BlockSpec((r1, D_MODEL), row), pl.BlockSpec((m, mband), lambda i: (0, i))]
        out_shape += [jax.ShapeDtypeStruct((D_MODEL, D_MODEL), BF16),
                      jax.ShapeDtypeStruct((m, late), F32)]
    return pl.pallas_call(
        functools.partial(_inproj_kernel, per_seq=per_seq, side_jobs=side_jobs),
        grid=(steps,),
        in_specs=in_specs,
        out_specs=out_specs,
        out_shape=out_shape,
        compiler_params=_params(("arbitrary",)),
        name="in_proj",
    )(*operands)


def _ssd_gmlp_block(*, xc, acum, tot, dt, d_row, e3, mask, state_t, wm_ref, bsb, u, vn):
    rows = xc.shape[0]
    xs = xc[:, :SSD_WIDTH]
    bm = xc[:, SSD_WIDTH:SSD_WIDTH + SSD_GROUPS * SSD_STATE]
    cm = xc[:, SSD_WIDTH + SSD_GROUPS * SSD_STATE:]

    acum2 = acum * LOG2E
    acum2_t = acum2.T
    dt_t = dt.T
    w_end = jnp.exp(tot - acum) * dt
    xw = xs * _expand(w_end, e3)
    eacum = jnp.exp(acum)

    xs_b = xs.astype(BF16)
    lane = lax.broadcasted_iota(jnp.int32, (rows, 2 * SSD_HEADDIM), 1)
    low_half = lane < SSD_HEADDIM

    scores = []
    for g in range(SSD_GROUPS):
        cg = cm[:, g * SSD_STATE:(g + 1) * SSD_STATE].astype(BF16)
        bg = bm[:, g * SSD_STATE:(g + 1) * SSD_STATE].astype(BF16)
        scores.append(_dot_nt(cg, bg))

    heads_per_group = SSD_HEADS // SSD_GROUPS
    y_pairs = []
    for pair in range(SSD_HEADS // 2):
        sl = slice(pair * 2 * SSD_HEADDIM, (pair + 1) * 2 * SSD_HEADDIM)
        x_pair = xs_b[:, sl]
        zero = jnp.zeros_like(x_pair)
        if state_t is not None:
            s_pair = state_t[:, sl].astype(BF16)
        acc = None
        for k in range(2):
            h = 2 * pair + k
            g = h // heads_per_group
            seg2 = acum2[:, h:h + 1] - acum2_t[h:h + 1, :]
            m = jnp.where(mask, scores[g] * jnp.exp2(seg2) * dt_t[h:h + 1, :], 0.0)
            keep = low_half if k == 0 else jnp.logical_not(low_half)
            rhs = jnp.where(keep, x_pair, zero)
            lhs = m.astype(BF16)
            if state_t is not None:
                c_sc = cm[:, g * SSD_STATE:(g + 1) * SSD_STATE] * eacum[:, h:h + 1]
                lhs = jnp.concatenate([lhs, c_sc.astype(BF16)], axis=1)
                rhs = jnp.concatenate([rhs, jnp.where(keep, s_pair, zero)], axis=0)
            part = _dot(lhs, rhs)
            acc = part if acc is None else acc + part
        y_pairs.append(acc)
    y = jnp.concatenate(y_pairs, axis=1) + d_row * xs

    vn_b = vn.astype(BF16)
    mixed = []
    for h in range(GM_HEADS):
        mixed.append(_dot(wm_ref[h], vn_b[:, h * GM_HEAD:(h + 1) * GM_HEAD]))
    y_gm = u * (jnp.concatenate(mixed, axis=1) + bsb)
    return y, xw, y_gm


def _gated_rmsnorm(y, z_act, norm_g):
    hg = y * z_act
    parts = []
    for g in range(SSD_GROUPS):
        hh = hg[:, g * GROUP_WIDTH:(g + 1) * GROUP_WIDTH]
        ms = jnp.mean(hh * hh, axis=-1, keepdims=True)
        parts.append(hh * lax.rsqrt(ms + LN_EPS))
    return jnp.concatenate(parts, axis=1) * norm_g


def _conv_silu(taps, conv_w_ref, conv_b):
    acc = conv_b + conv_w_ref[0:1, :] * taps[0]
    for k in range(1, CONV_K):
        acc = acc + conv_w_ref[k:k + 1, :] * taps[k]
    return _silu(acc)


MIX_CHUNKS = 4


def _prompt_mixer_kernel(z_ref, xbc_ref, dt_ref, u_ref, v_ref, conv_w_ref, conv_b_ref,
                         alog_ref, d_ref, e_ref, tril3_ref, ng_ref, ws_ref, bsb_ref,
                         w1f_ref, w2f_ref,
                         y_ref, ssm_ref, w1b_ref, w2b_ref, halo_ref, st_ref, wm_scr):
    for t in range(N_FF_TILES):
        w1b_ref[t] = w1f_ref[:, t * FF_TILE:(t + 1) * FF_TILE].astype(BF16)
    w2b_ref[...] = w2f_ref[...].astype(BF16)

    c = pl.program_id(1)
    nc = pl.num_programs(1)
    ri = lax.broadcasted_iota(jnp.int32, (CHUNK, CHUNK), 0)
    ci = lax.broadcasted_iota(jnp.int32, (CHUNK, CHUNK), 1)
    causal = ci <= ri

    @pl.when(jnp.logical_and(pl.program_id(0) == 0, c == 0))
    def _():
        for h in range(GM_HEADS):
            wm_scr[h] = jnp.where(causal, ws_ref[h], 0.0).astype(BF16)

    @pl.when(c == 0)
    def _():
        halo_ref[...] = jnp.zeros_like(halo_ref)
        st_ref[...] = jnp.zeros_like(st_ref)

    halo = halo_ref[...]
    state_t = st_ref[...]
    for k in range(MIX_CHUNKS):
        rows = slice(k * CHUNK, (k + 1) * CHUNK)
        xbc = xbc_ref[rows, :]
        xp = jnp.concatenate([halo, xbc], axis=0)
        taps = [pltpu.roll(xp, CONV_K - 1 - j, 0)[SUBLANES:, :] for j in range(CONV_K - 1)]
        taps.append(xbc)
        xc = _conv_silu(taps, conv_w_ref, conv_b_ref[...])
        halo = xbc[CHUNK - SUBLANES:, :]

        dt = dt_ref[rows, :]
        a = dt * (-jnp.exp(alog_ref[...]))
        acum = _dot(tril3_ref[...], jnp.concatenate(_split3(a), axis=0))
        tot = jnp.broadcast_to(acum[CHUNK - 1:CHUNK, :], (CHUNK, DT_PAD))

        y, xw, y_gm = _ssd_gmlp_block(
            xc=xc, acum=acum, tot=tot, dt=dt, d_row=d_ref[...], e3=e_ref[...], mask=causal,
            state_t=state_t, wm_ref=wm_scr, bsb=bsb_ref[...], u=u_ref[rows, :], vn=v_ref[rows, :])

        cd = jnp.exp(_expand(tot[0:SUBLANES, :], e_ref[...]))[0:1, :]
        bm = xc[:, SSD_WIDTH:SSD_WIDTH + SSD_GROUPS * SSD_STATE].astype(BF16)
        xw_b = xw.astype(BF16)
        upd = [_dot_tn(bm[:, g * SSD_STATE:(g + 1) * SSD_STATE],
                       xw_b[:, g * GROUP_WIDTH:(g + 1) * GROUP_WIDTH]) for g in range(SSD_GROUPS)]
        state_t = state_t * cd + jnp.concatenate(upd, axis=1)

        y_ssd = _gated_rmsnorm(y, z_ref[rows, :], ng_ref[...])
        y_ref[rows, :] = jnp.concatenate([y_ssd, y_gm], axis=1).astype(BF16)

    halo_ref[...] = halo
    st_ref[...] = state_t

    @pl.when(c == nc - 1)
    def _():
        ssm_ref[...] = state_t.T


def _prompt_mixer_call(z, xbc, dt, u, v, consts, batch, seq, w_ff1, w_ff2):
    rows = MIX_CHUNKS * CHUNK
    nc = seq // rows
    row = lambda b, c: (b * nc + c, 0)
    t = z.shape[0]
    const_specs = [_resident(a.shape) for a in consts]
    steps = batch * nc
    r1, r2 = D_MODEL // steps, D_FF // steps
    return pl.pallas_call(
        _prompt_mixer_kernel,
        grid=(batch, nc),
        in_specs=[pl.BlockSpec((rows, SSD_WIDTH), row),
                  pl.BlockSpec((rows, CONV_DIM), row),
                  pl.BlockSpec((rows, DT_PAD), row),
                  pl.BlockSpec((rows, GM_WIDTH), row),
                  pl.BlockSpec((rows, GM_WIDTH), row)] + const_specs
        + [pl.BlockSpec((r1, D_FF), row), pl.BlockSpec((r2, D_MODEL), row)],
        out_specs=[pl.BlockSpec((rows, 2 * SSD_WIDTH), row),
                   pl.BlockSpec((None, SSD_WIDTH, SSD_STATE), lambda b, c: (b, 0, 0)),
                   pl.BlockSpec((N_FF_TILES, r1, FF_TILE), lambda b, c: (0, b * nc + c, 0)),
                   pl.BlockSpec((r2, D_MODEL), row)],
        out_shape=[jax.ShapeDtypeStruct((t, 2 * SSD_WIDTH), BF16),
                   jax.ShapeDtypeStruct((batch, SSD_WIDTH, SSD_STATE), F32),
                   jax.ShapeDtypeStruct((N_FF_TILES, D_MODEL, FF_TILE), BF16),
                   jax.ShapeDtypeStruct((D_FF, D_MODEL), BF16)],
        scratch_shapes=[pltpu.VMEM((SUBLANES, CONV_DIM), F32),
                        pltpu.VMEM((SSD_STATE, SSD_WIDTH), F32),
                        pltpu.VMEM((GM_HEADS, CHUNK, CHUNK), BF16)],
        compiler_params=_params(("arbitrary", "arbitrary")),
        name="mixer_prompt",
    )(z, xbc, dt, u, v, *consts, w_ff1, w_ff2)


SAMPLE_BB = CHUNK // DEC_SEQ
SEQ_UNROLL = 16


def _seg_cumsum(a, t):
    k = 1
    while k < DEC_SEQ:
        a = a + jnp.where(t >= k, pltpu.roll(a, k, 0), 0.0)
        k *= 2
    return a


def _seg_last(a, t):
    rows = a.shape[0]
    x = jnp.where(t == DEC_SEQ - 1, a, 0.0)
    k = 1
    while k < DEC_SEQ:
        x = x + pltpu.roll(x, rows - k, 0)
        k *= 2
    return x


def _sample_mixer_kernel(z_ref, xbc_ref, buf_ref, dt_ref, u_ref, v_ref, s_ref, conv_w_ref,
                         conv_b_ref, alog_ref, d_ref, e_ref, tril3_ref, ng_ref, ws_ref, bsb_ref,
                         y_ref, snew_ref,
                         c_scr, b_scr, xw_scr, aux_scr, yoff_scr, wm_scr):
    del tril3_ref
    rows = CHUNK
    tcol = lax.broadcasted_iota(jnp.int32, (rows, 1), 0) % DEC_SEQ
    ri = lax.broadcasted_iota(jnp.int32, (rows, rows), 0)
    ci = lax.broadcasted_iota(jnp.int32, (rows, rows), 1)
    mask = jnp.logical_and(ci <= ri, (ci // DEC_SEQ) == (ri // DEC_SEQ))

    @pl.when(pl.program_id(0) == 0)
    def _():
        rep_r = (ci == ri % DEC_SEQ).astype(BF16)
        rep_c = (ri == ci % DEC_SEQ).astype(BF16)
        corner = jnp.logical_and(ri < DEC_SEQ, ci < DEC_SEQ)
        for h in range(GM_HEADS):
            w8 = jnp.where(corner, ws_ref[h], 0.0).astype(BF16)
            tiled = _dot(_dot(rep_r, w8).astype(BF16), rep_c)
            wm_scr[h] = jnp.where(mask, tiled, 0.0).astype(BF16)

    xbc = xbc_ref[...]
    buf = buf_ref[...]
    taps = []
    for k in range(CONV_K - 1):
        back = CONV_K - 1 - k
        taps.append(jnp.where(tcol >= back, pltpu.roll(xbc, back, 0),
                              pltpu.roll(buf, rows - DEC_SEQ + back, 0)))
    taps.append(xbc)
    xc = _conv_silu(taps, conv_w_ref, conv_b_ref[...])

    dt = dt_ref[...]
    a = dt * (-jnp.exp(alog_ref[...]))
    acum = _seg_cumsum(a, tcol)
    tot = _seg_last(acum, tcol)

    y, xw, y_gm = _ssd_gmlp_block(
        xc=xc, acum=acum, tot=tot, dt=dt, d_row=d_ref[...], e3=e_ref[...], mask=mask,
        state_t=None, wm_ref=wm_scr, bsb=bsb_ref[...], u=u_ref[...], vn=v_ref[...])

    tfull = lax.broadcasted_iota(jnp.int32, (rows, SSD_WIDTH), 0) % DEC_SEQ
    dcx = jnp.exp(_expand(tot, e_ref[...]))
    hi, mid, lo = _split3(dcx)
    aux_scr[...] = jnp.where(tfull == 0, hi.astype(F32),
                             jnp.where(tfull == 1, mid.astype(F32),
                                       jnp.where(tfull == 2, lo.astype(F32), 0.0)))
    xw_scr[...] = xw
    b_scr[...] = xc[:, SSD_WIDTH:SSD_WIDTH + SSD_GROUPS * SSD_STATE]
    c_scr[...] = xc[:, SSD_WIDTH + SSD_GROUPS * SSD_STATE:]

    r8 = lax.broadcasted_iota(jnp.int32, (DEC_SEQ, 2 * SSD_STATE), 0)
    l8 = lax.broadcasted_iota(jnp.int32, (DEC_SEQ, 2 * SSD_STATE), 1)
    ones_part = jnp.where(jnp.logical_and(r8 < 3, l8 >= SSD_STATE), 1.0, 0.0)
    zeros_b = jnp.zeros((DEC_SEQ, SSD_STATE), F32)

    def per_seq(b, carry):
        r0 = pl.multiple_of(b * DEC_SEQ, DEC_SEQ)
        rsl = pl.ds(r0, DEC_SEQ)
        for g in range(SSD_GROUPS):
            gs = slice(g * GROUP_WIDTH, (g + 1) * GROUP_WIDTH)
            ns = slice(g * SSD_STATE, (g + 1) * SSD_STATE)
            s_bg = s_ref[b, gs, :]
            yoff_scr[rsl, gs] = _dot_nt(c_scr[rsl, ns].astype(BF16), s_bg.astype(BF16))
            lhs = jnp.concatenate([xw_scr[rsl, gs], aux_scr[rsl, gs]], axis=0)
            rhs = jnp.concatenate(
                [jnp.concatenate([b_scr[rsl, ns], zeros_b], axis=1), ones_part], axis=0)
            res = _dot_tn(lhs.astype(BF16), rhs.astype(BF16))
            snew_ref[b, gs, :] = res[:, SSD_STATE:] * s_bg + res[:, :SSD_STATE]
        return carry

    lax.fori_loop(0, SAMPLE_BB, per_seq, 0, unroll=SEQ_UNROLL)

    y = y + yoff_scr[...] * jnp.exp(_expand(acum, e_ref[...]))
    y_ssd = _gated_rmsnorm(y, z_ref[...], ng_ref[...])
    y_ref[...] = jnp.concatenate([y_ssd, y_gm], axis=1).astype(BF16)


def _sample_mixer_call(z, xbc, buf8, dt, u, v, state, consts):
    t = z.shape[0]
    nb = t // CHUNK
    row = lambda i: (i, 0)
    st_spec = pl.BlockSpec((SAMPLE_BB, SSD_WIDTH, SSD_STATE), lambda i: (i, 0, 0))
    const_specs = [_resident(a.shape) for a in consts]
    return pl.pallas_call(
        _sample_mixer_kernel,
        grid=(nb,),
        in_specs=[pl.BlockSpec((CHUNK, SSD_WIDTH), row),
                  pl.BlockSpec((CHUNK, CONV_DIM), row),
                  pl.BlockSpec((CHUNK, CONV_DIM), row),
                  pl.BlockSpec((CHUNK, DT_PAD), row),
                  pl.BlockSpec((CHUNK, GM_WIDTH), row),
                  pl.BlockSpec((CHUNK, GM_WIDTH), row),
                  st_spec] + const_specs,
        out_specs=[pl.BlockSpec((CHUNK, 2 * SSD_WIDTH), row), st_spec],
        out_shape=[jax.ShapeDtypeStruct((t, 2 * SSD_WIDTH), BF16),
                   jax.ShapeDtypeStruct(state.shape, F32)],
        scratch_shapes=[pltpu.VMEM((CHUNK, SSD_GROUPS * SSD_STATE), F32),
                        pltpu.VMEM((CHUNK, SSD_GROUPS * SSD_STATE), F32),
                        pltpu.VMEM((CHUNK, SSD_WIDTH), F32),
                        pltpu.VMEM((CHUNK, SSD_WIDTH), F32),
                        pltpu.VMEM((CHUNK, SSD_WIDTH), F32),
                        pltpu.VMEM((GM_HEADS, CHUNK, CHUNK), BF16)],
        compiler_params=_params(("arbitrary",)),
        name="mixer_sample",
    )(z, xbc, buf8, dt, u, v, state, *consts)


def _outln_kernel(*refs, per_seq):
    it = iter(refs)
    xn_ref, y_ref, g_ref = next(it), next(it), next(it)
    p3_ref = next(it) if per_seq else None
    w_ref, lg_ref, lb_ref, o_ref = (next(it) for _ in range(4))
    mix = _dot(y_ref[...], w_ref[...])
    o_ref[...] = _layer_norm(ALPHA * xn_ref[...] + (1.0 + _mod_rows(g_ref, p3_ref)) * mix,
                             lg_ref[...], lb_ref[...])


def _outln_call(xn, ymix, mod, rows_per_mod, per_seq, w_out, ln_g, ln_b, tm):
    t = xn.shape[0]
    row = lambda i: (i, 0)
    operands = [xn, ymix, mod]
    in_specs = ([pl.BlockSpec((tm, D_MODEL), row), pl.BlockSpec((tm, D_MODEL), row)]
                + _mod_specs(tm, rows_per_mod, per_seq, (0,)))
    if per_seq:
        operands.append(_repeat_matrix3(tm))
        in_specs.append(_resident((tm, _repeat_k(tm))))
    operands += [w_out, ln_g, ln_b]
    in_specs += [_resident(w_out.shape), _resident((1, D_MODEL)), _resident((1, D_MODEL))]
    return pl.pallas_call(
        functools.partial(_outln_kernel, per_seq=per_seq),
        grid=(t // tm,),
        in_specs=in_specs,
        out_specs=pl.BlockSpec((tm, D_MODEL), row),
        out_shape=jax.ShapeDtypeStruct((t, D_MODEL), F32),
        compiler_params=_params(("arbitrary",)),
        name="out_ln",
    )(*operands)


def _ffn_kernel(*refs, per_seq):
    it = iter(refs)
    x_ref, sh_ref, sc_ref, g_ref = (next(it) for _ in range(4))
    p3_ref = next(it) if per_seq else None
    w1_ref, w2_ref, lg_ref, lb_ref, o_ref = (next(it) for _ in range(5))
    h_scr = next(it) if per_seq else None
    j = pl.program_id(1)
    nj = pl.num_programs(1)

    @pl.when(j == 0)
    def _():
        if per_seq:
            h_scr[...] = (x_ref[...] * (1.0 + _mod_rows(sc_ref, p3_ref))
                          + _mod_rows(sh_ref, p3_ref)).astype(BF16)
        o_ref[...] = jnp.zeros_like(o_ref)

    if per_seq:
        h = h_scr[...]
    else:
        h = (x_ref[...] * (1.0 + sc_ref[...]) + sh_ref[...]).astype(BF16)
    a = jnp.maximum(_dot(h, w1_ref[...]), 0.0)
    o_ref[...] += _dot((a * a).astype(BF16), w2_ref[...])

    @pl.when(j == nj - 1)
    def _():
        o_ref[...] = _layer_norm(ALPHA * x_ref[...] + (1.0 + _mod_rows(g_ref, p3_ref)) * o_ref[...],
                                 lg_ref[...], lb_ref[...])


def _ffn_call(x1, mod, rows_per_mod, per_seq, w1_tiles, w2, ln_g, ln_b, tm):
    t = x1.shape[0]
    row = lambda i, j: (i, 0)
    operands = [x1, mod, mod, mod]
    in_specs = [pl.BlockSpec((tm, D_MODEL), row)] + _mod_specs(tm, rows_per_mod, per_seq, (1, 2, 3))
    if per_seq:
        operands.append(_repeat_matrix3(tm))
        in_specs.append(_resident((tm, _repeat_k(tm))))
    operands += [w1_tiles, w2, ln_g, ln_b]
    in_specs += [pl.BlockSpec((None, D_MODEL, FF_TILE), lambda i, j: (j, 0, 0)),
                 pl.BlockSpec((FF_TILE, D_MODEL), lambda i, j: (j, 0)),
                 _resident((1, D_MODEL)), _resident((1, D_MODEL))]
    return pl.pallas_call(
        functools.partial(_ffn_kernel, per_seq=per_seq),
        grid=(t // tm, N_FF_TILES),
        in_specs=in_specs,
        out_specs=pl.BlockSpec((tm, D_MODEL), row),
        out_shape=jax.ShapeDtypeStruct((t, D_MODEL), F32),
        scratch_shapes=[pltpu.VMEM((tm, D_MODEL), BF16)] if per_seq else [],
        compiler_params=_params(("arbitrary", "arbitrary")),
        name="ffn",
    )(*operands)


def kernel(x_prompt, x_sample, state_ssm, state_conv, c_prompt, c_sample, ln_in_g, ln_in_b, w_mod, b_mod, w_in, conv_w, conv_b, dt_bias, a_log, d_skip, ssd_norm_g, gm_ln_g, gm_ln_b, gm_w_s, gm_b_s, w_out, ln_mix_g, ln_mix_b, w_ff1, w_ff2, ln_ffn_g, ln_ffn_b):
    depth = w_mod.shape[0]
    assert depth == 1
    bp, seq, _ = x_prompt.shape
    bs, dec, _ = x_sample.shape
    assert dec == DEC_SEQ and seq % CHUNK == 0 and (bs * dec) % CHUNK == 0

    r1 = lambda a: a.reshape(1, -1)
    ln_in_g2, ln_in_b2 = r1(ln_in_g), r1(ln_in_b)
    l = 0

    w_ab = _win_call(jnp.swapaxes(w_in[l], 0, 1))

    head_of_chan = jnp.arange(SSD_WIDTH, dtype=jnp.int32) // SSD_HEADDIM
    e_sel = (jnp.arange(DT_PAD, dtype=jnp.int32)[:, None] == head_of_chan[None, :]).astype(BF16)
    e3 = jnp.concatenate([e_sel, e_sel, e_sel], axis=0)
    tril = jnp.tril(jnp.ones((CHUNK, CHUNK), BF16))
    tril3 = jnp.concatenate([tril, tril, tril], axis=1)
    d_row = r1(jnp.repeat(d_skip[l], SSD_HEADDIM))
    dtb = _pad_cols(r1(dt_bias[l]), DT_PAD)
    alog = _pad_cols(r1(a_log[l]), DT_PAD)
    mixer_consts = [conv_w[l], r1(conv_b[l]), alog, d_row, e3, tril3, r1(ssd_norm_g[l])]
    gm_g2, gm_b2 = r1(gm_ln_g[l]), r1(gm_ln_b[l])
    bsb_p = jnp.repeat(gm_b_s[l].T, GM_HEAD, axis=1)
    reps = CHUNK // DEC_SEQ
    bsb_s = jnp.tile(jnp.repeat(gm_b_s[l][:, :DEC_SEQ].T, GM_HEAD, axis=1), (reps, 1))

    n_c = bp + bs
    c_all = jnp.concatenate([c_prompt, c_sample], axis=0)
    c_all = jnp.pad(c_all, ((0, (-n_c) % SUBLANES), (0, 0)))
    mod_e, c_act = _mod_call(c_all, w_mod[l], r1(b_mod[l]))
    mod_ep = mod_e[:bp].reshape(bp, 1, MOD_EARLY * D_MODEL)
    mod_es = mod_e[bp:n_c]

    xp2 = x_prompt.reshape(bp * seq, D_MODEL)
    z, xbc, dtr, u, v, xn_p, w_out_b, mod_l = _inproj_call(
        xp2, mod_ep, seq, False, ln_in_g2, ln_in_b2, dtb, gm_g2, gm_b2, w_ab, tm=256,
        side=(w_out[l], c_act, w_mod[l], r1(b_mod[l])))
    mod_lp = mod_l[:bp].reshape(bp, 1, MOD_LATE * D_MODEL)
    mod_ls = mod_l[bp:n_c]
    ymix, ssm_p, w1_t, w2_b = _prompt_mixer_call(
        z, xbc, dtr, u, v, mixer_consts + [gm_w_s[l], bsb_p], bp, seq, w_ff1[l], w_ff2[l])
    x1 = _outln_call(xn_p, ymix, mod_lp, seq, False, w_out_b,
                     r1(ln_mix_g[l]), r1(ln_mix_b[l]), tm=512)
    yp = _ffn_call(x1, mod_lp, seq, False, w1_t, w2_b, r1(ln_ffn_g[l]), r1(ln_ffn_b[l]), tm=512)
    conv_p = xbc.reshape(bp, seq, CONV_DIM)[:, seq - (CONV_K - 1):, :]

    xs2 = x_sample.reshape(bs * dec, D_MODEL)
    zs, xbcs, dtrs, us, vn_s, xn_s = _inproj_call(xs2, mod_es, None, True, ln_in_g2, ln_in_b2, dtb,
                                                  gm_g2, gm_b2, w_ab, tm=256)
    buf8 = jnp.pad(state_conv[l], ((0, 0), (DEC_SEQ - (CONV_K - 1), 0), (0, 0)))
    buf8 = buf8.reshape(bs * dec, CONV_DIM)
    st_in = state_ssm[l].reshape(bs, SSD_WIDTH, SSD_STATE)
    ymix_s, ssm_s = _sample_mixer_call(zs, xbcs, buf8, dtrs, us, vn_s, st_in,
                                       mixer_consts + [gm_w_s[l], bsb_s])
    x1s = _outln_call(xn_s, ymix_s, mod_ls, None, True, w_out_b,
                      r1(ln_mix_g[l]), r1(ln_mix_b[l]), tm=512)
    ys = _ffn_call(x1s, mod_ls, None, True, w1_t, w2_b, r1(ln_ffn_g[l]), r1(ln_ffn_b[l]), tm=512)
    conv_s = xbcs.reshape(bs, dec, CONV_DIM)[:, dec - (CONV_K - 1):, :]

    return (yp.reshape(bp, seq, D_MODEL),
            ys.reshape(bs, dec, D_MODEL),
            ssm_p.reshape(1, bp, SSD_HEADS, SSD_HEADDIM, SSD_STATE),
            conv_p[None],
            ssm_s.reshape(1, bs, SSD_HEADS, SSD_HEADDIM, SSD_STATE),
            conv_s[None],
            vn_s.reshape(1, bs, dec, GM_WIDTH))
```

```python
import functools
import math

import jax
import jax.numpy as jnp
from jax import lax
from jax.experimental import pallas as pl
from jax.experimental.pallas import tpu as pltpu

D_MODEL = 2048
SSD_WIDTH = 1024
SSD_HEADDIM = 64
SSD_HEADS = 16
SSD_GROUPS = 2
SSD_STATE = 128
GROUP_WIDTH = SSD_WIDTH // SSD_GROUPS
CONV_K = 4
CONV_DIM = SSD_WIDTH + 2 * SSD_GROUPS * SSD_STATE
GM_WIDTH = 1024
GM_HEAD = 128
GM_HEADS = 8
D_FF = 4 * D_MODEL
FF_TILE = 1024
N_FF_TILES = D_FF // FF_TILE
CHUNK = 128
DEC_SEQ = 8
DT_PAD = 128
SUBLANES = 8
LANES = 128
ALPHA = 2.0 ** 0.25
LN_EPS = 1e-5
LOG2E = math.log2(math.e)

V7X_VMEM_BYTES = 64 * 1024 * 1024
VMEM_LIMIT = V7X_VMEM_BYTES - 8 * 1024 * 1024

F32 = jnp.float32
BF16 = jnp.bfloat16


def _layer_norm(x, g, b):
    mu = jnp.mean(x, axis=-1, keepdims=True)
    xc = x - mu
    var = jnp.mean(xc * xc, axis=-1, keepdims=True)
    return xc * lax.rsqrt(var + LN_EPS) * g + b


def _silu(x):
    h = 0.5 * x
    return h + h * jnp.tanh(h)


def _gelu_tanh(x):
    c = math.sqrt(2.0 / math.pi)
    h = 0.5 * x
    return h + h * jnp.tanh(x * (c + (c * 0.044715) * (x * x)))


def _softplus(x):
    return jnp.maximum(x, 0.0) + jnp.log1p(jnp.exp(-jnp.abs(x)))


def _dot(a, b):
    return jnp.dot(a, b, preferred_element_type=F32)


def _dot_nt(a, b):
    return lax.dot_general(a, b, (((1,), (1,)), ((), ())), preferred_element_type=F32)


def _dot_tn(a, b):
    return lax.dot_general(a, b, (((0,), (0,)), ((), ())), preferred_element_type=F32)


def _split3(x):
    hi = x.astype(BF16)
    r1 = x - hi.astype(F32)
    mid = r1.astype(BF16)
    lo = (r1 - mid.astype(F32)).astype(BF16)
    return hi, mid, lo


def _expand(x, sel3):
    return _dot(jnp.concatenate(_split3(x), axis=1), sel3)


def _mod_rows(ref, p3_ref):
    if p3_ref is None:
        return ref[...]
    parts = list(_split3(ref[...]))
    pad = p3_ref.shape[1] - 3 * ref.shape[0]
    if pad:
        parts.append(jnp.zeros((pad, ref.shape[1]), BF16))
    return _dot(p3_ref[...], jnp.concatenate(parts, axis=0))


def _resident(shape):
    nd = len(shape)
    return pl.BlockSpec(shape, lambda *_: (0,) * nd, pipeline_mode=pl.Buffered(1))


def _mod_specs(tm, rows_per_mod, per_seq, pieces):
    if per_seq:
        return [pl.BlockSpec((tm // DEC_SEQ, D_MODEL), lambda i, *_, p=p: (i, p)) for p in pieces]
    tiles_per_mod = rows_per_mod // tm
    return [pl.BlockSpec((None, 1, D_MODEL), lambda i, *_, p=p: (i // tiles_per_mod, 0, p))
            for p in pieces]


def _repeat_matrix3(tm):
    nb = tm // DEC_SEQ
    sel = (jnp.arange(tm, dtype=jnp.int32)[:, None] // DEC_SEQ
           == jnp.arange(nb, dtype=jnp.int32)[None, :]).astype(BF16)
    return _pad_cols(jnp.concatenate([sel, sel, sel], axis=1), _repeat_k(tm))


def _repeat_k(tm):
    return -(-(3 * tm // DEC_SEQ) // LANES) * LANES


def _pad_cols(a, n):
    return jnp.pad(a, ((0, 0), (0, n - a.shape[1])))


def _params(sem):
    return pltpu.CompilerParams(dimension_semantics=sem, vmem_limit_bytes=VMEM_LIMIT)


MOD_EARLY = 2
MOD_LATE = 4


def _mod_kernel(c_ref, w_ref, b_ref, o_ref, a_ref):
    a = _silu(c_ref[...]).astype(BF16)
    a_ref[...] = a
    o_ref[...] = _dot(a, w_ref[...].astype(BF16)) + b_ref[...]


def _mod_call(c_all, w_mod, b_mod):
    m = c_all.shape[0]
    tn = 1024
    return pl.pallas_call(
        _mod_kernel,
        grid=(MOD_EARLY * D_MODEL // tn,),
        in_specs=[
            pl.BlockSpec((m, D_MODEL), lambda j: (0, 0)),
            pl.BlockSpec((D_MODEL, tn), lambda j: (0, j)),
            pl.BlockSpec((1, tn), lambda j: (0, j)),
        ],
        out_specs=[pl.BlockSpec((m, tn), lambda j: (0, j)),
                   pl.BlockSpec((m, D_MODEL), lambda j: (0, 0))],
        out_shape=[jax.ShapeDtypeStruct((m, MOD_EARLY * D_MODEL), F32),
                   jax.ShapeDtypeStruct((m, D_MODEL), BF16)],
        compiler_params=_params(("arbitrary",)),
        name="mod",
    )(c_all, w_mod, b_mod)


IN_WIDTHS = (SSD_WIDTH, CONV_DIM, DT_PAD, GM_WIDTH, GM_WIDTH)
WA_COLS = SSD_WIDTH + CONV_DIM + DT_PAD
WB_COLS = 2 * GM_WIDTH
DT_ROW = SSD_WIDTH + CONV_DIM
UV_ROW = DT_ROW + SSD_HEADS
WA_BAND = 384
WB_BAND = 512
assert WA_COLS % WA_BAND == 0 and WB_COLS % WB_BAND == 0 and DT_ROW % WB_BAND == 0


def _win_a_kernel(a_ref, o_ref):
    blk = a_ref[...]
    row = lax.broadcasted_iota(jnp.int32, blk.shape, 0) + pl.program_id(0) * WA_BAND
    o_ref[...] = jnp.where(row < UV_ROW, blk, 0.0).T.astype(BF16)


def _win_b_kernel(a_ref, b_ref, o_ref):
    blk = jnp.concatenate([a_ref[SSD_HEADS:, :], b_ref[0:SSD_HEADS, :]], axis=0)
    o_ref[...] = blk.T.astype(BF16)


def _win_call(w_in_t):
    first_b = DT_ROW // WB_BAND
    w_a = pl.pallas_call(
        _win_a_kernel,
        grid=(WA_COLS // WA_BAND,),
        in_specs=[pl.BlockSpec((WA_BAND, D_MODEL), lambda s: (s, 0))],
        out_specs=pl.BlockSpec((D_MODEL, WA_BAND), lambda s: (0, s)),
        out_shape=jax.ShapeDtypeStruct((D_MODEL, WA_COLS), BF16),
        compiler_params=_params(("arbitrary",)),
        name="w_in_cast_a",
    )(w_in_t)
    w_b = pl.pallas_call(
        _win_b_kernel,
        grid=(WB_COLS // WB_BAND,),
        in_specs=[pl.BlockSpec((WB_BAND, D_MODEL), lambda s: (first_b + s, 0)),
                  pl.BlockSpec((WB_BAND, D_MODEL), lambda s: (first_b + s + 1, 0))],
        out_specs=pl.BlockSpec((D_MODEL, WB_BAND), lambda s: (0, s)),
        out_shape=jax.ShapeDtypeStruct((D_MODEL, WB_COLS), BF16),
        compiler_params=_params(("arbitrary",)),
        name="w_in_cast_b",
    )(w_in_t, w_in_t)
    return w_a, w_b


def _inproj_kernel(*refs, per_seq, side_jobs):
    it = iter(refs)
    x_ref, sh_ref, sc_ref = next(it), next(it), next(it)
    p3_ref = next(it) if per_seq else None
    g_ref, b_ref, dtb_ref, gg_ref, gb_ref = (next(it) for _ in range(5))
    wa_ref, wb_ref = next(it), next(it)
    if side_jobs:
        wof_ref, ca_ref, wmod_ref, bmod_ref = (next(it) for _ in range(4))
    o_refs = [next(it) for _ in range(len(IN_WIDTHS))]
    xn_ref = next(it)

    xn = _layer_norm(x_ref[...], g_ref[...], b_ref[...])
    xn_ref[...] = xn
    h = (xn * (1.0 + _mod_rows(sc_ref, p3_ref)) + _mod_rows(sh_ref, p3_ref)).astype(BF16)
    acts = (_silu, None, lambda r: _softplus(r + dtb_ref[...]), _gelu_tanh,
            lambda r: _layer_norm(_gelu_tanh(r), gg_ref[...], gb_ref[...]))
    pieces = []
    w_ref, off = wa_ref, 0
    for width, o_ref, act in zip(IN_WIDTHS, o_refs, acts):
        if off == WA_COLS:
            w_ref, off = wb_ref, 0
        pieces.append((w_ref, off, width, o_ref, act))
        off += width
    for w_ref, off, width, o_ref, act in [pieces[-1]] + pieces[:-1]:
        r = _dot(h, w_ref[:, off:off + width])
        o_ref[...] = r if act is None else act(r)

    if side_jobs:
        wob_ref, modl_ref = next(it), next(it)
        wob_ref[...] = wof_ref[...].astype(BF16)
        modl_ref[...] = _dot(ca_ref[...], wmod_ref[...].astype(BF16)) + bmod_ref[...]


def _inproj_call(x2d, mod, rows_per_mod, per_seq, ln_g, ln_b, dtb, gm_g, gm_b, w_ab, tm, side=None):
    t = x2d.shape[0]
    steps = t // tm
    widths = IN_WIDTHS + (D_MODEL,)
    row = lambda i: (i, 0)
    side_jobs = side is not None
    operands = [x2d, mod, mod]
    in_specs = [pl.BlockSpec((tm, D_MODEL), row)] + _mod_specs(tm, rows_per_mod, per_seq, (0, 1))
    if per_seq:
        operands.append(_repeat_matrix3(tm))
        in_specs.append(_resident((tm, _repeat_k(tm))))
    operands += [ln_g, ln_b, dtb, gm_g, gm_b, *w_ab]
    in_specs += [_resident((1, D_MODEL)), _resident((1, D_MODEL)), _resident(dtb.shape),
                 _resident(gm_g.shape), _resident(gm_b.shape)]
    in_specs += [_resident(w.shape) for w in w_ab]
    out_specs = [pl.BlockSpec((tm, n), row) for n in widths]
    out_shape = [jax.ShapeDtypeStruct((t, n), F32) for n in widths]
    if side_jobs:
        w_out, c_act, w_mod, b_mod = side
        r1 = D_MODEL // steps
        m = c_act.shape[0]
        late = MOD_LATE * D_MODEL
        mband = late // steps
        first = MOD_EARLY * D_MODEL // mband
        operands += [w_out, c_act, w_mod, b_mod]
        in_specs += [pl.BlockSpec((r1, D_MODEL), row), _resident(c_act.shape),
                     pl.BlockSpec((D_MODEL, mband), lambda i: (0, first + i)),
                     pl.BlockSpec((1, mband), lambda i: (0, first + i))]
        out_specs += [pl.BlockSpec((r1, D_MODEL), row), pl.BlockSpec((m, mband), lambda i: (0, i))]
        out_shape += [jax.ShapeDtypeStruct((D_MODEL, D_MODEL), BF16),
                      jax.ShapeDtypeStruct((m, late), F32)]
    return pl.pallas_call(
        functools.partial(_inproj_kernel, per_seq=per_seq, side_jobs=side_jobs),
        grid=(steps,),
        in_specs=in_specs,
        out_specs=out_specs,
        out_shape=out_shape,
        compiler_params=_params(("arbitrary",)),
        name="in_proj",
    )(*operands)


def _ssd_gmlp_block(*, xc, acum, tot, dt, d_row, e3, mask, state_t, wm_ref, bsb, u, vn):
    rows = xc.shape[0]
    xs = xc[:, :SSD_WIDTH]
    bm = xc[:, SSD_WIDTH:SSD_WIDTH + SSD_GROUPS * SSD_STATE]
    cm = xc[:, SSD_WIDTH + SSD_GROUPS * SSD_STATE:]

    acum2 = acum * LOG2E
    acum2_t = acum2.T
    dt_t = dt.T
    w_end = jnp.exp(tot - acum) * dt
    xw = xs * _expand(w_end, e3)
    eacum = jnp.exp(acum)

    xs_b = xs.astype(BF16)
    lane = lax.broadcasted_iota(jnp.int32, (rows, 2 * SSD_HEADDIM), 1)
    low_half = lane < SSD_HEADDIM

    scores = []
    for g in range(SSD_GROUPS):
        cg = cm[:, g * SSD_STATE:(g + 1) * SSD_STATE].astype(BF16)
        bg = bm[:, g * SSD_STATE:(g + 1) * SSD_STATE].astype(BF16)
        scores.append(_dot_nt(cg, bg))

    heads_per_group = SSD_HEADS // SSD_GROUPS
    y_pairs = []
    for pair in range(SSD_HEADS // 2):
        sl = slice(pair * 2 * SSD_HEADDIM, (pair + 1) * 2 * SSD_HEADDIM)
        x_pair = xs_b[:, sl]
        zero = jnp.zeros_like(x_pair)
        if state_t is not None:
            s_pair = state_t[:, sl].astype(BF16)
        acc = None
        for k in range(2):
            h = 2 * pair + k
            g = h // heads_per_group
            seg2 = acum2[:, h:h + 1] - acum2_t[h:h + 1, :]
            m = jnp.where(mask, scores[g] * jnp.exp2(seg2) * dt_t[h:h + 1, :], 0.0)
            keep = low_half if k == 0 else jnp.logical_not(low_half)
            rhs = jnp.where(keep, x_pair, zero)
            lhs = m.astype(BF16)
            if state_t is not None:
                c_sc = cm[:, g * SSD_STATE:(g + 1) * SSD_STATE] * eacum[:, h:h + 1]
                lhs = jnp.concatenate([lhs, c_sc.astype(BF16)], axis=1)
                rhs = jnp.concatenate([rhs, jnp.where(keep, s_pair, zero)], axis=0)
            part = _dot(lhs, rhs)
            acc = part if acc is None else acc + part
        y_pairs.append(acc)
    y = jnp.concatenate(y_pairs, axis=1) + d_row * xs

    vn_b = vn.astype(BF16)
    mixed = []
    for h in range(GM_HEADS):
        mixed.append(_dot(wm_ref[h], vn_b[:, h * GM_HEAD:(h + 1) * GM_HEAD]))
    y_gm = u * (jnp.concatenate(mixed, axis=1) + bsb)
    return y, xw, y_gm


def _gated_rmsnorm(y, z_act, norm_g):
    hg = y * z_act
    parts = []
    for g in range(SSD_GROUPS):
        hh = hg[:, g * GROUP_WIDTH:(g + 1) * GROUP_WIDTH]
        ms = jnp.mean(hh * hh, axis=-1, keepdims=True)
        parts.append(hh * lax.rsqrt(ms + LN_EPS))
    return jnp.concatenate(parts, axis=1) * norm_g


def _conv_silu(taps, conv_w_ref, conv_b):
    acc = conv_b + conv_w_ref[0:1, :] * taps[0]
    for k in range(1, CONV_K):
        acc = acc + conv_w_ref[k:k + 1, :] * taps[k]
    return _silu(acc)


MIX_CHUNKS = 4


def _prompt_mixer_kernel(z_ref, xbc_ref, dt_ref, u_ref, v_ref, conv_w_ref, conv_b_ref,
                         alog_ref, d_ref, e_ref, tril3_ref, ng_ref, ws_ref, bsb_ref,
                         w1f_ref, w2f_ref,
                         y_ref, ssm_ref, w1b_ref, w2b_ref, halo_ref, st_ref, wm_scr):
    for t in range(N_FF_TILES):
        w1b_ref[t] = w1f_ref[:, t * FF_TILE:(t + 1) * FF_TILE].astype(BF16)
    w2b_ref[...] = w2f_ref[...].astype(BF16)

    c = pl.program_id(1)
    nc = pl.num_programs(1)
    ri = lax.broadcasted_iota(jnp.int32, (CHUNK, CHUNK), 0)
    ci = lax.broadcasted_iota(jnp.int32, (CHUNK, CHUNK), 1)
    causal = ci <= ri

    @pl.when(jnp.logical_and(pl.program_id(0) == 0, c == 0))
    def _():
        for h in range(GM_HEADS):
            wm_scr[h] = jnp.where(causal, ws_ref[h], 0.0).astype(BF16)

    @pl.when(c == 0)
    def _():
        halo_ref[...] = jnp.zeros_like(halo_ref)
        st_ref[...] = jnp.zeros_like(st_ref)

    halo = halo_ref[...]
    state_t = st_ref[...]
    for k in range(MIX_CHUNKS):
        rows = slice(k * CHUNK, (k + 1) * CHUNK)
        xbc = xbc_ref[rows, :]
        xp = jnp.concatenate([halo, xbc], axis=0)
        taps = [pltpu.roll(xp, CONV_K - 1 - j, 0)[SUBLANES:, :] for j in range(CONV_K - 1)]
        taps.append(xbc)
        xc = _conv_silu(taps, conv_w_ref, conv_b_ref[...])
        halo = xbc[CHUNK - SUBLANES:, :]

        dt = dt_ref[rows, :]
        a = dt * (-jnp.exp(alog_ref[...]))
        acum = _dot(tril3_ref[...], jnp.concatenate(_split3(a), axis=0))
        tot = jnp.broadcast_to(acum[CHUNK - 1:CHUNK, :], (CHUNK, DT_PAD))

        y, xw, y_gm = _ssd_gmlp_block(
            xc=xc, acum=acum, tot=tot, dt=dt, d_row=d_ref[...], e3=e_ref[...], mask=causal,
            state_t=state_t, wm_ref=wm_scr, bsb=bsb_ref[...], u=u_ref[rows, :], vn=v_ref[rows, :])

        cd = jnp.exp(_expand(tot[0:SUBLANES, :], e_ref[...]))[0:1, :]
        bm = xc[:, SSD_WIDTH:SSD_WIDTH + SSD_GROUPS * SSD_STATE].astype(BF16)
        xw_b = xw.astype(BF16)
        upd = [_dot_tn(bm[:, g * SSD_STATE:(g + 1) * SSD_STATE],
                       xw_b[:, g * GROUP_WIDTH:(g + 1) * GROUP_WIDTH]) for g in range(SSD_GROUPS)]
        state_t = state_t * cd + jnp.concatenate(upd, axis=1)

        y_ssd = _gated_rmsnorm(y, z_ref[rows, :], ng_ref[...])
        y_ref[rows, :] = jnp.concatenate([y_ssd, y_gm], axis=1).astype(BF16)

    halo_ref[...] = halo
    st_ref[...] = state_t

    @pl.when(c == nc - 1)
    def _():
        ssm_ref[...] = state_t.T


def _prompt_mixer_call(z, xbc, dt, u, v, consts, batch, seq, w_ff1, w_ff2):
    rows = MIX_CHUNKS * CHUNK
    nc = seq // rows
    row = lambda b, c: (b * nc + c, 0)
    t = z.shape[0]
    const_specs = [_resident(a.shape) for a in consts]
    steps = batch * nc
    r1, r2 = D_MODEL // steps, D_FF // steps
    return pl.pallas_call(
        _prompt_mixer_kernel,
        grid=(batch, nc),
        in_specs=[pl.BlockSpec((rows, SSD_WIDTH), row),
                  pl.BlockSpec((rows, CONV_DIM), row),
                  pl.BlockSpec((rows, DT_PAD), row),
                  pl.BlockSpec((rows, GM_WIDTH), row),
                  pl.BlockSpec((rows, GM_WIDTH), row)] + const_specs
        + [pl.BlockSpec((r1, D_FF), row), pl.BlockSpec((r2, D_MODEL), row)],
        out_specs=[pl.BlockSpec((rows, 2 * SSD_WIDTH), row),
                   pl.BlockSpec((None, SSD_WIDTH, SSD_STATE), lambda b, c: (b, 0, 0)),
                   pl.BlockSpec((N_FF_TILES, r1, FF_TILE), lambda b, c: (0, b * nc + c, 0)),
                   pl.BlockSpec((r2, D_MODEL), row)],
        out_shape=[jax.ShapeDtypeStruct((t, 2 * SSD_WIDTH), BF16),
                   jax.ShapeDtypeStruct((batch, SSD_WIDTH, SSD_STATE), F32),
                   jax.ShapeDtypeStruct((N_FF_TILES, D_MODEL, FF_TILE), BF16),
                   jax.ShapeDtypeStruct((D_FF, D_MODEL), BF16)],
        scratch_shapes=[pltpu.VMEM((SUBLANES, CONV_DIM), F32),
                        pltpu.VMEM((SSD_STATE, SSD_WIDTH), F32),
                        pltpu.VMEM((GM_HEADS, CHUNK, CHUNK), BF16)],
        compiler_params=_params(("arbitrary", "arbitrary")),
        name="mixer_prompt",
    )(z, xbc, dt, u, v, *consts, w_ff1, w_ff2)


SAMPLE_BB = CHUNK // DEC_SEQ
SEQ_UNROLL = 16


def _seg_cumsum(a, t):
    k = 1
    while k < DEC_SEQ:
        a = a + jnp.where(t >= k, pltpu.roll(a, k, 0), 0.0)
        k *= 2
    return a


def _seg_last(a, t):
    rows = a.shape[0]
    x = jnp.where(t == DEC_SEQ - 1, a, 0.0)
    k = 1
    while k < DEC_SEQ:
        x = x + pltpu.roll(x, rows - k, 0)
        k *= 2
    return x


def _sample_mixer_kernel(z_ref, xbc_ref, buf_ref, dt_ref, u_ref, v_ref, s_ref, conv_w_ref,
                         conv_b_ref, alog_ref, d_ref, e_ref, tril3_ref, ng_ref, ws_ref, bsb_ref,
                         y_ref, snew_ref,
                         c_scr, b_scr, xw_scr, aux_scr, yoff_scr, wm_scr):
    del tril3_ref
    rows = CHUNK
    tcol = lax.broadcasted_iota(jnp.int32, (rows, 1), 0) % DEC_SEQ
    ri = lax.broadcasted_iota(jnp.int32, (rows, rows), 0)
    ci = lax.broadcasted_iota(jnp.int32, (rows, rows), 1)
    mask = jnp.logical_and(ci <= ri, (ci // DEC_SEQ) == (ri // DEC_SEQ))

    @pl.when(pl.program_id(0) == 0)
    def _():
        rep_r = (ci == ri % DEC_SEQ).astype(BF16)
        rep_c = (ri == ci % DEC_SEQ).astype(BF16)
        corner = jnp.logical_and(ri < DEC_SEQ, ci < DEC_SEQ)
        for h in range(GM_HEADS):
            w8 = jnp.where(corner, ws_ref[h], 0.0).astype(BF16)
            tiled = _dot(_dot(rep_r, w8).astype(BF16), rep_c)
            wm_scr[h] = jnp.where(mask, tiled, 0.0).astype(BF16)

    xbc = xbc_ref[...]
    buf = buf_ref[...]
    taps = []
    for k in range(CONV_K - 1):
        back = CONV_K - 1 - k
        taps.append(jnp.where(tcol >= back, pltpu.roll(xbc, back, 0),
                              pltpu.roll(buf, rows - DEC_SEQ + back, 0)))
    taps.append(xbc)
    xc = _conv_silu(taps, conv_w_ref, conv_b_ref[...])

    dt = dt_ref[...]
    a = dt * (-jnp.exp(alog_ref[...]))
    acum = _seg_cumsum(a, tcol)
    tot = _seg_last(acum, tcol)

    y, xw, y_gm = _ssd_gmlp_block(
        xc=xc, acum=acum, tot=tot, dt=dt, d_row=d_ref[...], e3=e_ref[...], mask=mask,
        state_t=None, wm_ref=wm_scr, bsb=bsb_ref[...], u=u_ref[...], vn=v_ref[...])

    tfull = lax.broadcasted_iota(jnp.int32, (rows, SSD_WIDTH), 0) % DEC_SEQ
    dcx = jnp.exp(_expand(tot, e_ref[...]))
    hi, mid, lo = _split3(dcx)
    aux_scr[...] = jnp.where(tfull == 0, hi.astype(F32),
                             jnp.where(tfull == 1, mid.astype(F32),
                                       jnp.where(tfull == 2, lo.astype(F32), 0.0)))
    xw_scr[...] = xw
    b_scr[...] = xc[:, SSD_WIDTH:SSD_WIDTH + SSD_GROUPS * SSD_STATE]
    c_scr[...] = xc[:, SSD_WIDTH + SSD_GROUPS * SSD_STATE:]

    r8 = lax.broadcasted_iota(jnp.int32, (DEC_SEQ, 2 * SSD_STATE), 0)
    l8 = lax.broadcasted_iota(jnp.int32, (DEC_SEQ, 2 * SSD_STATE), 1)
    ones_part = jnp.where(jnp.logical_and(r8 < 3, l8 >= SSD_STATE), 1.0, 0.0)
    zeros_b = jnp.zeros((DEC_SEQ, SSD_STATE), F32)

    def per_seq(b, carry):
        r0 = pl.multiple_of(b * DEC_SEQ, DEC_SEQ)
        rsl = pl.ds(r0, DEC_SEQ)
        for g in range(SSD_GROUPS):
            gs = slice(g * GROUP_WIDTH, (g + 1) * GROUP_WIDTH)
            ns = slice(g * SSD_STATE, (g + 1) * SSD_STATE)
            s_bg = s_ref[b, gs, :]
            yoff_scr[rsl, gs] = _dot_nt(c_scr[rsl, ns].astype(BF16), s_bg.astype(BF16))
            lhs = jnp.concatenate([xw_scr[rsl, gs], aux_scr[rsl, gs]], axis=0)
            rhs = jnp.concatenate(
                [jnp.concatenate([b_scr[rsl, ns], zeros_b], axis=1), ones_part], axis=0)
            res = _dot_tn(lhs.astype(BF16), rhs.astype(BF16))
            snew_ref[b, gs, :] = res[:, SSD_STATE:] * s_bg + res[:, :SSD_STATE]
        return carry

    lax.fori_loop(0, SAMPLE_BB, per_seq, 0, unroll=SEQ_UNROLL)

    y = y + yoff_scr[...] * jnp.exp(_expand(acum, e_ref[...]))
    y_ssd = _gated_rmsnorm(y, z_ref[...], ng_ref[...])
    y_ref[...] = jnp.concatenate([y_ssd, y_gm], axis=1).astype(BF16)


def _sample_mixer_call(z, xbc, buf8, dt, u, v, state, consts):
    t = z.shape[0]
    nb = t // CHUNK
    row = lambda i: (i, 0)
    st_spec = pl.BlockSpec((SAMPLE_BB, SSD_WIDTH, SSD_STATE), lambda i: (i, 0, 0))
    const_specs = [_resident(a.shape) for a in consts]
    return pl.pallas_call(
        _sample_mixer_kernel,
        grid=(nb,),
        in_specs=[pl.BlockSpec((CHUNK, SSD_WIDTH), row),
                  pl.BlockSpec((CHUNK, CONV_DIM), row),
                  pl.BlockSpec((CHUNK, CONV_DIM), row),
                  pl.BlockSpec((CHUNK, DT_PAD), row),
                  pl.BlockSpec((CHUNK, GM_WIDTH), row),
                  pl.BlockSpec((CHUNK, GM_WIDTH), row),
                  st_spec] + const_specs,
        out_specs=[pl.BlockSpec((CHUNK, 2 * SSD_WIDTH), row), st_spec],
        out_shape=[jax.ShapeDtypeStruct((t, 2 * SSD_WIDTH), BF16),
                   jax.ShapeDtypeStruct(state.shape, F32)],
        scratch_shapes=[pltpu.VMEM((CHUNK, SSD_GROUPS * SSD_STATE), F32),
                        pltpu.VMEM((CHUNK, SSD_GROUPS * SSD_STATE), F32),
                        pltpu.VMEM((CHUNK, SSD_WIDTH), F32),
                        pltpu.VMEM((CHUNK, SSD_WIDTH), F32),
                        pltpu.VMEM((CHUNK, SSD_WIDTH), F32),
                        pltpu.VMEM((GM_HEADS, CHUNK, CHUNK), BF16)],
        compiler_params=_params(("arbitrary",)),
        name="mixer_sample",
    )(z, xbc, buf8, dt, u, v, state, *consts)


ROW_CHAINS = 2


def _outln_kernel(*refs, per_seq):
    it = iter(refs)
    xn_ref, y_ref, g_ref = next(it), next(it), next(it)
    p3_ref = next(it) if per_seq else None
    w_ref, lg_ref, lb_ref, o_ref = (next(it) for _ in range(4))
    gate = 1.0 + _mod_rows(g_ref, p3_ref)
    half = xn_ref.shape[0] // ROW_CHAINS
    for r in range(ROW_CHAINS):
        rows = slice(r * half, (r + 1) * half)
        mix = _dot(y_ref[rows, :], w_ref[...])
        gate_r = gate if gate.shape[0] == 1 else gate[rows, :]
        o_ref[rows, :] = _layer_norm(ALPHA * xn_ref[rows, :] + gate_r * mix, lg_ref[...], lb_ref[...])


def _outln_call(xn, ymix, mod, rows_per_mod, per_seq, w_out, ln_g, ln_b, tm):
    t = xn.shape[0]
    row = lambda i: (i, 0)
    operands = [xn, ymix, mod]
    in_specs = ([pl.BlockSpec((tm, D_MODEL), row), pl.BlockSpec((tm, D_MODEL), row)]
                + _mod_specs(tm, rows_per_mod, per_seq, (0,)))
    if per_seq:
        operands.append(_repeat_matrix3(tm))
        in_specs.append(_resident((tm, _repeat_k(tm))))
    operands += [w_out, ln_g, ln_b]
    in_specs += [_resident(w_out.shape), _resident((1, D_MODEL)), _resident((1, D_MODEL))]
    return pl.pallas_call(
        functools.partial(_outln_kernel, per_seq=per_seq),
        grid=(t // tm,),
        in_specs=in_specs,
        out_specs=pl.BlockSpec((tm, D_MODEL), row),
        out_shape=jax.ShapeDtypeStruct((t, D_MODEL), F32),
        compiler_params=_params(("arbitrary",)),
        name="out_ln",
    )(*operands)


def _ffn_kernel(*refs, per_seq):
    it = iter(refs)
    x_ref, sh_ref, sc_ref, g_ref = (next(it) for _ in range(4))
    p3_ref = next(it) if per_seq else None
    w1_ref, w2_ref, lg_ref, lb_ref, o_ref = (next(it) for _ in range(5))
    h_scr = next(it) if per_seq else None
    j = pl.program_id(1)
    nj = pl.num_programs(1)

    @pl.when(j == 0)
    def _():
        if per_seq:
            h_scr[...] = (x_ref[...] * (1.0 + _mod_rows(sc_ref, p3_ref))
                          + _mod_rows(sh_ref, p3_ref)).astype(BF16)
        o_ref[...] = jnp.zeros_like(o_ref)

    def partial_out(rows):
        if per_seq:
            h = h_scr[rows, :]
        else:
            h = (x_ref[rows, :] * (1.0 + sc_ref[...]) + sh_ref[...]).astype(BF16)
        a = jnp.maximum(_dot(h, w1_ref[...]), 0.0)
        return o_ref[rows, :] + _dot((a * a).astype(BF16), w2_ref[...])

    @pl.when(j < nj - 1)
    def _():
        o_ref[...] = partial_out(slice(None))

    @pl.when(j == nj - 1)
    def _():
        gate = 1.0 + _mod_rows(g_ref, p3_ref)
        half = x_ref.shape[0] // ROW_CHAINS
        for r in range(ROW_CHAINS):
            rows = slice(r * half, (r + 1) * half)
            gate_r = gate if gate.shape[0] == 1 else gate[rows, :]
            o_ref[rows, :] = _layer_norm(ALPHA * x_ref[rows, :] + gate_r * partial_out(rows),
                                         lg_ref[...], lb_ref[...])


def _ffn_call(x1, mod, rows_per_mod, per_seq, w1_tiles, w2, ln_g, ln_b, tm):
    t = x1.shape[0]
    row = lambda i, j: (i, 0)
    operands = [x1, mod, mod, mod]
    in_specs = [pl.BlockSpec((tm, D_MODEL), row)] + _mod_specs(tm, rows_per_mod, per_seq, (1, 2, 3))
    if per_seq:
        operands.append(_repeat_matrix3(tm))
        in_specs.append(_resident((tm, _repeat_k(tm))))
    operands += [w1_tiles, w2, ln_g, ln_b]
    in_specs += [pl.BlockSpec((None, D_MODEL, FF_TILE), lambda i, j: (j, 0, 0)),
                 pl.BlockSpec((FF_TILE, D_MODEL), lambda i, j: (j, 0)),
                 _resident((1, D_MODEL)), _resident((1, D_MODEL))]
    return pl.pallas_call(
        functools.partial(_ffn_kernel, per_seq=per_seq),
        grid=(t // tm, N_FF_TILES),
        in_specs=in_specs,
        out_specs=pl.BlockSpec((tm, D_MODEL), row),
        out_shape=jax.ShapeDtypeStruct((t, D_MODEL), F32),
        scratch_shapes=[pltpu.VMEM((tm, D_MODEL), BF16)] if per_seq else [],
        compiler_params=_params(("arbitrary", "arbitrary")),
        name="ffn",
    )(*operands)


def kernel(x_prompt, x_sample, state_ssm, state_conv, c_prompt, c_sample, ln_in_g, ln_in_b, w_mod, b_mod, w_in, conv_w, conv_b, dt_bias, a_log, d_skip, ssd_norm_g, gm_ln_g, gm_ln_b, gm_w_s, gm_b_s, w_out, ln_mix_g, ln_mix_b, w_ff1, w_ff2, ln_ffn_g, ln_ffn_b):
    depth = w_mod.shape[0]
    assert depth == 1
    bp, seq, _ = x_prompt.shape
    bs, dec, _ = x_sample.shape
    assert dec == DEC_SEQ and seq % CHUNK == 0 and (bs * dec) % CHUNK == 0

    r1 = lambda a: a.reshape(1, -1)
    ln_in_g2, ln_in_b2 = r1(ln_in_g), r1(ln_in_b)
    l = 0

    w_ab = _win_call(jnp.swapaxes(w_in[l], 0, 1))

    head_of_chan = jnp.arange(SSD_WIDTH, dtype=jnp.int32) // SSD_HEADDIM
    e_sel = (jnp.arange(DT_PAD, dtype=jnp.int32)[:, None] == head_of_chan[None, :]).astype(BF16)
    e3 = jnp.concatenate([e_sel, e_sel, e_sel], axis=0)
    tril = jnp.tril(jnp.ones((CHUNK, CHUNK), BF16))
    tril3 = jnp.concatenate([tril, tril, tril], axis=1)
    d_row = r1(jnp.repeat(d_skip[l], SSD_HEADDIM))
    dtb = _pad_cols(r1(dt_bias[l]), DT_PAD)
    alog = _pad_cols(r1(a_log[l]), DT_PAD)
    mixer_consts = [conv_w[l], r1(conv_b[l]), alog, d_row, e3, tril3, r1(ssd_norm_g[l])]
    gm_g2, gm_b2 = r1(gm_ln_g[l]), r1(gm_ln_b[l])
    bsb_p = jnp.repeat(gm_b_s[l].T, GM_HEAD, axis=1)
    reps = CHUNK // DEC_SEQ
    bsb_s = jnp.tile(jnp.repeat(gm_b_s[l][:, :DEC_SEQ].T, GM_HEAD, axis=1), (reps, 1))

    n_c = bp + bs
    c_all = jnp.concatenate([c_prompt, c_sample], axis=0)
    c_all = jnp.pad(c_all, ((0, (-n_c) % SUBLANES), (0, 0)))
    mod_e, c_act = _mod_call(c_all, w_mod[l], r1(b_mod[l]))
    mod_ep = mod_e[:bp].reshape(bp, 1, MOD_EARLY * D_MODEL)
    mod_es = mod_e[bp:n_c]

    xp2 = x_prompt.reshape(bp * seq, D_MODEL)
    z, xbc, dtr, u, v, xn_p, w_out_b, mod_l = _inproj_call(
        xp2, mod_ep, seq, False, ln_in_g2, ln_in_b2, dtb, gm_g2, gm_b2, w_ab, tm=256,
        side=(w_out[l], c_act, w_mod[l], r1(b_mod[l])))
    mod_lp = mod_l[:bp].reshape(bp, 1, MOD_LATE * D_MODEL)
    mod_ls = mod_l[bp:n_c]
    ymix, ssm_p, w1_t, w2_b = _prompt_mixer_call(
        z, xbc, dtr, u, v, mixer_consts + [gm_w_s[l], bsb_p], bp, seq, w_ff1[l], w_ff2[l])
    x1 = _outln_call(xn_p, ymix, mod_lp, seq, False, w_out_b,
                     r1(ln_mix_g[l]), r1(ln_mix_b[l]), tm=512)
    yp = _ffn_call(x1, mod_lp, seq, False, w1_t, w2_b, r1(ln_ffn_g[l]), r1(ln_ffn_b[l]), tm=512)
    conv_p = xbc.reshape(bp, seq, CONV_DIM)[:, seq - (CONV_K - 1):, :]

    xs2 = x_sample.reshape(bs * dec, D_MODEL)
    zs, xbcs, dtrs, us, vn_s, xn_s = _inproj_call(xs2, mod_es, None, True, ln_in_g2, ln_in_b2, dtb,
                                                  gm_g2, gm_b2, w_ab, tm=256)
    buf8 = jnp.pad(state_conv[l], ((0, 0), (DEC_SEQ - (CONV_K - 1), 0), (0, 0)))
    buf8 = buf8.reshape(bs * dec, CONV_DIM)
    st_in = state_ssm[l].reshape(bs, SSD_WIDTH, SSD_STATE)
    ymix_s, ssm_s = _sample_mixer_call(zs, xbcs, buf8, dtrs, us, vn_s, st_in,
                                       mixer_consts + [gm_w_s[l], bsb_s])
    x1s = _outln_call(xn_s, ymix_s, mod_ls, None, True, w_out_b,
                      r1(ln_mix_g[l]), r1(ln_mix_b[l]), tm=512)
    ys = _ffn_call(x1s, mod_ls, None, True, w1_t, w2_b, r1(ln_ffn_g[l]), r1(ln_ffn_b[l]), tm=512)
    conv_s = xbcs.reshape(bs, dec, CONV_DIM)[:, dec - (CONV_K - 1):, :]

    return (yp.reshape(bp, seq, D_MODEL),
            ys.reshape(bs, dec, D_MODEL),
            ssm_p.reshape(1, bp, SSD_HEADS, SSD_HEADDIM, SSD_STATE),
            conv_p[None],
            ssm_s.reshape(1, bs, SSD_HEADS, SSD_HEADDIM, SSD_STATE),
            conv_s[None],
            vn_s.reshape(1, bs, dec, GM_WIDTH))
```

```python
import functools
import math

import jax
import jax.numpy as jnp
from jax import lax
from jax.experimental import pallas as pl
from jax.experimental.pallas import tpu as pltpu

D_MODEL = 2048
SSD_WIDTH = 1024
SSD_HEADDIM = 64
SSD_HEADS = 16
SSD_GROUPS = 2
SSD_STATE = 128
GROUP_WIDTH = SSD_WIDTH // SSD_GROUPS
CONV_K = 4
CONV_DIM = SSD_WIDTH + 2 * SSD_GROUPS * SSD_STATE
GM_WIDTH = 1024
GM_HEAD = 128
GM_HEADS = 8
D_FF = 4 * D_MODEL
FF_TILE = 1024
N_FF_TILES = D_FF // FF_TILE
CHUNK = 128
DEC_SEQ = 8
DT_PAD = 128
SUBLANES = 8
LANES = 128
ALPHA = 2.0 ** 0.25
LN_EPS = 1e-5
LOG2E = math.log2(math.e)

V7X_VMEM_BYTES = 64 * 1024 * 1024
VMEM_LIMIT = V7X_VMEM_BYTES - 8 * 1024 * 1024

F32 = jnp.float32
BF16 = jnp.bfloat16


def _layer_norm(x, g, b):
    mu = jnp.mean(x, axis=-1, keepdims=True)
    xc = x - mu
    var = jnp.mean(xc * xc, axis=-1, keepdims=True)
    return xc * lax.rsqrt(var + LN_EPS) * g + b


def _silu(x):
    h = 0.5 * x
    return h + h * jnp.tanh(h)


def _gelu_tanh(x):
    c = math.sqrt(2.0 / math.pi)
    h = 0.5 * x
    return h + h * jnp.tanh(x * (c + (c * 0.044715) * (x * x)))


def _softplus(x):
    return jnp.maximum(x, 0.0) + jnp.log1p(jnp.exp(-jnp.abs(x)))


def _dot(a, b):
    return jnp.dot(a, b, preferred_element_type=F32)


def _dot_nt(a, b):
    return lax.dot_general(a, b, (((1,), (1,)), ((), ())), preferred_element_type=F32)


def _dot_tn(a, b):
    return lax.dot_general(a, b, (((0,), (0,)), ((), ())), preferred_element_type=F32)


def _split3(x):
    hi = x.astype(BF16)
    r1 = x - hi.astype(F32)
    mid = r1.astype(BF16)
    lo = (r1 - mid.astype(F32)).astype(BF16)
    return hi, mid, lo


def _expand(x, sel3):
    return _dot(jnp.concatenate(_split3(x), axis=1), sel3)


def _mod_rows(ref, p3_ref):
    if p3_ref is None:
        return ref[...]
    parts = list(_split3(ref[...]))
    pad = p3_ref.shape[1] - 3 * ref.shape[0]
    if pad:
        parts.append(jnp.zeros((pad, ref.shape[1]), BF16))
    return _dot(p3_ref[...], jnp.concatenate(parts, axis=0))


def _resident(shape):
    nd = len(shape)
    return pl.BlockSpec(shape, lambda *_: (0,) * nd, pipeline_mode=pl.Buffered(1))


def _mod_specs(tm, rows_per_mod, per_seq, pieces):
    if per_seq:
        return [pl.BlockSpec((tm // DEC_SEQ, D_MODEL), lambda i, *_, p=p: (i, p)) for p in pieces]
    tiles_per_mod = rows_per_mod // tm
    return [pl.BlockSpec((None, 1, D_MODEL), lambda i, *_, p=p: (i // tiles_per_mod, 0, p))
            for p in pieces]


def _repeat_matrix3(tm):
    nb = tm // DEC_SEQ
    sel = (jnp.arange(tm, dtype=jnp.int32)[:, None] // DEC_SEQ
           == jnp.arange(nb, dtype=jnp.int32)[None, :]).astype(BF16)
    return _pad_cols(jnp.concatenate([sel, sel, sel], axis=1), _repeat_k(tm))


def _repeat_k(tm):
    return -(-(3 * tm // DEC_SEQ) // LANES) * LANES


def _pad_cols(a, n):
    return jnp.pad(a, ((0, 0), (0, n - a.shape[1])))


def _params(sem):
    return pltpu.CompilerParams(dimension_semantics=sem, vmem_limit_bytes=VMEM_LIMIT)


MOD_EARLY = 2
MOD_LATE = 4


def _mod_kernel(c_ref, w_ref, b_ref, o_ref, a_ref):
    a = _silu(c_ref[...]).astype(BF16)
    a_ref[...] = a
    o_ref[...] = _dot(a, w_ref[...].astype(BF16)) + b_ref[...]


def _mod_call(c_all, w_mod, b_mod):
    m = c_all.shape[0]
    tn = 1024
    return pl.pallas_call(
        _mod_kernel,
        grid=(MOD_EARLY * D_MODEL // tn,),
        in_specs=[
            pl.BlockSpec((m, D_MODEL), lambda j: (0, 0)),
            pl.BlockSpec((D_MODEL, tn), lambda j: (0, j)),
            pl.BlockSpec((1, tn), lambda j: (0, j)),
        ],
        out_specs=[pl.BlockSpec((m, tn), lambda j: (0, j)),
                   pl.BlockSpec((m, D_MODEL), lambda j: (0, 0))],
        out_shape=[jax.ShapeDtypeStruct((m, MOD_EARLY * D_MODEL), F32),
                   jax.ShapeDtypeStruct((m, D_MODEL), BF16)],
        compiler_params=_params(("arbitrary",)),
        name="mod",
    )(c_all, w_mod, b_mod)


IN_WIDTHS = (SSD_WIDTH, CONV_DIM, DT_PAD, GM_WIDTH, GM_WIDTH)
WA_COLS = SSD_WIDTH + CONV_DIM + DT_PAD
WB_COLS = 2 * GM_WIDTH
DT_ROW = SSD_WIDTH + CONV_DIM
UV_ROW = DT_ROW + SSD_HEADS
WA_BAND = 384
WB_BAND = 512
assert WA_COLS % WA_BAND == 0 and WB_COLS % WB_BAND == 0 and DT_ROW % WB_BAND == 0


def _win_a_kernel(a_ref, o_ref):
    blk = a_ref[...]
    row = lax.broadcasted_iota(jnp.int32, blk.shape, 0) + pl.program_id(0) * WA_BAND
    o_ref[...] = jnp.where(row < UV_ROW, blk, 0.0).T.astype(BF16)


def _win_b_kernel(a_ref, b_ref, o_ref):
    blk = jnp.concatenate([a_ref[SSD_HEADS:, :], b_ref[0:SSD_HEADS, :]], axis=0)
    o_ref[...] = blk.T.astype(BF16)


def _win_call(w_in_t):
    first_b = DT_ROW // WB_BAND
    w_a = pl.pallas_call(
        _win_a_kernel,
        grid=(WA_COLS // WA_BAND,),
        in_specs=[pl.BlockSpec((WA_BAND, D_MODEL), lambda s: (s, 0))],
        out_specs=pl.BlockSpec((D_MODEL, WA_BAND), lambda s: (0, s)),
        out_shape=jax.ShapeDtypeStruct((D_MODEL, WA_COLS), BF16),
        compiler_params=_params(("arbitrary",)),
        name="w_in_cast_a",
    )(w_in_t)
    w_b = pl.pallas_call(
        _win_b_kernel,
        grid=(WB_COLS // WB_BAND,),
        in_specs=[pl.BlockSpec((WB_BAND, D_MODEL), lambda s: (first_b + s, 0)),
                  pl.BlockSpec((WB_BAND, D_MODEL), lambda s: (first_b + s + 1, 0))],
        out_specs=pl.BlockSpec((D_MODEL, WB_BAND), lambda s: (0, s)),
        out_shape=jax.ShapeDtypeStruct((D_MODEL, WB_COLS), BF16),
        compiler_params=_params(("arbitrary",)),
        name="w_in_cast_b",
    )(w_in_t, w_in_t)
    return w_a, w_b


def _inproj_kernel(*refs, per_seq, side_jobs):
    it = iter(refs)
    x_ref, sh_ref, sc_ref = next(it), next(it), next(it)
    p3_ref = next(it) if per_seq else None
    g_ref, b_ref, dtb_ref, gg_ref, gb_ref = (next(it) for _ in range(5))
    wa_ref, wb_ref = next(it), next(it)
    if side_jobs:
        wof_ref, ca_ref, wmod_ref, bmod_ref = (next(it) for _ in range(4))
    o_refs = [next(it) for _ in range(len(IN_WIDTHS))]
    xn_ref = next(it)

    xn = _layer_norm(x_ref[...], g_ref[...], b_ref[...])
    xn_ref[...] = xn
    h = (xn * (1.0 + _mod_rows(sc_ref, p3_ref)) + _mod_rows(sh_ref, p3_ref)).astype(BF16)
    acts = (_silu, None, lambda r: _softplus(r + dtb_ref[...]), _gelu_tanh,
            lambda r: _layer_norm(_gelu_tanh(r), gg_ref[...], gb_ref[...]))
    pieces = []
    w_ref, off = wa_ref, 0
    for width, o_ref, act in zip(IN_WIDTHS, o_refs, acts):
        if off == WA_COLS:
            w_ref, off = wb_ref, 0
        pieces.append((w_ref, off, width, o_ref, act))
        off += width
    for w_ref, off, width, o_ref, act in [pieces[-1]] + pieces[:-1]:
        r = _dot(h, w_ref[:, off:off + width])
        o_ref[...] = r if act is None else act(r)

    if side_jobs:
        wob_ref, modl_ref = next(it), next(it)
        wob_ref[...] = wof_ref[...].astype(BF16)
        modl_ref[...] = _dot(ca_ref[...], wmod_ref[...].astype(BF16)) + bmod_ref[...]


def _inproj_call(x2d, mod, rows_per_mod, per_seq, ln_g, ln_b, dtb, gm_g, gm_b, w_ab, tm, side=None):
    t = x2d.shape[0]
    steps = t // tm
    widths = IN_WIDTHS + (D_MODEL,)
    row = lambda i: (i, 0)
    side_jobs = side is not None
    operands = [x2d, mod, mod]
    in_specs = [pl.BlockSpec((tm, D_MODEL), row)] + _mod_specs(tm, rows_per_mod, per_seq, (0, 1))
    if per_seq:
        operands.append(_repeat_matrix3(tm))
        in_specs.append(_resident((tm, _repeat_k(tm))))
    operands += [ln_g, ln_b, dtb, gm_g, gm_b, *w_ab]
    in_specs += [_resident((1, D_MODEL)), _resident((1, D_MODEL)), _resident(dtb.shape),
                 _resident(gm_g.shape), _resident(gm_b.shape)]
    in_specs += [_resident(w.shape) for w in w_ab]
    out_specs = [pl.BlockSpec((tm, n), row) for n in widths]
    out_shape = [jax.ShapeDtypeStruct((t, n), F32) for n in widths]
    if side_jobs:
        w_out, c_act, w_mod, b_mod = side
        r1 = D_MODEL // steps
        m = c_act.shape[0]
        late = MOD_LATE * D_MODEL
        mband = late // steps
        first = MOD_EARLY * D_MODEL // mband
        operands += [w_out, c_act, w_mod, b_mod]
        in_specs += [pl.BlockSpec((r1, D_MODEL), row), _resident(c_act.shape),
                     pl.BlockSpec((D_MODEL, mband), lambda i: (0, first + i)),
                     pl.BlockSpec((1, mband), lambda i: (0, first + i))]
        out_specs += [pl.BlockSpec((r1, D_MODEL), row), pl.BlockSpec((m, mband), lambda i: (0, i))]
        out_shape += [jax.ShapeDtypeStruct((D_MODEL, D_MODEL), BF16),
                      jax.ShapeDtypeStruct((m, late), F32)]
    return pl.pallas_call(
        functools.partial(_inproj_kernel, per_seq=per_seq, side_jobs=side_jobs),
        grid=(steps,),
        in_specs=in_specs,
        out_specs=out_specs,
        out_shape=out_shape,
        compiler_params=_params(("arbitrary",)),
        name="in_proj",
    )(*operands)


def _ssd_gmlp_block(*, xc, acum, tot, dt, d_row, e3, mask, state_t, wm_ref, bsb, u, vn):
    rows = xc.shape[0]
    xs = xc[:, :SSD_WIDTH]
    bm = xc[:, SSD_WIDTH:SSD_WIDTH + SSD_GROUPS * SSD_STATE]
    cm = xc[:, SSD_WIDTH + SSD_GROUPS * SSD_STATE:]

    acum2 = acum * LOG2E
    acum2_t = acum2.T
    dt_t = dt.T
    w_end = jnp.exp(tot - acum) * dt
    xw = xs * _expand(w_end, e3)
    eacum = jnp.exp(acum)

    xs_b = xs.astype(BF16)
    lane = lax.broadcasted_iota(jnp.int32, (rows, 2 * SSD_HEADDIM), 1)
    low_half = lane < SSD_HEADDIM

    scores = []
    for g in range(SSD_GROUPS):
        cg = cm[:, g * SSD_STATE:(g + 1) * SSD_STATE].astype(BF16)
        bg = bm[:, g * SSD_STATE:(g + 1) * SSD_STATE].astype(BF16)
        scores.append(_dot_nt(cg, bg))

    heads_per_group = SSD_HEADS // SSD_GROUPS
    y_pairs = []
    for pair in range(SSD_HEADS // 2):
        sl = slice(pair * 2 * SSD_HEADDIM, (pair + 1) * 2 * SSD_HEADDIM)
        x_pair = xs_b[:, sl]
        zero = jnp.zeros_like(x_pair)
        if state_t is not None:
            s_pair = state_t[:, sl].astype(BF16)
        acc = None
        for k in range(2):
            h = 2 * pair + k
            g = h // heads_per_group
            seg2 = acum2[:, h:h + 1] - acum2_t[h:h + 1, :]
            m = jnp.where(mask, scores[g] * jnp.exp2(seg2) * dt_t[h:h + 1, :], 0.0)
            keep = low_half if k == 0 else jnp.logical_not(low_half)
            rhs = jnp.where(keep, x_pair, zero)
            lhs = m.astype(BF16)
            if state_t is not None:
                c_sc = cm[:, g * SSD_STATE:(g + 1) * SSD_STATE] * eacum[:, h:h + 1]
                lhs = jnp.concatenate([lhs, c_sc.astype(BF16)], axis=1)
                rhs = jnp.concatenate([rhs, jnp.where(keep, s_pair, zero)], axis=0)
            part = _dot(lhs, rhs)
            acc = part if acc is None else acc + part
        y_pairs.append(acc)
    y = jnp.concatenate(y_pairs, axis=1) + d_row * xs

    vn_b = vn.astype(BF16)
    mixed = []
    for h in range(GM_HEADS):
        mixed.append(_dot(wm_ref[h], vn_b[:, h * GM_HEAD:(h + 1) * GM_HEAD]))
    y_gm = u * (jnp.concatenate(mixed, axis=1) + bsb)
    return y, xw, y_gm


def _gated_rmsnorm(y, z_act, norm_g):
    hg = y * z_act
    parts = []
    for g in range(SSD_GROUPS):
        hh = hg[:, g * GROUP_WIDTH:(g + 1) * GROUP_WIDTH]
        ms = jnp.mean(hh * hh, axis=-1, keepdims=True)
        parts.append(hh * lax.rsqrt(ms + LN_EPS))
    return jnp.concatenate(parts, axis=1) * norm_g


def _conv_silu(taps, conv_w_ref, conv_b):
    acc = conv_b + conv_w_ref[0:1, :] * taps[0]
    for k in range(1, CONV_K):
        acc = acc + conv_w_ref[k:k + 1, :] * taps[k]
    return _silu(acc)


MIX_CHUNKS = 4


def _prompt_mixer_kernel(z_ref, xbc_ref, dt_ref, u_ref, v_ref, conv_w_ref, conv_b_ref,
                         alog_ref, d_ref, e_ref, tril3_ref, ng_ref, ws_ref, bsb_ref,
                         w1f_ref, w2f_ref,
                         y_ref, ssm_ref, w1b_ref, w2b_ref, halo_ref, st_ref, wm_scr):
    for t in range(N_FF_TILES):
        w1b_ref[t] = w1f_ref[:, t * FF_TILE:(t + 1) * FF_TILE].astype(BF16)
    w2b_ref[...] = w2f_ref[...].astype(BF16)

    c = pl.program_id(1)
    nc = pl.num_programs(1)
    ri = lax.broadcasted_iota(jnp.int32, (CHUNK, CHUNK), 0)
    ci = lax.broadcasted_iota(jnp.int32, (CHUNK, CHUNK), 1)
    causal = ci <= ri

    @pl.when(jnp.logical_and(pl.program_id(0) == 0, c == 0))
    def _():
        for h in range(GM_HEADS):
            wm_scr[h] = jnp.where(causal, ws_ref[h], 0.0).astype(BF16)

    @pl.when(c == 0)
    def _():
        halo_ref[...] = jnp.zeros_like(halo_ref)
        st_ref[...] = jnp.zeros_like(st_ref)

    halo = halo_ref[...]
    state_t = st_ref[...]
    for k in range(MIX_CHUNKS):
        rows = slice(k * CHUNK, (k + 1) * CHUNK)
        xbc = xbc_ref[rows, :]
        xp = jnp.concatenate([halo, xbc], axis=0)
        taps = [pltpu.roll(xp, CONV_K - 1 - j, 0)[SUBLANES:, :] for j in range(CONV_K - 1)]
        taps.append(xbc)
        xc = _conv_silu(taps, conv_w_ref, conv_b_ref[...])
        halo = xbc[CHUNK - SUBLANES:, :]

        dt = dt_ref[rows, :]
        a = dt * (-jnp.exp(alog_ref[...]))
        acum = _dot(tril3_ref[...], jnp.concatenate(_split3(a), axis=0))
        tot = jnp.broadcast_to(acum[CHUNK - 1:CHUNK, :], (CHUNK, DT_PAD))

        y, xw, y_gm = _ssd_gmlp_block(
            xc=xc, acum=acum, tot=tot, dt=dt, d_row=d_ref[...], e3=e_ref[...], mask=causal,
            state_t=state_t, wm_ref=wm_scr, bsb=bsb_ref[...], u=u_ref[rows, :], vn=v_ref[rows, :])

        cd = jnp.exp(_expand(tot[0:SUBLANES, :], e_ref[...]))[0:1, :]
        bm = xc[:, SSD_WIDTH:SSD_WIDTH + SSD_GROUPS * SSD_STATE].astype(BF16)
        xw_b = xw.astype(BF16)
        upd = [_dot_tn(bm[:, g * SSD_STATE:(g + 1) * SSD_STATE],
                       xw_b[:, g * GROUP_WIDTH:(g + 1) * GROUP_WIDTH]) for g in range(SSD_GROUPS)]
        state_t = state_t * cd + jnp.concatenate(upd, axis=1)

        y_ssd = _gated_rmsnorm(y, z_ref[rows, :], ng_ref[...])
        y_ref[rows, :] = jnp.concatenate([y_ssd, y_gm], axis=1).astype(BF16)

    halo_ref[...] = halo
    st_ref[...] = state_t

    @pl.when(c == nc - 1)
    def _():
        ssm_ref[...] = state_t.T


def _prompt_mixer_call(z, xbc, dt, u, v, consts, batch, seq, w_ff1, w_ff2):
    rows = MIX_CHUNKS * CHUNK
    nc = seq // rows
    row = lambda b, c: (b * nc + c, 0)
    t = z.shape[0]
    const_specs = [_resident(a.shape) for a in consts]
    steps = batch * nc
    r1, r2 = D_MODEL // steps, D_FF // steps
    return pl.pallas_call(
        _prompt_mixer_kernel,
        grid=(batch, nc),
        in_specs=[pl.BlockSpec((rows, SSD_WIDTH), row),
                  pl.BlockSpec((rows, CONV_DIM), row),
                  pl.BlockSpec((rows, DT_PAD), row),
                  pl.BlockSpec((rows, GM_WIDTH), row),
                  pl.BlockSpec((rows, GM_WIDTH), row)] + const_specs
        + [pl.BlockSpec((r1, D_FF), row), pl.BlockSpec((r2, D_MODEL), row)],
        out_specs=[pl.BlockSpec((rows, 2 * SSD_WIDTH), row),
                   pl.BlockSpec((None, SSD_WIDTH, SSD_STATE), lambda b, c: (b, 0, 0)),
                   pl.BlockSpec((N_FF_TILES, r1, FF_TILE), lambda b, c: (0, b * nc + c, 0)),
                   pl.BlockSpec((r2, D_MODEL), row)],
        out_shape=[jax.ShapeDtypeStruct((t, 2 * SSD_WIDTH), BF16),
                   jax.ShapeDtypeStruct((batch, SSD_WIDTH, SSD_STATE), F32),
                   jax.ShapeDtypeStruct((N_FF_TILES, D_MODEL, FF_TILE), BF16),
                   jax.ShapeDtypeStruct((D_FF, D_MODEL), BF16)],
        scratch_shapes=[pltpu.VMEM((SUBLANES, CONV_DIM), F32),
                        pltpu.VMEM((SSD_STATE, SSD_WIDTH), F32),
                        pltpu.VMEM((GM_HEADS, CHUNK, CHUNK), BF16)],
        compiler_params=_params(("arbitrary", "arbitrary")),
        name="mixer_prompt",
    )(z, xbc, dt, u, v, *consts, w_ff1, w_ff2)


SAMPLE_BB = CHUNK // DEC_SEQ
SEQ_UNROLL = 16


def _seg_cumsum(a, t):
    k = 1
    while k < DEC_SEQ:
        a = a + jnp.where(t >= k, pltpu.roll(a, k, 0), 0.0)
        k *= 2
    return a


def _seg_last(a, t):
    rows = a.shape[0]
    x = jnp.where(t == DEC_SEQ - 1, a, 0.0)
    k = 1
    while k < DEC_SEQ:
        x = x + pltpu.roll(x, rows - k, 0)
        k *= 2
    return x


def _sample_mixer_kernel(z_ref, xbc_ref, buf_ref, dt_ref, u_ref, v_ref, s_ref, conv_w_ref,
                         conv_b_ref, alog_ref, d_ref, e_ref, tril3_ref, ng_ref, ws_ref, bsb_ref,
                         y_ref, snew_ref,
                         c_scr, b_scr, xw_scr, aux_scr, yoff_scr, wm_scr):
    del tril3_ref
    rows = CHUNK
    tcol = lax.broadcasted_iota(jnp.int32, (rows, 1), 0) % DEC_SEQ
    ri = lax.broadcasted_iota(jnp.int32, (rows, rows), 0)
    ci = lax.broadcasted_iota(jnp.int32, (rows, rows), 1)
    mask = jnp.logical_and(ci <= ri, (ci // DEC_SEQ) == (ri // DEC_SEQ))

    @pl.when(pl.program_id(0) == 0)
    def _():
        rep_r = (ci == ri % DEC_SEQ).astype(BF16)
        rep_c = (ri == ci % DEC_SEQ).astype(BF16)
        corner = jnp.logical_and(ri < DEC_SEQ, ci < DEC_SEQ)
        for h in range(GM_HEADS):
            w8 = jnp.where(corner, ws_ref[h], 0.0).astype(BF16)
            tiled = _dot(_dot(rep_r, w8).astype(BF16), rep_c)
            wm_scr[h] = jnp.where(mask, tiled, 0.0).astype(BF16)

    xbc = xbc_ref[...]
    buf = buf_ref[...]
    taps = []
    for k in range(CONV_K - 1):
        back = CONV_K - 1 - k
        taps.append(jnp.where(tcol >= back, pltpu.roll(xbc, back, 0),
                              pltpu.roll(buf, rows - DEC_SEQ + back, 0)))
    taps.append(xbc)
    xc = _conv_silu(taps, conv_w_ref, conv_b_ref[...])

    dt = dt_ref[...]
    a = dt * (-jnp.exp(alog_ref[...]))
    acum = _seg_cumsum(a, tcol)
    tot = _seg_last(acum, tcol)

    y, xw, y_gm = _ssd_gmlp_block(
        xc=xc, acum=acum, tot=tot, dt=dt, d_row=d_ref[...], e3=e_ref[...], mask=mask,
        state_t=None, wm_ref=wm_scr, bsb=bsb_ref[...], u=u_ref[...], vn=v_ref[...])

    tfull = lax.broadcasted_iota(jnp.int32, (rows, SSD_WIDTH), 0) % DEC_SEQ
    dcx = jnp.exp(_expand(tot, e_ref[...]))
    hi, mid, lo = _split3(dcx)
    aux_scr[...] = jnp.where(tfull == 0, hi.astype(F32),
                             jnp.where(tfull == 1, mid.astype(F32),
                                       jnp.where(tfull == 2, lo.astype(F32), 0.0)))
    xw_scr[...] = xw
    b_scr[...] = xc[:, SSD_WIDTH:SSD_WIDTH + SSD_GROUPS * SSD_STATE]
    c_scr[...] = xc[:, SSD_WIDTH + SSD_GROUPS * SSD_STATE:]

    r8 = lax.broadcasted_iota(jnp.int32, (DEC_SEQ, 2 * SSD_STATE), 0)
    l8 = lax.broadcasted_iota(jnp.int32, (DEC_SEQ, 2 * SSD_STATE), 1)
    ones_part = jnp.where(jnp.logical_and(r8 < 3, l8 >= SSD_STATE), 1.0, 0.0)
    zeros_b = jnp.zeros((DEC_SEQ, SSD_STATE), F32)

    def per_seq(b, carry):
        r0 = pl.multiple_of(b * DEC_SEQ, DEC_SEQ)
        rsl = pl.ds(r0, DEC_SEQ)
        for g in range(SSD_GROUPS):
            gs = slice(g * GROUP_WIDTH, (g + 1) * GROUP_WIDTH)
            ns = slice(g * SSD_STATE, (g + 1) * SSD_STATE)
            s_bg = s_ref[b, gs, :]
            yoff_scr[rsl, gs] = _dot_nt(c_scr[rsl, ns].astype(BF16), s_bg.astype(BF16))
            lhs = jnp.concatenate([xw_scr[rsl, gs], aux_scr[rsl, gs]], axis=0)
            rhs = jnp.concatenate(
                [jnp.concatenate([b_scr[rsl, ns], zeros_b], axis=1), ones_part], axis=0)
            res = _dot_tn(lhs.astype(BF16), rhs.astype(BF16))
            snew_ref[b, gs, :] = res[:, SSD_STATE:] * s_bg + res[:, :SSD_STATE]
        return carry

    lax.fori_loop(0, SAMPLE_BB, per_seq, 0, unroll=SEQ_UNROLL)

    y = y + yoff_scr[...] * jnp.exp(_expand(acum, e_ref[...]))
    y_ssd = _gated_rmsnorm(y, z_ref[...], ng_ref[...])
    y_ref[...] = jnp.concatenate([y_ssd, y_gm], axis=1).astype(BF16)


def _sample_mixer_call(z, xbc, buf8, dt, u, v, state, consts):
    t = z.shape[0]
    nb = t // CHUNK
    row = lambda i: (i, 0)
    st_spec = pl.BlockSpec((SAMPLE_BB, SSD_WIDTH, SSD_STATE), lambda i: (i, 0, 0))
    const_specs = [_resident(a.shape) for a in consts]
    return pl.pallas_call(
        _sample_mixer_kernel,
        grid=(nb,),
        in_specs=[pl.BlockSpec((CHUNK, SSD_WIDTH), row),
                  pl.BlockSpec((CHUNK, CONV_DIM), row),
                  pl.BlockSpec((CHUNK, CONV_DIM), row),
                  pl.BlockSpec((CHUNK, DT_PAD), row),
                  pl.BlockSpec((CHUNK, GM_WIDTH), row),
                  pl.BlockSpec((CHUNK, GM_WIDTH), row),
                  st_spec] + const_specs,
        out_specs=[pl.BlockSpec((CHUNK, 2 * SSD_WIDTH), row), st_spec],
        out_shape=[jax.ShapeDtypeStruct((t, 2 * SSD_WIDTH), BF16),
                   jax.ShapeDtypeStruct(state.shape, F32)],
        scratch_shapes=[pltpu.VMEM((CHUNK, SSD_GROUPS * SSD_STATE), F32),
                        pltpu.VMEM((CHUNK, SSD_GROUPS * SSD_STATE), F32),
                        pltpu.VMEM((CHUNK, SSD_WIDTH), F32),
                        pltpu.VMEM((CHUNK, SSD_WIDTH), F32),
                        pltpu.VMEM((CHUNK, SSD_WIDTH), F32),
                        pltpu.VMEM((GM_HEADS, CHUNK, CHUNK), BF16)],
        compiler_params=_params(("arbitrary",)),
        name="mixer_sample",
    )(z, xbc, buf8, dt, u, v, state, *consts)


ROW_CHAINS = 2


def _outln_kernel(*refs, per_seq):
    it = iter(refs)
    xn_ref, y_ref, g_ref = next(it), next(it), next(it)
    p3_ref = next(it) if per_seq else None
    w_ref, lg_ref, lb_ref, o_ref = (next(it) for _ in range(4))
    gate = 1.0 + _mod_rows(g_ref, p3_ref)
    half = xn_ref.shape[0] // ROW_CHAINS
    for r in range(ROW_CHAINS):
        rows = slice(r * half, (r + 1) * half)
        mix = _dot(y_ref[rows, :], w_ref[...])
        gate_r = gate if gate.shape[0] == 1 else gate[rows, :]
        o_ref[rows, :] = _layer_norm(ALPHA * xn_ref[rows, :] + gate_r * mix, lg_ref[...], lb_ref[...])


def _outln_call(xn, ymix, mod, rows_per_mod, per_seq, w_out, ln_g, ln_b, tm):
    t = xn.shape[0]
    row = lambda i: (i, 0)
    operands = [xn, ymix, mod]
    in_specs = ([pl.BlockSpec((tm, D_MODEL), row), pl.BlockSpec((tm, D_MODEL), row)]
                + _mod_specs(tm, rows_per_mod, per_seq, (0,)))
    if per_seq:
        operands.append(_repeat_matrix3(tm))
        in_specs.append(_resident((tm, _repeat_k(tm))))
    operands += [w_out, ln_g, ln_b]
    in_specs += [_resident(w_out.shape), _resident((1, D_MODEL)), _resident((1, D_MODEL))]
    return pl.pallas_call(
        functools.partial(_outln_kernel, per_seq=per_seq),
        grid=(t // tm,),
        in_specs=in_specs,
        out_specs=pl.BlockSpec((tm, D_MODEL), row),
        out_shape=jax.ShapeDtypeStruct((t, D_MODEL), F32),
        compiler_params=_params(("arbitrary",)),
        name="out_ln",
    )(*operands)


def _ffn_kernel(*refs, per_seq):
    it = iter(refs)
    x_ref, sh_ref, sc_ref, g_ref = (next(it) for _ in range(4))
    p3_ref = next(it) if per_seq else None
    w1_ref, w2_ref, lg_ref, lb_ref, o_ref = (next(it) for _ in range(5))
    h_scr = next(it) if per_seq else None
    j = pl.program_id(1)
    nj = pl.num_programs(1)

    @pl.when(j == 0)
    def _():
        if per_seq:
            h_scr[...] = (x_ref[...] * (1.0 + _mod_rows(sc_ref, p3_ref))
                          + _mod_rows(sh_ref, p3_ref)).astype(BF16)
        o_ref[...] = jnp.zeros_like(o_ref)

    def partial_out(rows):
        if per_seq:
            h = h_scr[rows, :]
        else:
            h = (x_ref[rows, :] * (1.0 + sc_ref[...]) + sh_ref[...]).astype(BF16)
        a = jnp.maximum(_dot(h, w1_ref[...]), 0.0)
        return o_ref[rows, :] + _dot((a * a).astype(BF16), w2_ref[...])

    @pl.when(j < nj - 1)
    def _():
        o_ref[...] = partial_out(slice(None))

    @pl.when(j == nj - 1)
    def _():
        gate = 1.0 + _mod_rows(g_ref, p3_ref)
        half = x_ref.shape[0] // ROW_CHAINS
        for r in range(ROW_CHAINS):
            rows = slice(r * half, (r + 1) * half)
            gate_r = gate if gate.shape[0] == 1 else gate[rows, :]
            o_ref[rows, :] = _layer_norm(ALPHA * x_ref[rows, :] + gate_r * partial_out(rows),
                                         lg_ref[...], lb_ref[...])


def _ffn_call(x1, mod, rows_per_mod, per_seq, w1_tiles, w2, ln_g, ln_b, tm):
    t = x1.shape[0]
    row = lambda i, j: (i, 0)
    operands = [x1, mod, mod, mod]
    in_specs = [pl.BlockSpec((tm, D_MODEL), row)] + _mod_specs(tm, rows_per_mod, per_seq, (1, 2, 3))
    if per_seq:
        operands.append(_repeat_matrix3(tm))
        in_specs.append(_resident((tm, _repeat_k(tm))))
    operands += [w1_tiles, w2, ln_g, ln_b]
    in_specs += [pl.BlockSpec((None, D_MODEL, FF_TILE), lambda i, j: (j, 0, 0)),
                 pl.BlockSpec((FF_TILE, D_MODEL), lambda i, j: (j, 0)),
                 _resident((1, D_MODEL)), _resident((1, D_MODEL))]
    return pl.pallas_call(
        functools.partial(_ffn_kernel, per_seq=per_seq),
        grid=(t // tm, N_FF_TILES),
        in_specs=in_specs,
        out_specs=pl.BlockSpec((tm, D_MODEL), row),
        out_shape=jax.ShapeDtypeStruct((t, D_MODEL), F32),
        scratch_shapes=[pltpu.VMEM((tm, D_MODEL), BF16)] if per_seq else [],
        compiler_params=_params(("arbitrary", "arbitrary")),
        name="ffn",
    )(*operands)


def kernel(x_prompt, x_sample, state_ssm, state_conv, c_prompt, c_sample, ln_in_g, ln_in_b, w_mod, b_mod, w_in, conv_w, conv_b, dt_bias, a_log, d_skip, ssd_norm_g, gm_ln_g, gm_ln_b, gm_w_s, gm_b_s, w_out, ln_mix_g, ln_mix_b, w_ff1, w_ff2, ln_ffn_g, ln_ffn_b):
    depth = w_mod.shape[0]
    assert depth == 1
    bp, seq, _ = x_prompt.shape
    bs, dec, _ = x_sample.shape
    assert dec == DEC_SEQ and seq % CHUNK == 0 and (bs * dec) % CHUNK == 0

    r1 = lambda a: a.reshape(1, -1)
    ln_in_g2, ln_in_b2 = r1(ln_in_g), r1(ln_in_b)
    l = 0

    w_ab = _win_call(jnp.swapaxes(w_in[l], 0, 1))

    head_of_chan = jnp.arange(SSD_WIDTH, dtype=jnp.int32) // SSD_HEADDIM
    e_sel = (jnp.arange(DT_PAD, dtype=jnp.int32)[:, None] == head_of_chan[None, :]).astype(BF16)
    e3 = jnp.concatenate([e_sel, e_sel, e_sel], axis=0)
    tril = jnp.tril(jnp.ones((CHUNK, CHUNK), BF16))
    tril3 = jnp.concatenate([tril, tril, tril], axis=1)
    d_row = r1(jnp.repeat(d_skip[l], SSD_HEADDIM))
    dtb = _pad_cols(r1(dt_bias[l]), DT_PAD)
    alog = _pad_cols(r1(a_log[l]), DT_PAD)
    mixer_consts = [conv_w[l], r1(conv_b[l]), alog, d_row, e3, tril3, r1(ssd_norm_g[l])]
    gm_g2, gm_b2 = r1(gm_ln_g[l]), r1(gm_ln_b[l])
    bsb_p = jnp.repeat(gm_b_s[l].T, GM_HEAD, axis=1)
    reps = CHUNK // DEC_SEQ
    bsb_s = jnp.tile(jnp.repeat(gm_b_s[l][:, :DEC_SEQ].T, GM_HEAD, axis=1), (reps, 1))

    n_c = bp + bs
    c_all = jnp.concatenate([c_prompt, c_sample], axis=0)
    c_all = jnp.pad(c_all, ((0, (-n_c) % SUBLANES), (0, 0)))
    mod_e, c_act = _mod_call(c_all, w_mod[l], r1(b_mod[l]))
    mod_ep = mod_e[:bp].reshape(bp, 1, MOD_EARLY * D_MODEL)
    mod_es = mod_e[bp:n_c]

    xp2 = x_prompt.reshape(bp * seq, D_MODEL)
    z, xbc, dtr, u, v, xn_p, w_out_b, mod_l = _inproj_call(
        xp2, mod_ep, seq, False, ln_in_g2, ln_in_b2, dtb, gm_g2, gm_b2, w_ab, tm=256,
        side=(w_out[l], c_act, w_mod[l], r1(b_mod[l])))
    mod_lp = mod_l[:bp].reshape(bp, 1, MOD_LATE * D_MODEL)
    mod_ls = mod_l[bp:n_c]
    ymix, ssm_p, w1_t, w2_b = _prompt_mixer_call(
        z, xbc, dtr, u, v, mixer_consts + [gm_w_s[l], bsb_p], bp, seq, w_ff1[l], w_ff2[l])
    x1 = _outln_call(xn_p, ymix, mod_lp, seq, False, w_out_b,
                     r1(ln_mix_g[l]), r1(ln_mix_b[l]), tm=1024)
    yp = _ffn_call(x1, mod_lp, seq, False, w1_t, w2_b, r1(ln_ffn_g[l]), r1(ln_ffn_b[l]), tm=512)
    conv_p = xbc.reshape(bp, seq, CONV_DIM)[:, seq - (CONV_K - 1):, :]

    xs2 = x_sample.reshape(bs * dec, D_MODEL)
    zs, xbcs, dtrs, us, vn_s, xn_s = _inproj_call(xs2, mod_es, None, True, ln_in_g2, ln_in_b2, dtb,
                                                  gm_g2, gm_b2, w_ab, tm=256)
    buf8 = jnp.pad(state_conv[l], ((0, 0), (DEC_SEQ - (CONV_K - 1), 0), (0, 0)))
    buf8 = buf8.reshape(bs * dec, CONV_DIM)
    st_in = state_ssm[l].reshape(bs, SSD_WIDTH, SSD_STATE)
    ymix_s, ssm_s = _sample_mixer_call(zs, xbcs, buf8, dtrs, us, vn_s, st_in,
                                       mixer_consts + [gm_w_s[l], bsb_s])
    x1s = _outln_call(xn_s, ymix_s, mod_ls, None, True, w_out_b,
                      r1(ln_mix_g[l]), r1(ln_mix_b[l]), tm=512)
    ys = _ffn_call(x1s, mod_ls, None, True, w1_t, w2_b, r1(ln_ffn_g[l]), r1(ln_ffn_b[l]), tm=512)
    conv_s = xbcs.reshape(bs, dec, CONV_DIM)[:, dec - (CONV_K - 1):, :]

    return (yp.reshape(bp, seq, D_MODEL),
            ys.reshape(bs, dec, D_MODEL),
            ssm_p.reshape(1, bp, SSD_HEADS, SSD_HEADDIM, SSD_STATE),
            conv_p[None],
            ssm_s.reshape(1, bs, SSD_HEADS, SSD_HEADDIM, SSD_STATE),
            conv_s[None],
            vn_s.reshape(1, bs, dec, GM_WIDTH))
```
